```python
import math
import jax, jax.numpy as jnp
from jax import lax
import numpy as np

D_MODEL = 1024
BATCH = 32
SEQ = 256
DEPTH = 4
DEC_BATCH = 8
DEC_SEQ = 2048
PAST_LEN = 512

GRID_W = 64
N_MIXERS = 3
N_LAYERS_A = (DEPTH + 2) // 3
N_LAYERS_B = (DEPTH + 1) // 3
N_LAYERS_C = DEPTH // 3
H_A = 8
HD_A = D_MODEL // (2 * H_A)
H_B = 16
KV_B = 4
G_B = H_B // KV_B
HD_B = D_MODEL // H_B
WINDOW = 128
QBLK = 128
D_RNN = D_MODEL
N_BLK_C = 16
BW_C = D_RNN // N_BLK_C
CONV_W = 4
CONV_LEFT = CONV_W // 2
LRU_C = 8.0
N_EXPERTS = 16
EC_FACTOR = 2
D_EXPERT = 1024
ROPE_THETA = 10000.0
EPS = 1e-6
NEG = -1e30

kernel_name = "hybrid_diffusion_trunk_step"


def rmsnorm(x, g):
    xf = x.astype(jnp.float32)
    y = xf * lax.rsqrt(jnp.mean(xf * xf, axis=-1, keepdims=True) + EPS)
    return (y * g.astype(jnp.float32)).astype(x.dtype)


def rope_2d(x):
    t, hd = x.shape[1], x.shape[-1]
    rows = t // GRID_W
    pos_row = jnp.repeat(jnp.arange(rows), GRID_W)
    pos_col = jnp.tile(jnp.arange(GRID_W), rows)
    half = hd // 2
    n_freq = half // 2
    inv_freq = ROPE_THETA ** (-jnp.arange(n_freq, dtype=jnp.float32) / n_freq)
    shape = (t,) + (1,) * (x.ndim - 3) + (n_freq,)

    def rot(xa, pos):
        ang = pos.astype(jnp.float32)[:, None] * inv_freq[None, :]
        cos = jnp.cos(ang).reshape(shape)
        sin = jnp.sin(ang).reshape(shape)
        x1 = xa[..., :n_freq].astype(jnp.float32)
        x2 = xa[..., n_freq:].astype(jnp.float32)
        return jnp.concatenate([x1 * cos - x2 * sin, x1 * sin + x2 * cos], axis=-1)

    out = jnp.concatenate([rot(x[..., :half], pos_row), rot(x[..., half:], pos_col)], axis=-1)
    return out.astype(x.dtype)


def to_blocks(x):
    b, t = x.shape[:2]
    return jnp.moveaxis(x.reshape((b, t // QBLK, QBLK) + x.shape[2:]), 1, 0)


def from_blocks(y):
    y = jnp.moveaxis(y, 0, 1)
    return y.reshape((y.shape[0], y.shape[1] * y.shape[2]) + y.shape[3:])


def diff_attention(q, k, v, lam):
    scale = HD_A ** -0.5

    def block(qb):
        s = jnp.einsum('bqhcd,bkhcd->bhcqk', qb, k).astype(jnp.float32) * scale
        p = jax.nn.softmax(s, axis=-1)
        w = p[:, :, 0] - lam * p[:, :, 1]
        return jnp.einsum('bhqk,bkhe->bqhe', w.astype(v.dtype), v)

    return from_blocks(lax.map(block, to_blocks(q)))


def mixer_a(h, w_in, q_g, k_g, lam_q, lam_k, subln_g, w_out, lam_init, cache):
    b, t, _ = h.shape
    q, k, v = jnp.split(h @ w_in, 3, axis=-1)
    q = rmsnorm(q.reshape(b, t, H_A, 2, HD_A), q_g)
    k = rmsnorm(k.reshape(b, t, H_A, 2, HD_A), k_g)
    v = v.reshape(b, t, H_A, 2 * HD_A)
    lam = (jnp.exp(jnp.sum(lam_q[0] * lam_k[0]).astype(jnp.float32))
           - jnp.exp(jnp.sum(lam_q[1] * lam_k[1]).astype(jnp.float32)) + lam_init)
    if cache is None:
        o = diff_attention(q, k, v, lam)
        new = (k.reshape(b, t, H_A, 2 * HD_A), v)
    else:
        ck, cv = cache
        keys = jnp.concatenate([rope_2d(k), ck.reshape(b, -1, H_A, 2, HD_A)], axis=1)
        vals = jnp.concatenate([v, cv], axis=1)
        o = diff_attention(rope_2d(q), keys, vals, lam)
        new = None
    o = rmsnorm(o, subln_g) * (1.0 - lam_init)
    return o.reshape(b, t, H_A * 2 * HD_A) @ w_out, new


def sink_column(sink, b):
    return jnp.broadcast_to(sink.reshape(1, KV_B, G_B, 1, 1).astype(jnp.float32), (b, KV_B, G_B, QBLK, 1))


def gqa_sink_dense(q, k, v, sink):
    scale = HD_B ** -0.5
    b = q.shape[0]

    def block(qb):
        s = jnp.einsum('bqkgd,bskd->bkgqs', qb, k).astype(jnp.float32) * scale
        p = jax.nn.softmax(jnp.concatenate([s, sink_column(sink, b)], axis=-1), axis=-1)[..., :-1]
        return jnp.einsum('bkgqs,bskd->bqkgd', p.astype(v.dtype), v)

    return from_blocks(lax.map(block, to_blocks(q)))


def window_attention_latent(q, k, v, ck, cv, sink):
    scale = HD_B ** -0.5
    b, t = q.shape[:2]
    nb = t // QBLK
    span = QBLK + 2 * WINDOW
    pad = ((0, 0), (WINDOW, WINDOW), (0, 0), (0, 0))
    kp, vp = jnp.pad(k, pad), jnp.pad(v, pad)
    offs = jnp.arange(span) - WINDOW
    tc = ck.shape[1]

    def block(args):
        qb, j = args
        kb = lax.dynamic_slice_in_dim(kp, j * QBLK, span, axis=1)
        vb = lax.dynamic_slice_in_dim(vp, j * QBLK, span, axis=1)
        qpos = j * QBLK + jnp.arange(QBLK)
        kpos = j * QBLK + offs
        mask = (jnp.abs(qpos[:, None] - kpos[None, :]) <= WINDOW) & (kpos >= 0)[None, :] & (kpos < t)[None, :]
        s_loc = jnp.einsum('bqkgd,bskd->bkgqs', qb, kb).astype(jnp.float32) * scale
        s_loc = jnp.where(mask, s_loc, NEG)
        s_ctx = jnp.einsum('bqkgd,bskd->bkgqs', qb, ck).astype(jnp.float32) * scale
        p = jax.nn.softmax(jnp.concatenate([s_loc, s_ctx, sink_column(sink, b)], axis=-1), axis=-1)
        p = p.astype(v.dtype)
        return (jnp.einsum('bkgqs,bskd->bqkgd', p[..., :span], vb)
                + jnp.einsum('bkgqs,bskd->bqkgd', p[..., span:span + tc], cv))

    return from_blocks(lax.map(block, (to_blocks(q), jnp.arange(nb))))


def mixer_b(h, w_in, q_g, k_g, sink, w_out, cache):
    b, t, _ = h.shape
    qkv = h @ w_in
    q = rmsnorm(qkv[..., :H_B * HD_B].reshape(b, t, KV_B, G_B, HD_B), q_g)
    k = rmsnorm(qkv[..., H_B * HD_B:(H_B + KV_B) * HD_B].reshape(b, t, KV_B, HD_B), k_g)
    v = qkv[..., (H_B + KV_B) * HD_B:].reshape(b, t, KV_B, HD_B)
    if cache is None:
        o = gqa_sink_dense(q, k, v, sink)
        new = (k, v)
    else:
        o = window_attention_latent(rope_2d(q), rope_2d(k), v, cache[0], cache[1], sink)
        new = None
    return o.reshape(b, t, H_B * HD_B) @ w_out, new


def centred_conv(x, w, bias):
    t = x.shape[1]
    xp = jnp.pad(x, ((0, 0), (CONV_LEFT, CONV_W - 1 - CONV_LEFT), (0, 0)))
    return sum(xp[:, i:i + t] * w[i] for i in range(CONV_W)) + bias


def rg_lru(x, w_rg, b_rg, w_ig, b_ig, lam, h0, reverse):
    b, t, _ = x.shape
    xb = x.reshape(b, t, N_BLK_C, BW_C)
    r = jax.nn.sigmoid((jnp.einsum('btnk,nkj->btnj', xb, w_rg).reshape(b, t, D_RNN) + b_rg).astype(jnp.float32))
    i = jax.nn.sigmoid((jnp.einsum('btnk,nkj->btnj', xb, w_ig).reshape(b, t, D_RNN) + b_ig).astype(jnp.float32))
    log_a = -LRU_C * r * jax.nn.softplus(-lam.astype(jnp.float32))
    a = jnp.exp(log_a)
    u = jnp.sqrt(-jnp.expm1(2.0 * log_a)) * i * x.astype(jnp.float32)

    def combine(e1, e2):
        a1, b1 = e1
        a2, b2 = e2
        return a1 * a2, a2 * b1 + b2

    acum, bcum = lax.associative_scan(combine, (a, u), axis=1, reverse=reverse)
    hs = acum * h0[:, None, :] + bcum
    final = hs[:, 0] if reverse else hs[:, -1]
    return hs, final


def mixer_c(h, w_in, conv_w, conv_b, w_rg, b_rg, w_ig, b_ig, lam, w_out, state):
    b = h.shape[0]
    gate, xr = jnp.split(h @ w_in, 2, axis=-1)
    xr = centred_conv(xr, conv_w, conv_b)
    if state is None:
        h0f = jnp.zeros((b, D_RNN), jnp.float32)
        h0b = jnp.zeros((b, D_RNN), jnp.float32)
    else:
        h0f = state[:, 0].astype(jnp.float32)
        h0b = state[:, 1].astype(jnp.float32)
    hf, ff = rg_lru(xr, w_rg[0], b_rg[0], w_ig[0], b_ig[0], lam[0], h0f, False)
    hb, fb = rg_lru(xr, w_rg[1], b_rg[1], w_ig[1], b_ig[1], lam[1], h0b, True)
    y = jax.nn.gelu(gate) * (hf + hb).astype(h.dtype)
    new = jnp.stack([ff, fb], axis=1).astype(h.dtype) if state is None else None
    return y @ w_out, new


def ec_moe(h, w_router, w_gate, w_up, w_down):
    b, t, d = h.shape
    cap = EC_FACTOR * t // N_EXPERTS
    aff = jax.nn.softmax((h @ w_router).astype(jnp.float32), axis=-1)
    g, idx = lax.top_k(jnp.swapaxes(aff, 1, 2), cap)
    xs = jax.vmap(lambda hb, ib: hb[ib])(h, idx)
    hid = jax.nn.silu(jnp.einsum('becd,edf->becf', xs, w_gate)) * jnp.einsum('becd,edf->becf', xs, w_up)
    ys = jnp.einsum('becf,efd->becd', hid, w_down) * g[..., None].astype(h.dtype)
    return jax.vmap(lambda yb, ib: jax.ops.segment_sum(yb.reshape(-1, d), ib.reshape(-1), num_segments=t))(ys, idx)


def apply_layer(x, mod, g_mix, g_ffn, mixer, moe_w):
    sh1, sc1, ga1, sh2, sc2, ga2 = jnp.split(mod, 6, axis=-1)
    y, new = mixer(rmsnorm(x, g_mix) * (1.0 + sc1) + sh1)
    x = x + ga1 * y
    x = x + ga2 * ec_moe(rmsnorm(x, g_ffn) * (1.0 + sc2) + sh2, *moe_w)
    return x, new


def setup_inputs(seed: int = 0) -> dict:
    key = jax.random.key(seed)
    ks = iter(jax.random.split(key, 64))
    D = D_MODEL

    def nrm(shape, scale=1.0):
        return jax.random.normal(next(ks), shape, jnp.float32) * scale

    def gain(shape):
        return 1.0 + 0.01 * nrm(shape)

    inp = {}
    inp["x_prompt"] = nrm((BATCH, SEQ, D))
    inp["x_sample"] = nrm((DEC_BATCH, DEC_SEQ, D))
    inp["cache_a_k"] = nrm((DEC_BATCH, N_LAYERS_A, PAST_LEN, H_A, 2 * HD_A))
    inp["cache_a_v"] = nrm((DEC_BATCH, N_LAYERS_A, PAST_LEN, H_A, 2 * HD_A))
    inp["cache_b_k"] = nrm((DEC_BATCH, N_LAYERS_B, PAST_LEN, KV_B, HD_B))
    inp["cache_b_v"] = nrm((DEC_BATCH, N_LAYERS_B, PAST_LEN, KV_B, HD_B))
    inp["state_c_h"] = nrm((DEC_BATCH, N_LAYERS_C, 2, D_RNN), 0.5)
    inp["c"] = nrm((DEC_BATCH, D))
    inp["c_ctx"] = nrm((D,))
    inp["ada_w"] = nrm((DEPTH, D, 6 * D), 0.5 * D ** -0.5)
    inp["ada_b"] = nrm((DEPTH, 6 * D), 0.01)
    inp["norm_mix_g"] = gain((DEPTH, D))
    inp["norm_ffn_g"] = gain((DEPTH, D))
    inp["a_w_in"] = nrm((N_LAYERS_A, D, 3 * D), D ** -0.5)
    inp["a_q_norm_g"] = gain((N_LAYERS_A, HD_A))
    inp["a_k_norm_g"] = gain((N_LAYERS_A, HD_A))
    inp["a_lam_q"] = nrm((N_LAYERS_A, 2, HD_A), 0.1)
    inp["a_lam_k"] = nrm((N_LAYERS_A, 2, HD_A), 0.1)
    inp["a_subln_g"] = gain((N_LAYERS_A, 2 * HD_A))
    inp["a_w_out"] = nrm((N_LAYERS_A, D, D), D ** -0.5)
    inp["b_w_in"] = nrm((N_LAYERS_B, D, (H_B + 2 * KV_B) * HD_B), D ** -0.5)
    inp["b_q_norm_g"] = gain((N_LAYERS_B, HD_B))
    inp["b_k_norm_g"] = gain((N_LAYERS_B, HD_B))
    inp["b_sink"] = nrm((N_LAYERS_B, H_B), 0.5)
    inp["b_w_out"] = nrm((N_LAYERS_B, H_B * HD_B, D), (H_B * HD_B) ** -0.5)
    inp["c_w_in"] = nrm((N_LAYERS_C, D, 2 * D_RNN), D ** -0.5)
    inp["c_conv_w"] = nrm((N_LAYERS_C, CONV_W, D_RNN), CONV_W ** -0.5)
    inp["c_conv_b"] = nrm((N_LAYERS_C, D_RNN), 0.01)
    inp["c_w_rg"] = nrm((N_LAYERS_C, 2, N_BLK_C, BW_C, BW_C), BW_C ** -0.5)
    inp["c_b_rg"] = nrm((N_LAYERS_C, 2, D_RNN), 0.01)
    inp["c_w_ig"] = nrm((N_LAYERS_C, 2, N_BLK_C, BW_C, BW_C), BW_C ** -0.5)
    inp["c_b_ig"] = nrm((N_LAYERS_C, 2, D_RNN), 0.01)
    u = jax.random.uniform(next(ks), (N_LAYERS_C, 2, D_RNN), jnp.float32, 0.9, 0.999)
    a = u ** (1.0 / LRU_C)
    inp["c_lam"] = jnp.log(a) - jnp.log1p(-a)
    inp["c_w_out"] = nrm((N_LAYERS_C, D_RNN, D), D_RNN ** -0.5)
    inp["moe_w_router"] = nrm((DEPTH, D, N_EXPERTS), D ** -0.5)
    inp["moe_w_gate"] = nrm((DEPTH, N_EXPERTS, D, D_EXPERT), D ** -0.5)
    inp["moe_w_up"] = nrm((DEPTH, N_EXPERTS, D, D_EXPERT), D ** -0.5)
    inp["moe_w_down"] = nrm((DEPTH, N_EXPERTS, D_EXPERT, D), D_EXPERT ** -0.5)
    return inp


def reference(x_prompt, x_sample, cache_a_k, cache_a_v, cache_b_k, cache_b_v, state_c_h, c, c_ctx,
              ada_w, ada_b, norm_mix_g, norm_ffn_g,
              a_w_in, a_q_norm_g, a_k_norm_g, a_lam_q, a_lam_k, a_subln_g, a_w_out,
              b_w_in, b_q_norm_g, b_k_norm_g, b_sink, b_w_out,
              c_w_in, c_conv_w, c_conv_b, c_w_rg, c_b_rg, c_w_ig, c_b_ig, c_lam, c_w_out,
              moe_w_router, moe_w_gate, moe_w_up, moe_w_down):
    xp, xs = x_prompt, x_sample
    new_a_k, new_a_v, new_b_k, new_b_v, new_c_h = [], [], [], [], []
    ia = ib = ic = 0
    for l in range(DEPTH):
        mod_p = jax.nn.silu(c_ctx) @ ada_w[l] + ada_b[l]
        mod_s = (jax.nn.silu(c) @ ada_w[l] + ada_b[l])[:, None, :]
        moe_w = (moe_w_router[l], moe_w_gate[l], moe_w_up[l], moe_w_down[l])
        kind = l % N_MIXERS
        if kind == 0:
            lam_init = 0.8 - 0.6 * math.exp(-0.3 * l)
            pa = (a_w_in[ia], a_q_norm_g[ia], a_k_norm_g[ia], a_lam_q[ia], a_lam_k[ia], a_subln_g[ia], a_w_out[ia], lam_init)
            cache = (cache_a_k[:, ia], cache_a_v[:, ia])
            xp, (kn, vn) = apply_layer(xp, mod_p, norm_mix_g[l], norm_ffn_g[l], lambda h: mixer_a(h, *pa, None), moe_w)
            xs, _ = apply_layer(xs, mod_s, norm_mix_g[l], norm_ffn_g[l], lambda h: mixer_a(h, *pa, cache), moe_w)
            new_a_k.append(kn)
            new_a_v.append(vn)
            ia += 1
        elif kind == 1:
            pb = (b_w_in[ib], b_q_norm_g[ib], b_k_norm_g[ib], b_sink[ib], b_w_out[ib])
            cache = (cache_b_k[:, ib], cache_b_v[:, ib])
            xp, (kn, vn) = apply_layer(xp, mod_p, norm_mix_g[l], norm_ffn_g[l], lambda h: mixer_b(h, *pb, None), moe_w)
            xs, _ = apply_layer(xs, mod_s, norm_mix_g[l], norm_ffn_g[l], lambda h: mixer_b(h, *pb, cache), moe_w)
            new_b_k.append(kn)
            new_b_v.append(vn)
            ib += 1
        else:
            pc = (c_w_in[ic], c_conv_w[ic], c_conv_b[ic], c_w_rg[ic], c_b_rg[ic], c_w_ig[ic], c_b_ig[ic], c_lam[ic], c_w_out[ic])
            st = state_c_h[:, ic]
            xp, hn = apply_layer(xp, mod_p, norm_mix_g[l], norm_ffn_g[l], lambda h: mixer_c(h, *pc, None), moe_w)
            xs, _ = apply_layer(xs, mod_s, norm_mix_g[l], norm_ffn_g[l], lambda h: mixer_c(h, *pc, st), moe_w)
            new_c_h.append(hn)
            ic += 1
    return (xp, xs, jnp.stack(new_a_k, axis=1), jnp.stack(new_a_v, axis=1),
            jnp.stack(new_b_k, axis=1), jnp.stack(new_b_v, axis=1), jnp.stack(new_c_h, axis=1))
```

```python
import functools
import math

import jax
import jax.numpy as jnp
from jax import lax
from jax.experimental import pallas as pl
from jax.experimental.pallas import tpu as pltpu

F32, BF16, I32 = jnp.float32, jnp.bfloat16, jnp.int32

D = 1024
DEPTH = 4
GRID_W = 64
H_A, HD_A = 8, 64
H_B, KV_B, G_B, HD_B = 16, 4, 4, 64
WINDOW = 128
D_RNN = 1024
N_BLK_C, BW_C = 16, 64
LRU_C = 8.0
N_EXPERTS = 16
EC_FACTOR = 2
ROPE_THETA = 10000.0
EPS = 1e-6
LANES = 128
MXU_DIM = 256
MIB = 1024 * 1024


def _cparams(n_axes, vmem_mib=48):
    return pltpu.CompilerParams(dimension_semantics=("arbitrary",) * n_axes,
                                vmem_limit_bytes=vmem_mib * MIB)


def _sigmoid(x):
    return 1.0 / (1.0 + jnp.exp(-x))


def _rms(x):
    return x * lax.rsqrt(jnp.mean(x * x, axis=-1, keepdims=True) + EPS)


def _dot(a, b):
    return jnp.dot(a, b, preferred_element_type=F32)


def _dot_nt(a, b):
    return lax.dot_general(a, b, (((1,), (1,)), ((), ())), preferred_element_type=F32)


def _mod_kernel(c_ref, w_ref, b_ref, o_ref):
    c = c_ref[...]
    s = (c * _sigmoid(c)).astype(BF16)
    o_ref[...] = _dot(s, w_ref[...].astype(BF16)) + b_ref[...]


def _modulation(cvec, ada_w, ada_b):
    nt = 1536
    return pl.pallas_call(
        _mod_kernel,
        grid=(DEPTH, 6 * D // nt),
        in_specs=[pl.BlockSpec((16, D), lambda l, j: (0, 0)),
                  pl.BlockSpec((None, D, nt), lambda l, j: (l, 0, j)),
                  pl.BlockSpec((None, 1, nt), lambda l, j: (l, 0, j))],
        out_specs=pl.BlockSpec((None, 16, nt), lambda l, j: (l, 0, j)),
        out_shape=jax.ShapeDtypeStruct((DEPTH, 16, 6 * D), F32),
        compiler_params=_cparams(2, 32),
        name="adaln_mod",
    )(cvec, ada_w, ada_b.reshape(DEPTH, 1, 6 * D))


def _nmm_kernel(x_ref, mod_ref, g_ref, w_ref, o_ref, wbf_ref):
    @pl.when(pl.program_id(0) == 0)
    def _():
        wbf_ref[...] = w_ref[...].astype(BF16)

    m = mod_ref[...]
    h = (_rms(x_ref[...]) * g_ref[...] * (1.0 + m[:, D:2 * D]) + m[:, 0:D]).astype(BF16)
    o_ref[...] = _dot(h, wbf_ref[...])


def _norm_mod_matmul(x, mod3, mod_base, rows_per_mod, g, w, tm=256):
    r, n = x.shape[0], w.shape[1]
    return pl.pallas_call(
        _nmm_kernel,
        grid=(r // tm,),
        in_specs=[pl.BlockSpec((tm, D), lambda i: (i, 0)),
                  pl.BlockSpec((None, 1, 6 * D), lambda i: (mod_base + (i * tm) // rows_per_mod, 0, 0)),
                  pl.BlockSpec((1, D), lambda i: (0, 0)),
                  pl.BlockSpec((D, n), lambda i: (0, 0), pipeline_mode=pl.Buffered(1))],
        out_specs=pl.BlockSpec((tm, n), lambda i: (i, 0)),
        out_shape=jax.ShapeDtypeStruct((r, n), F32),
        scratch_shapes=[pltpu.VMEM((D, n), BF16)],
        compiler_params=_cparams(1, 48),
        name="norm_mod_matmul",
    )(x, mod3, g.reshape(1, D), w)


def _proj_tail(y, x_ref, mod_ref, g2_ref, xo_ref, h2_ref):
    m = mod_ref[...]
    xn = x_ref[...] + m[:, 2 * D:3 * D] * y
    xo_ref[...] = xn
    h2_ref[...] = (_rms(xn) * g2_ref[...] * (1.0 + m[:, 4 * D:5 * D]) + m[:, 3 * D:4 * D]).astype(BF16)


def _proj_kernel(o_ref, x_ref, mod_ref, g2_ref, w_ref, xo_ref, h2_ref, wbf_ref):
    @pl.when(pl.program_id(0) == 0)
    def _():
        wbf_ref[...] = w_ref[...].astype(BF16)

    _proj_tail(_dot(o_ref[...], wbf_ref[...]), x_ref, mod_ref, g2_ref, xo_ref, h2_ref)


def _proj_residual(o, x, mod3, mod_base, rows_per_mod, g2, w, tm=256):
    r = x.shape[0]
    return pl.pallas_call(
        _proj_kernel,
        grid=(r // tm,),
        in_specs=[pl.BlockSpec((tm, D), lambda i: (i, 0)),
                  pl.BlockSpec((tm, D), lambda i: (i, 0)),
                  pl.BlockSpec((None, 1, 6 * D), lambda i: (mod_base + (i * tm) // rows_per_mod, 0, 0)),
                  pl.BlockSpec((1, D), lambda i: (0, 0)),
                  pl.BlockSpec((D, D), lambda i: (0, 0))],
        out_specs=[pl.BlockSpec((tm, D), lambda i: (i, 0)), pl.BlockSpec((tm, D), lambda i: (i, 0))],
        out_shape=[jax.ShapeDtypeStruct((r, D), F32), jax.ShapeDtypeStruct((r, D), BF16)],
        scratch_shapes=[pltpu.VMEM((D, D), BF16)],
        compiler_params=_cparams(1, 40),
        name="proj_residual",
    )(o, x, mod3, g2.reshape(1, D), w)


def _group_mean_sq(x):
    ss = x * x
    hi = ss.astype(BF16)
    lo = (ss - hi.astype(F32)).astype(BF16)
    r = lax.broadcasted_iota(I32, (LANES, LANES), 0) >> 6
    c = lax.broadcasted_iota(I32, (LANES, LANES), 1) >> 6
    ones_bd = jnp.where(r == c, 1.0, 0.0).astype(BF16)
    return (_dot(hi, ones_bd) + _dot(lo, ones_bd)) * (1.0 / HD_A)


def _rope128(y, cos, sa, sb):
    return y * cos + pltpu.roll(y, LANES - 16, 1) * sa + pltpu.roll(y, 16, 1) * sb


def _prep_kernel(*refs, nq, nk, rope, k_f32):
    it = iter(refs)
    qkv_ref, qg_ref, kg_ref = next(it), next(it), next(it)
    if rope:
        cos_ref, sa_ref, sb_ref = next(it), next(it), next(it)
    q_ref, k_ref, v_ref = next(it), next(it), next(it)
    kf_ref = next(it) if k_f32 else None
    if rope:
        cos, sa, sb = cos_ref[...], sa_ref[...], sb_ref[...]
    for j in range((nq + nk) // LANES):
        x = qkv_ref[:, j * LANES:(j + 1) * LANES]
        is_q = j < nq // LANES
        y = x * lax.rsqrt(_group_mean_sq(x) + EPS) * (qg_ref[...] if is_q else kg_ref[...])
        if rope:
            y = _rope128(y, cos, sa, sb)
        if is_q:
            q_ref[:, j * LANES:(j + 1) * LANES] = (y * (HD_A ** -0.5)).astype(BF16)
        else:
            jj = j - nq // LANES
            k_ref[:, jj * LANES:(jj + 1) * LANES] = y.astype(BF16)
            if k_f32:
                kf_ref[:, jj * LANES:(jj + 1) * LANES] = y
    v_ref[...] = qkv_ref[:, nq + nk:].astype(BF16)


def _rope_tables(t):
    n_freq = HD_A // 4
    inv_freq = ROPE_THETA ** (-jnp.arange(n_freq, dtype=F32) / n_freq)
    pos_row = jnp.repeat(jnp.arange(t // GRID_W), GRID_W).astype(F32)
    pos_col = jnp.tile(jnp.arange(GRID_W), t // GRID_W).astype(F32)
    ang_r = pos_row[:, None] * inv_freq[None, :]
    ang_c = pos_col[:, None] * inv_freq[None, :]
    z = jnp.zeros_like(ang_r)
    cos64 = jnp.concatenate([jnp.cos(ang_r), jnp.cos(ang_r), jnp.cos(ang_c), jnp.cos(ang_c)], axis=-1)
    sa64 = jnp.concatenate([-jnp.sin(ang_r), z, -jnp.sin(ang_c), z], axis=-1)
    sb64 = jnp.concatenate([z, jnp.sin(ang_r), z, jnp.sin(ang_c)], axis=-1)
    return tuple(jnp.tile(a, (1, 2)) for a in (cos64, sa64, sb64))


def _qk_prep(qkv, qg, kg, nq, nk, nv, t, rope, k_f32, tm=256):
    r = qkv.shape[0]
    tile2 = lambda g: jnp.tile(g.reshape(1, HD_A), (1, 2))
    args = [qkv, tile2(qg), tile2(kg)]
    in_specs = [pl.BlockSpec((tm, nq + nk + nv), lambda i: (i, 0)),
                pl.BlockSpec((1, LANES), lambda i: (0, 0)),
                pl.BlockSpec((1, LANES), lambda i: (0, 0))]
    if rope:
        args += list(_rope_tables(t))
        in_specs += [pl.BlockSpec((tm, LANES), lambda i: (i % (t // tm), 0))] * 3
    out_shape = [jax.ShapeDtypeStruct((r, nq), BF16), jax.ShapeDtypeStruct((r, nk), BF16),
                 jax.ShapeDtypeStruct((r, nv), BF16)]
    out_specs = [pl.BlockSpec((tm, nq), lambda i: (i, 0)), pl.BlockSpec((tm, nk), lambda i: (i, 0)),
                 pl.BlockSpec((tm, nv), lambda i: (i, 0))]
    if k_f32:
        out_shape.append(jax.ShapeDtypeStruct((r, nk), F32))
        out_specs.append(pl.BlockSpec((tm, nk), lambda i: (i, 0)))
    return pl.pallas_call(
        functools.partial(_prep_kernel, nq=nq, nk=nk, rope=rope, k_f32=k_f32),
        grid=(r // tm,),
        in_specs=in_specs, out_specs=out_specs, out_shape=out_shape,
        compiler_params=_cparams(1, 32),
        name="qk_prep",
    )(*args)


def _diff_attn_kernel(*refs, heads, ctx, lam_init):
    it = iter(refs)
    q_ref, k_ref, v_ref = next(it), next(it), next(it)
    if ctx:
        ck_ref, cv_ref = next(it), next(it)
    lq_ref, lk_ref, sg_ref, o_ref = next(it), next(it), next(it), next(it)
    e = jnp.exp(jnp.sum(lq_ref[...] * lk_ref[...], axis=-1, keepdims=True))
    lam = e[0:1, :] - e[1:2, :] + lam_init
    lane = lax.broadcasted_iota(I32, (1, LANES), 1)
    for h in range(heads):
        sl = slice(h * LANES, (h + 1) * LANES)
        q, k, v = q_ref[:, sl], k_ref[:, sl], v_ref[:, sl]
        if ctx:
            ck, cv = ck_ref[:, sl].astype(BF16), cv_ref[:, sl].astype(BF16)
        zero = jnp.zeros_like(q)
        w_lat, w_ctx = None, None
        for c in range(2):
            qc = jnp.where((lane < HD_A) if c == 0 else (lane >= HD_A), q, zero)
            s = _dot_nt(qc, k)
            m = jnp.max(s, axis=-1, keepdims=True)
            if ctx:
                sc = _dot_nt(qc, ck)
                m = jnp.maximum(m, jnp.max(sc, axis=-1, keepdims=True))
            p = jnp.exp(s - m)
            l = jnp.sum(p, axis=-1, keepdims=True)
            if ctx:
                pc = jnp.exp(sc - m)
                l = l + jnp.sum(pc, axis=-1, keepdims=True)
            coef = (1.0 / l) if c == 0 else (-lam / l)
            w_lat = p * coef if c == 0 else w_lat + p * coef
            if ctx:
                w_ctx = pc * coef if c == 0 else w_ctx + pc * coef
        o = _dot(w_lat.astype(BF16), v)
        if ctx:
            o = o + _dot(w_ctx.astype(BF16), cv)
        o_ref[:, sl] = (_rms(o) * sg_ref[...] * (1.0 - lam_init)).astype(BF16)


def _diff_attention(q, k, v, cache, lam_q, lam_k, subln_g, lam_init, b, t, heads, tq):
    nq = t // tq
    hb = H_A // heads
    args = [q, k, v]
    in_specs = [pl.BlockSpec((tq, heads * LANES), lambda bi, hi, qi: (bi * nq + qi, hi)),
                pl.BlockSpec((t, heads * LANES), lambda bi, hi, qi: (bi, hi)),
                pl.BlockSpec((t, heads * LANES), lambda bi, hi, qi: (bi, hi))]
    if cache is not None:
        ck, cv, la = cache
        p = ck.shape[2]
        args += [ck, cv]
        in_specs += [pl.BlockSpec((None, None, p, heads * LANES), lambda bi, hi, qi: (bi, la, 0, hi))] * 2
    args += [lam_q, lam_k, subln_g.reshape(1, LANES)]
    in_specs += [pl.BlockSpec((2, HD_A), lambda bi, hi, qi: (0, 0)),
                 pl.BlockSpec((2, HD_A), lambda bi, hi, qi: (0, 0)),
                 pl.BlockSpec((1, LANES), lambda bi, hi, qi: (0, 0))]
    return pl.pallas_call(
        functools.partial(_diff_attn_kernel, heads=heads, ctx=cache is not None, lam_init=lam_init),
        grid=(b, hb, nq),
        in_specs=in_specs,
        out_specs=pl.BlockSpec((tq, heads * LANES), lambda bi, hi, qi: (bi * nq + qi, hi)),
        out_shape=jax.ShapeDtypeStruct((b * t, H_A * LANES), BF16),
        compiler_params=_cparams(3, 48),
        name="diff_attention",
    )(*args)


def _both_halves(x, half):
    lane_half = lax.broadcasted_iota(I32, (1, LANES), 1) >> 6
    xm = jnp.where(lane_half == half, x, 0.0)
    return xm + pltpu.roll(xm, HD_B, 1)


def _gqa_kernel(*refs, kvs, windowed, ctx, t, tq):
    it = iter(refs)
    q_ref, k_ref, v_ref = next(it), next(it), next(it)
    if ctx:
        ck_ref, cv_ref = next(it), next(it)
    sink_ref, o_ref = next(it), next(it)
    lane_half = lax.broadcasted_iota(I32, (1, LANES), 1) >> 6
    if windowed:
        span = tq + 2 * WINDOW
        q0 = pl.program_id(2) * tq
        start = pl.multiple_of(jnp.clip(q0 - WINDOW, 0, t - span), LANES)
        qpos = q0 + lax.broadcasted_iota(I32, (tq, span), 0)
        kpos = start + lax.broadcasted_iota(I32, (tq, span), 1)
        valid = jnp.abs(qpos - kpos) <= WINDOW
    for kv in range(kvs):
        if kvs == 1:
            half = pl.program_id(1) % 2
            kcol = slice(0, LANES)
        else:
            half = kv % 2
            kcol = slice((kv // 2) * LANES, (kv // 2 + 1) * LANES)
        if windowed:
            kw = k_ref[pl.ds(start, span), kcol]
            vw = v_ref[pl.ds(start, span), kcol]
        else:
            kw, vw = k_ref[:, kcol], v_ref[:, kcol]
        kd = _both_halves(kw.astype(F32), half).astype(BF16)
        vd = _both_halves(vw.astype(F32), half).astype(BF16)
        if ctx:
            ckd = _both_halves(ck_ref[:, kcol], half).astype(BF16)
            cvd = _both_halves(cv_ref[:, kcol], half).astype(BF16)
        for gp in range(G_B // 2):
            qcol = slice((kv * G_B // 2 + gp) * LANES, (kv * G_B // 2 + gp + 1) * LANES)
            q = q_ref[:, qcol]
            o_pair = None
            for gh in range(2):
                g = gp * 2 + gh
                qm = jnp.where(lane_half == gh, q, jnp.zeros_like(q))
                sink = sink_ref[kv:kv + 1, g:g + 1] if kvs > 1 else sink_ref[:, g:g + 1]
                s = _dot_nt(qm, kd)
                if windowed:
                    s = jnp.where(valid, s, -jnp.inf)
                m = jnp.maximum(jnp.max(s, axis=-1, keepdims=True), sink)
                if ctx:
                    sc = _dot_nt(qm, ckd)
                    m = jnp.maximum(m, jnp.max(sc, axis=-1, keepdims=True))
                p = jnp.exp(s - m)
                l = jnp.sum(p, axis=-1, keepdims=True) + jnp.exp(sink - m)
                if ctx:
                    pc = jnp.exp(sc - m)
                    l = l + jnp.sum(pc, axis=-1, keepdims=True)
                r = 1.0 / l
                o = _dot((p * r).astype(BF16), vd)
                if ctx:
                    o = o + _dot((pc * r).astype(BF16), cvd)
                o = jnp.where(lane_half == gh, o, 0.0)
                o_pair = o if gh == 0 else o_pair + o
            o_ref[:, qcol] = o_pair.astype(BF16)


def _gqa_attention(q, k, v, cache, sink, b, t, windowed, tq):
    if windowed:
        kvs, nq = 1, t // tq
        grid = (b, KV_B, nq)
        qw = G_B * HD_B
        q_spec = pl.BlockSpec((tq, qw), lambda bi, kv, qi: (bi * nq + qi, kv))
        kv_spec = pl.BlockSpec((t, LANES), lambda bi, kv, qi: (bi, kv // 2))
        sink_arr = sink.reshape(KV_B, 1, G_B)
        sink_spec = pl.BlockSpec((None, 1, G_B), lambda bi, kv, qi: (kv, 0, 0))
        o_spec = pl.BlockSpec((tq, qw), lambda bi, kv, qi: (bi * nq + qi, kv))
    else:
        kvs, tq = KV_B, t
        grid = (b, 1, 1)
        q_spec = pl.BlockSpec((t, H_B * HD_B), lambda bi, kv, qi: (bi, 0))
        kv_spec = pl.BlockSpec((t, KV_B * HD_B), lambda bi, kv, qi: (bi, 0))
        sink_arr = sink.reshape(KV_B, G_B)
        sink_spec = pl.BlockSpec((KV_B, G_B), lambda bi, kv, qi: (0, 0))
        o_spec = pl.BlockSpec((t, H_B * HD_B), lambda bi, kv, qi: (bi, 0))
    args, in_specs = [q, k, v], [q_spec, kv_spec, kv_spec]
    if cache is not None:
        ck, cv, lb = cache
        p = ck.shape[2]
        args += [ck, cv]
        in_specs += [pl.BlockSpec((None, None, p, LANES), lambda bi, kv, qi: (bi, lb, 0, kv // 2))] * 2
    args.append(sink_arr)
    in_specs.append(sink_spec)
    return pl.pallas_call(
        functools.partial(_gqa_kernel, kvs=kvs, windowed=windowed, ctx=cache is not None, t=t, tq=tq),
        grid=grid, in_specs=in_specs, out_specs=o_spec,
        out_shape=jax.ShapeDtypeStruct((b * t, H_B * HD_B), BF16),
        compiler_params=_cparams(3, 48),
        name="gqa_attention",
    )(*args)


def _lru_gates_kernel(xp_ref, xc_ref, xn_ref, cw_ref, cb_ref, wg_ref, bg_ref, lam_ref,
                      af_ref, uf_ref, ab_ref, ub_ref, *, tt):
    ti, nt = pl.program_id(1), pl.num_programs(1)
    nl = -lam_ref[...]
    sp = jnp.maximum(nl, 0.0) + jnp.log1p(jnp.exp(-jnp.abs(nl)))
    cur = xc_ref[...]
    prev = xp_ref[...] * (ti > 0).astype(F32)
    nxt = xn_ref[...] * (ti < nt - 1).astype(F32)
    row = lax.broadcasted_iota(I32, (tt, 1), 0)
    xm1 = jnp.where(row == 0, prev[tt - 1:tt, :], pltpu.roll(cur, 1, 0))
    xm2 = jnp.where(row == 0, prev[tt - 2:tt - 1, :],
                    jnp.where(row == 1, prev[tt - 1:tt, :], pltpu.roll(cur, 2, 0)))
    xp1 = jnp.where(row == tt - 1, nxt[0:1, :], pltpu.roll(cur, tt - 1, 0))
    cw = cw_ref[...]
    xc = xm2 * cw[0:1, :] + xm1 * cw[1:2, :] + cur * cw[2:3, :] + xp1 * cw[3:4, :] + cb_ref[...]
    xcb = xc.astype(BF16)
    outs = ((af_ref, uf_ref), (ab_ref, ub_ref))
    for j in range(D_RNN // MXU_DIM):
        cs = slice(j * MXU_DIM, (j + 1) * MXU_DIM)
        z = _dot(xcb[:, cs], wg_ref[j].astype(BF16))
        xj = xc[:, cs]
        for d in range(2):
            r = _sigmoid(z[:, (2 * d) * MXU_DIM:(2 * d + 1) * MXU_DIM] + bg_ref[2 * d:2 * d + 1, cs])
            i = _sigmoid(z[:, (2 * d + 1) * MXU_DIM:(2 * d + 2) * MXU_DIM] + bg_ref[2 * d + 1:2 * d + 2, cs])
            log_a = -LRU_C * r * sp[d:d + 1, cs]
            a = jnp.exp(log_a)
            outs[d][0][:, cs] = a
            outs[d][1][:, cs] = jnp.sqrt(1.0 - a * a) * i * xj


def _lru_gates(gx, conv_w, conv_b, w_rg, b_rg, w_ig, b_ig, lam, b, t, tt=256):
    nt = t // tt
    per_tile = MXU_DIM // BW_C
    eye = jnp.eye(per_tile, dtype=F32)

    def bd(w):
        w4 = w.reshape(N_BLK_C // per_tile, per_tile, BW_C, BW_C)
        return (w4[:, :, :, None, :] * eye[None, :, None, :, None]).reshape(-1, MXU_DIM, MXU_DIM)

    wg = jnp.concatenate([bd(w_rg[0]), bd(w_ig[0]), bd(w_rg[1]), bd(w_ig[1])], axis=-1)
    bg = jnp.stack([b_rg[0], b_ig[0], b_rg[1], b_ig[1]])
    x_spec = lambda f: pl.BlockSpec((tt, D_RNN), f)
    full = lambda shape: pl.BlockSpec(shape, lambda bi, ti: (0,) * len(shape))
    o_spec = pl.BlockSpec((tt, D_RNN), lambda bi, ti: (ti, bi))
    o_shape = jax.ShapeDtypeStruct((t, b * D_RNN), F32)
    return pl.pallas_call(
        functools.partial(_lru_gates_kernel, tt=tt),
        grid=(b, nt),
        in_specs=[x_spec(lambda bi, ti: (bi * nt + jnp.maximum(ti - 1, 0), 1)),
                  x_spec(lambda bi, ti: (bi * nt + ti, 1)),
                  x_spec(lambda bi, ti: (bi * nt + jnp.minimum(ti + 1, nt - 1), 1)),
                  full((4, D_RNN)), full((1, D_RNN)), full((4, MXU_DIM, 4 * MXU_DIM)),
                  full((4, D_RNN)), full((2, D_RNN))],
        out_specs=[o_spec] * 4, out_shape=[o_shape] * 4,
        compiler_params=_cparams(2, 48),
        name="lru_gates",
    )(gx, gx, gx, conv_w, conv_b.reshape(1, D_RNN), wg, bg, lam)


def _scan_kernel(a_ref, u_ref, h0_ref, hs_ref, h_ref, *, b, tc, reverse):
    @pl.when(pl.program_id(0) == 0)
    def _():
        h_ref[...] = h0_ref[...]

    def step(i, h):
        tl = (tc - 1 - i) if reverse else i
        rows = pl.ds(pl.multiple_of(tl * b, 8), b)
        h = a_ref[rows, :] * h + u_ref[rows, :]
        hs_ref[rows, :] = h
        return h

    h_ref[...] = lax.fori_loop(0, tc, step, h_ref[...], unroll=8)


def _lru_scan(a, u, h0, b, t, reverse, rows=512):
    tc = rows // b
    nc = t // tc
    blk = pl.BlockSpec((tc * b, D_RNN), (lambda c: (nc - 1 - c, 0)) if reverse else (lambda c: (c, 0)))
    return pl.pallas_call(
        functools.partial(_scan_kernel, b=b, tc=tc, reverse=reverse),
        grid=(nc,),
        in_specs=[blk, blk, pl.BlockSpec((b, D_RNN), lambda c: (0, 0))],
        out_specs=blk,
        out_shape=jax.ShapeDtypeStruct((t * b, D_RNN), F32),
        scratch_shapes=[pltpu.VMEM((b, D_RNN), F32)],
        compiler_params=_cparams(1, 48),
        name="lru_scan",
    )(a, u, h0)


def _lru_proj_kernel(gate_ref, hf_ref, hb_ref, x_ref, mod_ref, g2_ref, w_ref, xo_ref, h2_ref, wbf_ref):
    @pl.when((pl.program_id(0) == 0) & (pl.program_id(1) == 0))
    def _():
        wbf_ref[...] = w_ref[...].astype(BF16)

    g = gate_ref[...]
    gelu = 0.5 * g * (1.0 + jnp.tanh(math.sqrt(2.0 / math.pi) * (g + 0.044715 * (g * g * g))))
    y = (gelu * (hf_ref[...] + hb_ref[...])).astype(BF16)
    _proj_tail(_dot(y, wbf_ref[...]), x_ref, mod_ref, g2_ref, xo_ref, h2_ref)


def _lru_proj_residual(gx, hf, hb, x, mod3, mod_base, per_batch_mod, g2, w, b, t, tt=256):
    nt = t // tt
    row = lambda bi, ti: (bi * nt + ti, 0)
    return pl.pallas_call(
        _lru_proj_kernel,
        grid=(b, nt),
        in_specs=[pl.BlockSpec((tt, D_RNN), row),
                  pl.BlockSpec((tt, D_RNN), lambda bi, ti: (ti, bi)),
                  pl.BlockSpec((tt, D_RNN), lambda bi, ti: (ti, bi)),
                  pl.BlockSpec((tt, D), row),
                  pl.BlockSpec((None, 1, 6 * D), lambda bi, ti: (mod_base + bi * per_batch_mod, 0, 0)),
                  pl.BlockSpec((1, D), lambda bi, ti: (0, 0)),
                  pl.BlockSpec((D_RNN, D), lambda bi, ti: (0, 0))],
        out_specs=[pl.BlockSpec((tt, D), row), pl.BlockSpec((tt, D), row)],
        out_shape=[jax.ShapeDtypeStruct((b * t, D), F32), jax.ShapeDtypeStruct((b * t, D), BF16)],
        scratch_shapes=[pltpu.VMEM((D_RNN, D), BF16)],
        compiler_params=_cparams(2, 40),
        name="lru_proj_residual",
    )(gx, hf, hb, x, mod3, g2.reshape(1, D), w)


def _select_kernel(h_ref, wr_ref, slot_ref, aff_ref, tri_ref, *, bs, t, cap):
    @pl.when(pl.program_id(0) == 0)
    def _():
        r = lax.broadcasted_iota(I32, (t, t), 0)
        c = lax.broadcasted_iota(I32, (t, t), 1)
        tri_ref[...] = jnp.where(r < c, 1.0, 0.0).astype(BF16)

    wr = wr_ref[...].astype(BF16)
    affs = []
    for s in range(bs):
        logits = _dot_nt(wr, h_ref[s])
        ex = jnp.exp(logits - jnp.max(logits, axis=0, keepdims=True))
        affs.append(ex / jnp.sum(ex, axis=0, keepdims=True))
    aff = jnp.concatenate(affs, axis=0) if bs > 1 else affs[0]
    bits = pltpu.bitcast(aff, I32)
    count = lambda mask: jnp.sum(jnp.where(mask, 1.0, 0.0), axis=-1, keepdims=True)
    th = jnp.zeros((bs * N_EXPERTS, 1), I32)
    for bit in range(30, -1, -1):
        cand = th | (1 << bit)
        th = jnp.where(count(bits >= cand) >= cap, cand, th)
    gt, eq = bits > th, bits == th
    need = cap - count(gt)
    lane = lax.broadcasted_iota(I32, (1, t), 1)
    lim = jnp.zeros((bs * N_EXPERTS, 1), I32)
    for bit in range(t.bit_length() - 1, -1, -1):
        cand = lim | (1 << bit)
        ok = (cand <= t) & (count(eq & (lane < cand)) <= need)
        lim = jnp.where(ok, cand, lim)
    sel = gt | (eq & (lane < lim))
    pos = _dot(jnp.where(sel, 1.0, 0.0).astype(BF16), tri_ref[...])
    slot = jnp.where(sel, pos.astype(I32), -1)
    for s in range(bs):
        slot_ref[s] = slot[s * N_EXPERTS:(s + 1) * N_EXPERTS, :]
        aff_ref[s] = aff[s * N_EXPERTS:(s + 1) * N_EXPERTS, :]


def _moe_select(h2, w_router, b, t, cap, bs):
    return pl.pallas_call(
        functools.partial(_select_kernel, bs=bs, t=t, cap=cap),
        grid=(b // bs,),
        in_specs=[pl.BlockSpec((bs, t, D), lambda i: (i, 0, 0)),
                  pl.BlockSpec((N_EXPERTS, D), lambda i: (0, 0))],
        out_specs=[pl.BlockSpec((bs, N_EXPERTS, t), lambda i: (i, 0, 0))] * 2,
        out_shape=[jax.ShapeDtypeStruct((b, N_EXPERTS, t), I32), jax.ShapeDtypeStruct((b, N_EXPERTS, t), F32)],
        scratch_shapes=[pltpu.VMEM((t, t), BF16)],
        compiler_params=_cparams(1, 48),
        name="moe_select",
    )(h2, w_router.T)


def _ffn_kernel(slot_ref, aff_ref, h_ref, wg_ref, wu_ref, wd_ref, ys_ref,
                wgb_ref, wub_ref, wdb_ref, xs_ref, gc_ref, *, bg, cap):
    @pl.when(pl.program_id(1) == 0)
    def _():
        wgb_ref[...] = wg_ref[...].astype(BF16)
        wub_ref[...] = wu_ref[...].astype(BF16)
        wdb_ref[...] = wd_ref[...].astype(BF16)

    j = lax.broadcasted_iota(I32, (cap, 1), 0)
    for s in range(bg):
        onehot = slot_ref[s] == j
        xs_ref[s * cap:(s + 1) * cap, :] = _dot(jnp.where(onehot, 1.0, 0.0).astype(BF16), h_ref[s]).astype(BF16)
        gc_ref[s * cap:(s + 1) * cap, :] = jnp.sum(jnp.where(onehot, aff_ref[s], 0.0), axis=-1, keepdims=True)
    xs = xs_ref[...]
    zg = _dot(xs, wgb_ref[...])
    hid = (zg * _sigmoid(zg) * _dot(xs, wub_ref[...])).astype(BF16)
    ys_ref[...] = (_dot(hid, wdb_ref[...]) * gc_ref[...]).astype(BF16)


def _moe_ffn(slot, aff, h2, w_gate, w_up, w_down, layer, b, t, cap, bg):
    f = w_gate.shape[-1]
    sa_spec = pl.BlockSpec((bg, None, 1, t), lambda e, i: (i, e, 0, 0))
    w_spec = lambda d0, d1: pl.BlockSpec((None, None, d0, d1), lambda e, i: (layer, e, 0, 0))
    return pl.pallas_call(
        functools.partial(_ffn_kernel, bg=bg, cap=cap),
        grid=(N_EXPERTS, b // bg),
        in_specs=[sa_spec, sa_spec, pl.BlockSpec((bg, t, D), lambda e, i: (i, 0, 0)),
                  w_spec(D, f), w_spec(D, f), w_spec(f, D)],
        out_specs=pl.BlockSpec((None, bg * cap, D), lambda e, i: (e, i, 0)),
        out_shape=jax.ShapeDtypeStruct((N_EXPERTS, b * cap, D), BF16),
        scratch_shapes=[pltpu.VMEM((D, f), BF16), pltpu.VMEM((D, f), BF16), pltpu.VMEM((f, D), BF16),
                        pltpu.VMEM((bg * cap, D), BF16), pltpu.VMEM((bg * cap, 1), F32)],
        compiler_params=_cparams(2, 56),
        name="moe_ffn",
    )(slot, aff, h2, w_gate, w_up, w_down)


def _combine_kernel(slot_ref, ys_ref, x_ref, mod_ref, o_ref, *, cap):
    n = N_EXPERTS * cap
    e_of = lax.broadcasted_iota(I32, (N_EXPERTS, n), 1) >> (cap.bit_length() - 1)
    rep = jnp.where(e_of == lax.broadcasted_iota(I32, (N_EXPERTS, n), 0), 1.0, 0.0).astype(BF16)
    slot_rep = _dot(slot_ref[...].astype(F32).astype(BF16), rep)
    jn = (lax.broadcasted_iota(I32, (1, n), 1) & (cap - 1)).astype(F32)
    onehot = jnp.where(slot_rep == jn, 1.0, 0.0).astype(BF16)
    y = _dot(onehot, ys_ref[...].reshape(n, D))
    o_ref[...] = x_ref[...] + mod_ref[:, 5 * D:6 * D] * y


def _moe_combine(slot_t, ys, x, mod3, mod_base, per_batch_mod, b, t, cap, tt=256):
    nt = t // tt
    return pl.pallas_call(
        functools.partial(_combine_kernel, cap=cap),
        grid=(b, nt),
        in_specs=[pl.BlockSpec((None, tt, N_EXPERTS), lambda bi, ti: (bi, ti, 0)),
                  pl.BlockSpec((N_EXPERTS, None, cap, D), lambda bi, ti: (0, bi, 0, 0)),
                  pl.BlockSpec((tt, D), lambda bi, ti: (bi * nt + ti, 0)),
                  pl.BlockSpec((None, 1, 6 * D), lambda bi, ti: (mod_base + bi * per_batch_mod, 0, 0))],
        out_specs=pl.BlockSpec((tt, D), lambda bi, ti: (bi * nt + ti, 0)),
        out_shape=jax.ShapeDtypeStruct((b * t, D), F32),
        compiler_params=_cparams(2, 48),
        name="moe_combine",
    )(slot_t, ys, x, mod3)


def _ec_moe(x, h2, mod3, mod_base, per_batch_mod, moe_w, layer, b, t, bs, bg):
    w_router, w_gate, w_up, w_down = moe_w
    cap = EC_FACTOR * t // N_EXPERTS
    h3 = h2.reshape(b, t, D)
    slot, aff = _moe_select(h3, w_router[layer], b, t, cap, bs)
    ys = _moe_ffn(slot.reshape(b, N_EXPERTS, 1, t), aff.reshape(b, N_EXPERTS, 1, t), h3,
                  w_gate, w_up, w_down, layer, b, t, cap, bg)
    return _moe_combine(jnp.swapaxes(slot, 1, 2), ys.reshape(N_EXPERTS, b, cap, D), x,
                        mod3, mod_base, per_batch_mod, b, t, cap)


def kernel(x_prompt, x_sample, cache_a_k, cache_a_v, cache_b_k, cache_b_v, state_c_h, c, c_ctx, ada_w, ada_b, norm_mix_g, norm_ffn_g, a_w_in, a_q_norm_g, a_k_norm_g, a_lam_q, a_lam_k, a_subln_g, a_w_out, b_w_in, b_q_norm_g, b_k_norm_g, b_sink, b_w_out, c_w_in, c_conv_w, c_conv_b, c_w_rg, c_b_rg, c_w_ig, c_b_ig, c_lam, c_w_out, moe_w_router, moe_w_gate, moe_w_up, moe_w_down):
    bp, tp, _ = x_prompt.shape
    bs_, ts, _ = x_sample.shape
    past = cache_a_k.shape[2]
    cvec = jnp.concatenate([c, c_ctx[None, :], jnp.zeros((16 - bs_ - 1, D), F32)], axis=0)
    mods = _modulation(cvec, ada_w, ada_b)
    groups = {"p": (bp, tp, bs_, 0), "s": (bs_, ts, 0, 1)}
    xs = {"p": x_prompt.reshape(bp * tp, D), "s": x_sample.reshape(bs_ * ts, D)}
    ck_a = cache_a_k.reshape(bs_, -1, past, H_A * 2 * HD_A)
    cv_a = cache_a_v.reshape(bs_, -1, past, H_A * 2 * HD_A)
    ck_b = cache_b_k.reshape(bs_, -1, past, KV_B * HD_B)
    cv_b = cache_b_v.reshape(bs_, -1, past, KV_B * HD_B)
    moe_w = (moe_w_router, moe_w_gate, moe_w_up, moe_w_down)
    new_a_k, new_a_v, new_b_k, new_b_v, new_c_h = [], [], [], [], []
    ia = ib = ic = 0
    for l in range(DEPTH):
        mod3 = mods[l].reshape(16, 1, 6 * D)
        kind = l % 3
        for key in ("p", "s"):
            b, t, mbase, per_b = groups[key]
            rows_per_mod = t if per_b else b * t
            x = xs[key]
            sample = key == "s"
            if kind == 0:
                lam_init = 0.8 - 0.6 * math.exp(-0.3 * l)
                qkv = _norm_mod_matmul(x, mod3, mbase, rows_per_mod, norm_mix_g[l], a_w_in[ia])
                outs = _qk_prep(qkv, a_q_norm_g[ia], a_k_norm_g[ia], D, D, D, t, rope=sample, k_f32=not sample)
                q, k, v = outs[:3]
                if not sample:
                    new_a_k.append(outs[3].reshape(b, t, H_A, 2 * HD_A))
                    new_a_v.append(qkv[:, 2 * D:].reshape(b, t, H_A, 2 * HD_A))
                o = _diff_attention(q, k, v, (ck_a, cv_a, ia) if sample else None, a_lam_q[ia], a_lam_k[ia],
                                    a_subln_g[ia], lam_init, b, t, heads=1 if sample else H_A, tq=256)
                x, h2 = _proj_residual(o, x, mod3, mbase, rows_per_mod, norm_ffn_g[l], a_w_out[ia])
            elif kind == 1:
                nq, nk = H_B * HD_B, KV_B * HD_B
                qkv = _norm_mod_matmul(x, mod3, mbase, rows_per_mod, norm_mix_g[l], b_w_in[ib])
                outs = _qk_prep(qkv, b_q_norm_g[ib], b_k_norm_g[ib], nq, nk, nk, t, rope=sample, k_f32=not sample)
                q, k, v = outs[:3]
                if not sample:
                    new_b_k.append(outs[3].reshape(b, t, KV_B, HD_B))
                    new_b_v.append(qkv[:, nq + nk:].reshape(b, t, KV_B, HD_B))
                o = _gqa_attention(q, k, v, (ck_b, cv_b, ib) if sample else None, b_sink[ib], b, t,
                                   windowed=sample, tq=256)
                x, h2 = _proj_residual(o, x, mod3, mbase, rows_per_mod, norm_ffn_g[l], b_w_out[ib])
            else:
                gx = _norm_mod_matmul(x, mod3, mbase, rows_per_mod, norm_mix_g[l], c_w_in[ic])
                af, uf, ab, ub = _lru_gates(gx, c_conv_w[ic], c_conv_b[ic], c_w_rg[ic], c_b_rg[ic],
                                            c_w_ig[ic], c_b_ig[ic], c_lam[ic], b, t)
                if sample:
                    h0f, h0b = state_c_h[:, ic, 0], state_c_h[:, ic, 1]
                else:
                    h0f = h0b = jnp.zeros((b, D_RNN), F32)
                tm = lambda z: z.reshape(t * b, D_RNN)
                hf = _lru_scan(tm(af), tm(uf), h0f, b, t, reverse=False)
                hb = _lru_scan(tm(ab), tm(ub), h0b, b, t, reverse=True)
                if not sample:
                    new_c_h.append(jnp.stack([hf[(t - 1) * b:], hb[:b]], axis=1))
                x, h2 = _lru_proj_residual(gx, hf.reshape(t, b * D_RNN), hb.reshape(t, b * D_RNN), x, mod3,
                                           mbase, per_b, norm_ffn_g[l], c_w_out[ic], b, t)
            xs[key] = _ec_moe(x, h2, mod3, mbase, per_b, moe_w, l, b, t,
                              bs=2 if sample else 8, bg=1 if sample else 8)
        ia, ib, ic = ia + (kind == 0), ib + (kind == 1), ic + (kind == 2)
    return (xs["p"].reshape(bp, tp, D), xs["s"].reshape(bs_, ts, D),
            jnp.stack(new_a_k, axis=1), jnp.stack(new_a_v, axis=1),
            jnp.stack(new_b_k, axis=1), jnp.stack(new_b_v, axis=1), jnp.stack(new_c_h, axis=1))
```

```python
import functools
import math

import jax
import jax.numpy as jnp
from jax import lax
from jax.experimental import pallas as pl
from jax.experimental.pallas import tpu as pltpu

F32, BF16, I32 = jnp.float32, jnp.bfloat16, jnp.int32

D = 1024
DEPTH = 4
GRID_W = 64
H_A, HD_A = 8, 64
H_B, KV_B, G_B, HD_B = 16, 4, 4, 64
WINDOW = 128
D_RNN = 1024
N_BLK_C, BW_C = 16, 64
LRU_C = 8.0
N_EXPERTS = 16
EC_FACTOR = 2
ROPE_THETA = 10000.0
EPS = 1e-6
LANES = 128
MXU_DIM = 256
MIB = 1024 * 1024


def _cparams(n_axes, vmem_mib=48):
    return pltpu.CompilerParams(dimension_semantics=("arbitrary",) * n_axes,
                                vmem_limit_bytes=vmem_mib * MIB)


def _sigmoid(x):
    return 1.0 / (1.0 + jnp.exp(-x))


def _rms(x):
    return x * lax.rsqrt(jnp.mean(x * x, axis=-1, keepdims=True) + EPS)


def _dot(a, b):
    return jnp.dot(a, b, preferred_element_type=F32)


def _dot_nt(a, b):
    return lax.dot_general(a, b, (((1,), (1,)), ((), ())), preferred_element_type=F32)


def _mod_kernel(c_ref, w_ref, b_ref, o_ref):
    c = c_ref[...]
    s = (c * _sigmoid(c)).astype(BF16)
    o_ref[...] = _dot(s, w_ref[...].astype(BF16)) + b_ref[...]


def _modulation(cvec, ada_w, ada_b):
    nt = 1536
    return pl.pallas_call(
        _mod_kernel,
        grid=(DEPTH, 6 * D // nt),
        in_specs=[pl.BlockSpec((16, D), lambda l, j: (0, 0)),
                  pl.BlockSpec((None, D, nt), lambda l, j: (l, 0, j)),
                  pl.BlockSpec((None, 1, nt), lambda l, j: (l, 0, j))],
        out_specs=pl.BlockSpec((None, 16, nt), lambda l, j: (l, 0, j)),
        out_shape=jax.ShapeDtypeStruct((DEPTH, 16, 6 * D), F32),
        compiler_params=_cparams(2, 32),
        name="adaln_mod",
    )(cvec, ada_w, ada_b.reshape(DEPTH, 1, 6 * D))


def _nmm_kernel(x_ref, mod_ref, g_ref, w_ref, o_ref, wbf_ref):
    @pl.when(pl.program_id(0) == 0)
    def _():
        wbf_ref[...] = w_ref[...].astype(BF16)

    m = mod_ref[...]
    h = (_rms(x_ref[...]) * g_ref[...] * (1.0 + m[:, D:2 * D]) + m[:, 0:D]).astype(BF16)
    o_ref[...] = _dot(h, wbf_ref[...])


def _norm_mod_matmul(x, mod3, mod_base, rows_per_mod, g, w, tm=256):
    r, n = x.shape[0], w.shape[1]
    return pl.pallas_call(
        _nmm_kernel,
        grid=(r // tm,),
        in_specs=[pl.BlockSpec((tm, D), lambda i: (i, 0)),
                  pl.BlockSpec((None, 1, 6 * D), lambda i: (mod_base + (i * tm) // rows_per_mod, 0, 0)),
                  pl.BlockSpec((1, D), lambda i: (0, 0)),
                  pl.BlockSpec((D, n), lambda i: (0, 0), pipeline_mode=pl.Buffered(1))],
        out_specs=pl.BlockSpec((tm, n), lambda i: (i, 0)),
        out_shape=jax.ShapeDtypeStruct((r, n), F32),
        scratch_shapes=[pltpu.VMEM((D, n), BF16)],
        compiler_params=_cparams(1, 48),
        name="norm_mod_matmul",
    )(x, mod3, g.reshape(1, D), w)


def _proj_tail(y, x_ref, mod_ref, g2_ref, xo_ref, h2_ref):
    m = mod_ref[...]
    xn = x_ref[...] + m[:, 2 * D:3 * D] * y
    xo_ref[...] = xn
    h2_ref[...] = (_rms(xn) * g2_ref[...] * (1.0 + m[:, 4 * D:5 * D]) + m[:, 3 * D:4 * D]).astype(BF16)


def _proj_kernel(o_ref, x_ref, mod_ref, g2_ref, w_ref, xo_ref, h2_ref, wbf_ref):
    @pl.when(pl.program_id(0) == 0)
    def _():
        wbf_ref[...] = w_ref[...].astype(BF16)

    _proj_tail(_dot(o_ref[...], wbf_ref[...]), x_ref, mod_ref, g2_ref, xo_ref, h2_ref)


def _proj_residual(o, x, mod3, mod_base, rows_per_mod, g2, w, tm=256):
    r = x.shape[0]
    return pl.pallas_call(
        _proj_kernel,
        grid=(r // tm,),
        in_specs=[pl.BlockSpec((tm, D), lambda i: (i, 0)),
                  pl.BlockSpec((tm, D), lambda i: (i, 0)),
                  pl.BlockSpec((None, 1, 6 * D), lambda i: (mod_base + (i * tm) // rows_per_mod, 0, 0)),
                  pl.BlockSpec((1, D), lambda i: (0, 0)),
                  pl.BlockSpec((D, D), lambda i: (0, 0))],
        out_specs=[pl.BlockSpec((tm, D), lambda i: (i, 0)), pl.BlockSpec((tm, D), lambda i: (i, 0))],
        out_shape=[jax.ShapeDtypeStruct((r, D), F32), jax.ShapeDtypeStruct((r, D), BF16)],
        scratch_shapes=[pltpu.VMEM((D, D), BF16)],
        compiler_params=_cparams(1, 40),
        name="proj_residual",
    )(o, x, mod3, g2.reshape(1, D), w)


def _group_mean_sq(x):
    ss = x * x
    hi = ss.astype(BF16)
    lo = (ss - hi.astype(F32)).astype(BF16)
    r = lax.broadcasted_iota(I32, (LANES, LANES), 0) >> 6
    c = lax.broadcasted_iota(I32, (LANES, LANES), 1) >> 6
    ones_bd = jnp.where(r == c, 1.0, 0.0).astype(BF16)
    return (_dot(hi, ones_bd) + _dot(lo, ones_bd)) * (1.0 / HD_A)


def _rope128(y, cos, sa, sb):
    return y * cos + pltpu.roll(y, LANES - 16, 1) * sa + pltpu.roll(y, 16, 1) * sb


def _prep_kernel(*refs, nq, nk, rope, k_f32):
    it = iter(refs)
    qkv_ref, qg_ref, kg_ref = next(it), next(it), next(it)
    if rope:
        cos_ref, sa_ref, sb_ref = next(it), next(it), next(it)
    q_ref, k_ref, v_ref = next(it), next(it), next(it)
    kf_ref = next(it) if k_f32 else None
    if rope:
        cos, sa, sb = cos_ref[...], sa_ref[...], sb_ref[...]
    for j in range((nq + nk) // LANES):
        x = qkv_ref[:, j * LANES:(j + 1) * LANES]
        is_q = j < nq // LANES
        y = x * lax.rsqrt(_group_mean_sq(x) + EPS) * (qg_ref[...] if is_q else kg_ref[...])
        if rope:
            y = _rope128(y, cos, sa, sb)
        if is_q:
            q_ref[:, j * LANES:(j + 1) * LANES] = (y * (HD_A ** -0.5)).astype(BF16)
        else:
            jj = j - nq // LANES
            k_ref[:, jj * LANES:(jj + 1) * LANES] = y.astype(BF16)
            if k_f32:
                kf_ref[:, jj * LANES:(jj + 1) * LANES] = y
    v_ref[...] = qkv_ref[:, nq + nk:].astype(BF16)


def _rope_tables(t):
    n_freq = HD_A // 4
    inv_freq = ROPE_THETA ** (-jnp.arange(n_freq, dtype=F32) / n_freq)
    pos_row = jnp.repeat(jnp.arange(t // GRID_W), GRID_W).astype(F32)
    pos_col = jnp.tile(jnp.arange(GRID_W), t // GRID_W).astype(F32)
    ang_r = pos_row[:, None] * inv_freq[None, :]
    ang_c = pos_col[:, None] * inv_freq[None, :]
    z = jnp.zeros_like(ang_r)
    cos64 = jnp.concatenate([jnp.cos(ang_r), jnp.cos(ang_r), jnp.cos(ang_c), jnp.cos(ang_c)], axis=-1)
    sa64 = jnp.concatenate([-jnp.sin(ang_r), z, -jnp.sin(ang_c), z], axis=-1)
    sb64 = jnp.concatenate([z, jnp.sin(ang_r), z, jnp.sin(ang_c)], axis=-1)
    return tuple(jnp.tile(a, (1, 2)) for a in (cos64, sa64, sb64))


def _qk_prep(qkv, qg, kg, nq, nk, nv, t, rope, k_f32, tm=256):
    r = qkv.shape[0]
    tile2 = lambda g: jnp.tile(g.reshape(1, HD_A), (1, 2))
    args = [qkv, tile2(qg), tile2(kg)]
    in_specs = [pl.BlockSpec((tm, nq + nk + nv), lambda i: (i, 0)),
                pl.BlockSpec((1, LANES), lambda i: (0, 0)),
                pl.BlockSpec((1, LANES), lambda i: (0, 0))]
    if rope:
        args += list(_rope_tables(t))
        in_specs += [pl.BlockSpec((tm, LANES), lambda i: (i % (t // tm), 0))] * 3
    out_shape = [jax.ShapeDtypeStruct((r, nq), BF16), jax.ShapeDtypeStruct((r, nk), BF16),
                 jax.ShapeDtypeStruct((r, nv), BF16)]
    out_specs = [pl.BlockSpec((tm, nq), lambda i: (i, 0)), pl.BlockSpec((tm, nk), lambda i: (i, 0)),
                 pl.BlockSpec((tm, nv), lambda i: (i, 0))]
    if k_f32:
        out_shape.append(jax.ShapeDtypeStruct((r, nk), F32))
        out_specs.append(pl.BlockSpec((tm, nk), lambda i: (i, 0)))
    return pl.pallas_call(
        functools.partial(_prep_kernel, nq=nq, nk=nk, rope=rope, k_f32=k_f32),
        grid=(r // tm,),
        in_specs=in_specs, out_specs=out_specs, out_shape=out_shape,
        compiler_params=_cparams(1, 32),
        name="qk_prep",
    )(*args)


def _diff_attn_kernel(*refs, heads, ctx, lam_init, nsub):
    it = iter(refs)
    q_ref, k_ref, v_ref = next(it), next(it), next(it)
    if ctx:
        ck_ref, cv_ref = next(it), next(it)
    lq_ref, lk_ref, sg_ref, o_ref = next(it), next(it), next(it), next(it)
    e = jnp.exp(jnp.sum(lq_ref[...] * lk_ref[...], axis=-1, keepdims=True))
    lam = e[0:1, :] - e[1:2, :] + lam_init
    lane = lax.broadcasted_iota(I32, (1, LANES), 1)
    tqs = q_ref.shape[0] // nsub
    units = []
    for h in range(heads):
        sl = slice(h * LANES, (h + 1) * LANES)
        k = k_ref[:, sl]
        ck = ck_ref[:, sl].astype(BF16) if ctx else None
        for j in range(nsub):
            rows = slice(j * tqs, (j + 1) * tqs)
            q = q_ref[rows, sl]
            zero = jnp.zeros_like(q)
            scores = []
            for c in range(2):
                qc = jnp.where((lane < HD_A) if c == 0 else (lane >= HD_A), q, zero)
                scores.append((_dot_nt(qc, k), _dot_nt(qc, ck) if ctx else None))
            units.append((sl, rows, scores))
    for sl, rows, scores in units:
        w_lat, w_ctx = None, None
        for c, (s, sc) in enumerate(scores):
            m = jnp.max(s, axis=-1, keepdims=True)
            if ctx:
                m = jnp.maximum(m, jnp.max(sc, axis=-1, keepdims=True))
            p = jnp.exp(s - m)
            l = jnp.sum(p, axis=-1, keepdims=True)
            if ctx:
                pc = jnp.exp(sc - m)
                l = l + jnp.sum(pc, axis=-1, keepdims=True)
            coef = (1.0 / l) if c == 0 else (-lam / l)
            w_lat = p * coef if c == 0 else w_lat + p * coef
            if ctx:
                w_ctx = pc * coef if c == 0 else w_ctx + pc * coef
        o = _dot(w_lat.astype(BF16), v_ref[:, sl])
        if ctx:
            o = o + _dot(w_ctx.astype(BF16), cv_ref[:, sl].astype(BF16))
        o_ref[rows, sl] = (_rms(o) * sg_ref[...] * (1.0 - lam_init)).astype(BF16)


def _diff_attention(q, k, v, cache, lam_q, lam_k, subln_g, lam_init, b, t, heads, tq, nsub):
    nq = t // tq
    hb = H_A // heads
    args = [q, k, v]
    in_specs = [pl.BlockSpec((tq, heads * LANES), lambda bi, hi, qi: (bi * nq + qi, hi)),
                pl.BlockSpec((t, heads * LANES), lambda bi, hi, qi: (bi, hi)),
                pl.BlockSpec((t, heads * LANES), lambda bi, hi, qi: (bi, hi))]
    if cache is not None:
        ck, cv, la = cache
        p = ck.shape[2]
        args += [ck, cv]
        in_specs += [pl.BlockSpec((None, None, p, heads * LANES), lambda bi, hi, qi: (bi, la, 0, hi))] * 2
    args += [lam_q, lam_k, subln_g.reshape(1, LANES)]
    in_specs += [pl.BlockSpec((2, HD_A), lambda bi, hi, qi: (0, 0)),
                 pl.BlockSpec((2, HD_A), lambda bi, hi, qi: (0, 0)),
                 pl.BlockSpec((1, LANES), lambda bi, hi, qi: (0, 0))]
    return pl.pallas_call(
        functools.partial(_diff_attn_kernel, heads=heads, ctx=cache is not None, lam_init=lam_init, nsub=nsub),
        grid=(b, hb, nq),
        in_specs=in_specs,
        out_specs=pl.BlockSpec((tq, heads * LANES), lambda bi, hi, qi: (bi * nq + qi, hi)),
        out_shape=jax.ShapeDtypeStruct((b * t, H_A * LANES), BF16),
        compiler_params=_cparams(3, 48),
        name="diff_attention",
    )(*args)


def _both_halves(x, half):
    lane_half = lax.broadcasted_iota(I32, (1, LANES), 1) >> 6
    xm = jnp.where(lane_half == half, x, 0.0)
    return xm + pltpu.roll(xm, HD_B, 1)


def _gqa_kernel(*refs, kvs, windowed, ctx, t, tq):
    it = iter(refs)
    q_ref, k_ref, v_ref = next(it), next(it), next(it)
    if ctx:
        ck_ref, cv_ref = next(it), next(it)
    sink_ref, o_ref = next(it), next(it)
    lane_half = lax.broadcasted_iota(I32, (1, LANES), 1) >> 6
    if windowed:
        kd_ref, vd_ref, ckd_ref, cvd_ref = next(it), next(it), next(it), next(it)
        half = pl.program_id(1) % 2

        @pl.when(pl.program_id(2) == 0)
        def _():
            kd_ref[...] = _both_halves(k_ref[...].astype(F32), half).astype(BF16)
            vd_ref[...] = _both_halves(v_ref[...].astype(F32), half).astype(BF16)
            ckd_ref[...] = _both_halves(ck_ref[...], half).astype(BF16)
            cvd_ref[...] = _both_halves(cv_ref[...], half).astype(BF16)

        span = tq + 2 * WINDOW
        q0 = pl.program_id(2) * tq
        start = pl.multiple_of(jnp.clip(q0 - WINDOW, 0, t - span), LANES)
        qpos = q0 + lax.broadcasted_iota(I32, (tq, span), 0)
        kpos = start + lax.broadcasted_iota(I32, (tq, span), 1)
        valid = jnp.abs(qpos - kpos) <= WINDOW
    units = []
    for kv in range(kvs):
        if windowed:
            kd, vd = kd_ref[pl.ds(start, span), :], vd_ref[pl.ds(start, span), :]
            ckd, cvd = ckd_ref[...], cvd_ref[...]
        else:
            kcol = slice((kv // 2) * LANES, (kv // 2 + 1) * LANES)
            kd = _both_halves(k_ref[:, kcol].astype(F32), kv % 2).astype(BF16)
            vd = _both_halves(v_ref[:, kcol].astype(F32), kv % 2).astype(BF16)
            ckd = cvd = None
        for g in range(G_B):
            qcol = slice((kv * G_B + g) // 2 * LANES, ((kv * G_B + g) // 2 + 1) * LANES)
            q = q_ref[:, qcol]
            qm = jnp.where(lane_half == g % 2, q, jnp.zeros_like(q))
            sink = sink_ref[kv:kv + 1, g:g + 1] if kvs > 1 else sink_ref[:, g:g + 1]
            units.append((g, qcol, sink, _dot_nt(qm, kd), _dot_nt(qm, ckd) if ctx else None, vd, cvd))
    o_pair = None
    for g, qcol, sink, s, sc, vd, cvd in units:
        if windowed:
            s = jnp.where(valid, s, -jnp.inf)
        m = jnp.maximum(jnp.max(s, axis=-1, keepdims=True), sink)
        if ctx:
            m = jnp.maximum(m, jnp.max(sc, axis=-1, keepdims=True))
        p = jnp.exp(s - m)
        l = jnp.sum(p, axis=-1, keepdims=True) + jnp.exp(sink - m)
        if ctx:
            pc = jnp.exp(sc - m)
            l = l + jnp.sum(pc, axis=-1, keepdims=True)
        r = 1.0 / l
        o = _dot((p * r).astype(BF16), vd)
        if ctx:
            o = o + _dot((pc * r).astype(BF16), cvd)
        o = jnp.where(lane_half == g % 2, o, 0.0)
        if g % 2 == 0:
            o_pair = o
        else:
            o_ref[:, qcol] = (o_pair + o).astype(BF16)


def _gqa_attention(q, k, v, cache, sink, b, t, windowed, tq):
    if windowed:
        kvs, nq = 1, t // tq
        grid = (b, KV_B, nq)
        qw = G_B * HD_B
        q_spec = pl.BlockSpec((tq, qw), lambda bi, kv, qi: (bi * nq + qi, kv))
        kv_spec = pl.BlockSpec((t, LANES), lambda bi, kv, qi: (bi, kv // 2))
        sink_arr = sink.reshape(KV_B, 1, G_B)
        sink_spec = pl.BlockSpec((None, 1, G_B), lambda bi, kv, qi: (kv, 0, 0))
        o_spec = pl.BlockSpec((tq, qw), lambda bi, kv, qi: (bi * nq + qi, kv))
    else:
        kvs, tq = KV_B, t
        grid = (b, 1, 1)
        q_spec = pl.BlockSpec((t, H_B * HD_B), lambda bi, kv, qi: (bi, 0))
        kv_spec = pl.BlockSpec((t, KV_B * HD_B), lambda bi, kv, qi: (bi, 0))
        sink_arr = sink.reshape(KV_B, G_B)
        sink_spec = pl.BlockSpec((KV_B, G_B), lambda bi, kv, qi: (0, 0))
        o_spec = pl.BlockSpec((t, H_B * HD_B), lambda bi, kv, qi: (bi, 0))
    args, in_specs = [q, k, v], [q_spec, kv_spec, kv_spec]
    if cache is not None:
        ck, cv, lb = cache
        p = ck.shape[2]
        args += [ck, cv]
        in_specs += [pl.BlockSpec((None, None, p, LANES), lambda bi, kv, qi: (bi, lb, 0, kv // 2))] * 2
    args.append(sink_arr)
    in_specs.append(sink_spec)
    scratch = []
    if windowed:
        scratch = [pltpu.VMEM((t, LANES), BF16)] * 2 + [pltpu.VMEM((p, LANES), BF16)] * 2
    return pl.pallas_call(
        functools.partial(_gqa_kernel, kvs=kvs, windowed=windowed, ctx=cache is not None, t=t, tq=tq),
        grid=grid, in_specs=in_specs, out_specs=o_spec,
        out_shape=jax.ShapeDtypeStruct((b * t, H_B * HD_B), BF16),
        scratch_shapes=scratch,
        compiler_params=_cparams(3, 48),
        name="gqa_attention",
    )(*args)


def _lru_gates_kernel(xp_ref, xc_ref, xn_ref, cw_ref, cb_ref, wg_ref, bg_ref, lam_ref,
                      af_ref, uf_ref, ab_ref, ub_ref, *, tt):
    ti, nt = pl.program_id(1), pl.num_programs(1)
    nl = -lam_ref[...]
    sp = jnp.maximum(nl, 0.0) + jnp.log1p(jnp.exp(-jnp.abs(nl)))
    cur = xc_ref[...]
    prev = xp_ref[...] * (ti > 0).astype(F32)
    nxt = xn_ref[...] * (ti < nt - 1).astype(F32)
    row = lax.broadcasted_iota(I32, (tt, 1), 0)
    xm1 = jnp.where(row == 0, prev[tt - 1:tt, :], pltpu.roll(cur, 1, 0))
    xm2 = jnp.where(row == 0, prev[tt - 2:tt - 1, :],
                    jnp.where(row == 1, prev[tt - 1:tt, :], pltpu.roll(cur, 2, 0)))
    xp1 = jnp.where(row == tt - 1, nxt[0:1, :], pltpu.roll(cur, tt - 1, 0))
    cw = cw_ref[...]
    xc = xm2 * cw[0:1, :] + xm1 * cw[1:2, :] + cur * cw[2:3, :] + xp1 * cw[3:4, :] + cb_ref[...]
    xcb = xc.astype(BF16)
    outs = ((af_ref, uf_ref), (ab_ref, ub_ref))
    for j in range(D_RNN // MXU_DIM):
        cs = slice(j * MXU_DIM, (j + 1) * MXU_DIM)
        z = _dot(xcb[:, cs], wg_ref[j].astype(BF16))
        xj = xc[:, cs]
        for d in range(2):
            r = _sigmoid(z[:, (2 * d) * MXU_DIM:(2 * d + 1) * MXU_DIM] + bg_ref[2 * d:2 * d + 1, cs])
            i = _sigmoid(z[:, (2 * d + 1) * MXU_DIM:(2 * d + 2) * MXU_DIM] + bg_ref[2 * d + 1:2 * d + 2, cs])
            log_a = -LRU_C * r * sp[d:d + 1, cs]
            a = jnp.exp(log_a)
            outs[d][0][:, cs] = a
            outs[d][1][:, cs] = jnp.sqrt(1.0 - a * a) * i * xj


def _lru_gates(gx, conv_w, conv_b, w_rg, b_rg, w_ig, b_ig, lam, b, t, tt=256):
    nt = t // tt
    per_tile = MXU_DIM // BW_C
    eye = jnp.eye(per_tile, dtype=F32)

    def bd(w):
        w4 = w.reshape(N_BLK_C // per_tile, per_tile, BW_C, BW_C)
        return (w4[:, :, :, None, :] * eye[None, :, None, :, None]).reshape(-1, MXU_DIM, MXU_DIM)

    wg = jnp.concatenate([bd(w_rg[0]), bd(w_ig[0]), bd(w_rg[1]), bd(w_ig[1])], axis=-1)
    bg = jnp.stack([b_rg[0], b_ig[0], b_rg[1], b_ig[1]])
    x_spec = lambda f: pl.BlockSpec((tt, D_RNN), f)
    full = lambda shape: pl.BlockSpec(shape, lambda bi, ti: (0,) * len(shape))
    o_spec = pl.BlockSpec((tt, D_RNN), lambda bi, ti: (ti, bi))
    o_shape = jax.ShapeDtypeStruct((t, b * D_RNN), F32)
    return pl.pallas_call(
        functools.partial(_lru_gates_kernel, tt=tt),
        grid=(b, nt),
        in_specs=[x_spec(lambda bi, ti: (bi * nt + jnp.maximum(ti - 1, 0), 1)),
                  x_spec(lambda bi, ti: (bi * nt + ti, 1)),
                  x_spec(lambda bi, ti: (bi * nt + jnp.minimum(ti + 1, nt - 1), 1)),
                  full((4, D_RNN)), full((1, D_RNN)), full((4, MXU_DIM, 4 * MXU_DIM)),
                  full((4, D_RNN)), full((2, D_RNN))],
        out_specs=[o_spec] * 4, out_shape=[o_shape] * 4,
        compiler_params=_cparams(2, 48),
        name="lru_gates",
    )(gx, gx, gx, conv_w, conv_b.reshape(1, D_RNN), wg, bg, lam)


def _scan_kernel(a_ref, u_ref, h0_ref, hs_ref, h_ref, *, b, tc, reverse):
    @pl.when(pl.program_id(0) == 0)
    def _():
        h_ref[...] = h0_ref[...]

    def step(i, h):
        tl = (tc - 1 - i) if reverse else i
        rows = pl.ds(pl.multiple_of(tl * b, 8), b)
        h = a_ref[rows, :] * h + u_ref[rows, :]
        hs_ref[rows, :] = h
        return h

    h_ref[...] = lax.fori_loop(0, tc, step, h_ref[...], unroll=8)


def _lru_scan(a, u, h0, b, t, reverse, rows=512):
    tc = rows // b
    nc = t // tc
    blk = pl.BlockSpec((tc * b, D_RNN), (lambda c: (nc - 1 - c, 0)) if reverse else (lambda c: (c, 0)))
    return pl.pallas_call(
        functools.partial(_scan_kernel, b=b, tc=tc, reverse=reverse),
        grid=(nc,),
        in_specs=[blk, blk, pl.BlockSpec((b, D_RNN), lambda c: (0, 0))],
        out_specs=blk,
        out_shape=jax.ShapeDtypeStruct((t * b, D_RNN), F32),
        scratch_shapes=[pltpu.VMEM((b, D_RNN), F32)],
        compiler_params=_cparams(1, 48),
        name="lru_scan",
    )(a, u, h0)


def _lru_proj_kernel(gate_ref, hf_ref, hb_ref, x_ref, mod_ref, g2_ref, w_ref, xo_ref, h2_ref, wbf_ref):
    @pl.when((pl.program_id(0) == 0) & (pl.program_id(1) == 0))
    def _():
        wbf_ref[...] = w_ref[...].astype(BF16)

    g = gate_ref[...]
    gelu = 0.5 * g * (1.0 + jnp.tanh(math.sqrt(2.0 / math.pi) * (g + 0.044715 * (g * g * g))))
    y = (gelu * (hf_ref[...] + hb_ref[...])).astype(BF16)
    _proj_tail(_dot(y, wbf_ref[...]), x_ref, mod_ref, g2_ref, xo_ref, h2_ref)


def _lru_proj_residual(gx, hf, hb, x, mod3, mod_base, per_batch_mod, g2, w, b, t, tt=256):
    nt = t // tt
    row = lambda bi, ti: (bi * nt + ti, 0)
    return pl.pallas_call(
        _lru_proj_kernel,
        grid=(b, nt),
        in_specs=[pl.BlockSpec((tt, D_RNN), row),
                  pl.BlockSpec((tt, D_RNN), lambda bi, ti: (ti, bi)),
                  pl.BlockSpec((tt, D_RNN), lambda bi, ti: (ti, bi)),
                  pl.BlockSpec((tt, D), row),
                  pl.BlockSpec((None, 1, 6 * D), lambda bi, ti: (mod_base + bi * per_batch_mod, 0, 0)),
                  pl.BlockSpec((1, D), lambda bi, ti: (0, 0)),
                  pl.BlockSpec((D_RNN, D), lambda bi, ti: (0, 0))],
        out_specs=[pl.BlockSpec((tt, D), row), pl.BlockSpec((tt, D), row)],
        out_shape=[jax.ShapeDtypeStruct((b * t, D), F32), jax.ShapeDtypeStruct((b * t, D), BF16)],
        scratch_shapes=[pltpu.VMEM((D_RNN, D), BF16)],
        compiler_params=_cparams(2, 40),
        name="lru_proj_residual",
    )(gx, hf, hb, x, mod3, g2.reshape(1, D), w)


def _lru_conv_gates(xp_ref, xc_ref, xn_ref, cw_ref, cb_ref, wg_ref, bg_ref, lam_ref, a_ref, u_ref, ti, nt, tt):
    nl = -lam_ref[...]
    sp = jnp.maximum(nl, 0.0) + jnp.log1p(jnp.exp(-jnp.abs(nl)))
    cur = xc_ref[...]
    prev = xp_ref[...] * (ti > 0).astype(F32)
    nxt = xn_ref[...] * (ti < nt - 1).astype(F32)
    row = lax.broadcasted_iota(I32, (tt, 1), 0)
    xm1 = jnp.where(row == 0, prev[7:8, :], pltpu.roll(cur, 1, 0))
    xm2 = jnp.where(row == 0, prev[6:7, :], jnp.where(row == 1, prev[7:8, :], pltpu.roll(cur, 2, 0)))
    xp1 = jnp.where(row == tt - 1, nxt[0:1, :], pltpu.roll(cur, tt - 1, 0))
    cw = cw_ref[...]
    xc = xm2 * cw[0:1, :] + xm1 * cw[1:2, :] + cur * cw[2:3, :] + xp1 * cw[3:4, :] + cb_ref[...]
    xcb = xc.astype(BF16)
    for j in range(D_RNN // MXU_DIM):
        cs = slice(j * MXU_DIM, (j + 1) * MXU_DIM)
        z = _dot(xcb[:, cs], wg_ref[j].astype(BF16))
        r = _sigmoid(z[:, :MXU_DIM] + bg_ref[0:1, cs])
        i = _sigmoid(z[:, MXU_DIM:] + bg_ref[1:2, cs])
        a = jnp.exp(-LRU_C * r * sp[:, cs])
        a_ref[:, cs] = a
        u_ref[:, cs] = jnp.sqrt(1.0 - a * a) * i * xc[:, cs]


def _tile_scan(a_ref, u_ref, hs_ref, h, tt, reverse):
    row = lax.broadcasted_iota(I32, (8, 1), 0)
    ng = tt // 8
    for g in (range(ng - 1, -1, -1) if reverse else range(ng)):
        rows = slice(g * 8, (g + 1) * 8)
        a8, u8 = a_ref[rows, :], u_ref[rows, :]
        for s in (1, 2, 4):
            keep = (row < 8 - s) if reverse else (row >= s)
            shift = 8 - s if reverse else s
            u8 = a8 * jnp.where(keep, pltpu.roll(u8, shift, 0), 0.0) + u8
            a8 = a8 * jnp.where(keep, pltpu.roll(a8, shift, 0), 1.0)
        h8 = a8 * h + u8
        hs_ref[rows, :] = h8
        h = h8[0:1, :] if reverse else h8[7:8, :]
    return h


def _lru_fwd_kernel(xp_ref, xc_ref, xn_ref, cw_ref, cb_ref, wg_ref, bg_ref, lam_ref, h0_ref,
                    hf_ref, last_ref, carry_ref, a_ref, u_ref, *, tt):
    ti, nt = pl.program_id(1), pl.num_programs(1)

    @pl.when(ti == 0)
    def _():
        carry_ref[...] = h0_ref[...]

    _lru_conv_gates(xp_ref, xc_ref, xn_ref, cw_ref, cb_ref, wg_ref, bg_ref, lam_ref, a_ref, u_ref, ti, nt, tt)
    h = _tile_scan(a_ref, u_ref, hf_ref, carry_ref[...], tt, reverse=False)
    carry_ref[...] = h
    last_ref[...] = h


def _lru_bwd_kernel(xp_ref, xc_ref, xn_ref, cw_ref, cb_ref, wg_ref, bg_ref, lam_ref, h0_ref,
                    gate_ref, hf_ref, x_ref, mod_ref, g2_ref, w_ref,
                    xo_ref, h2_ref, last_ref, carry_ref, a_ref, u_ref, hb_ref, wbf_ref, *, tt):
    ti, nt = pl.program_id(1), pl.num_programs(1)

    @pl.when((pl.program_id(0) == 0) & (ti == 0))
    def _():
        wbf_ref[...] = w_ref[...].astype(BF16)

    @pl.when(ti == 0)
    def _():
        carry_ref[...] = h0_ref[...]

    tr = nt - 1 - ti
    _lru_conv_gates(xp_ref, xc_ref, xn_ref, cw_ref, cb_ref, wg_ref, bg_ref, lam_ref, a_ref, u_ref, tr, nt, tt)
    h = _tile_scan(a_ref, u_ref, hb_ref, carry_ref[...], tt, reverse=True)
    carry_ref[...] = h
    last_ref[...] = h
    g = gate_ref[...]
    gelu = 0.5 * g * (1.0 + jnp.tanh(math.sqrt(2.0 / math.pi) * (g + 0.044715 * (g * g * g))))
    y = (gelu * (hf_ref[...] + hb_ref[...])).astype(BF16)
    _proj_tail(_dot(y, wbf_ref[...]), x_ref, mod_ref, g2_ref, xo_ref, h2_ref)


def _lru_mixer(gx, x, state, conv_w, conv_b, w_rg, b_rg, w_ig, b_ig, lam, w_out, mod3, mod_base, per_batch_mod,
               g2, b, t, tt=256):
    nt = t // tt
    per_tile = MXU_DIM // BW_C
    eye = jnp.eye(per_tile, dtype=F32)

    def bd(w):
        w4 = w.reshape(N_BLK_C // per_tile, per_tile, BW_C, BW_C)
        return (w4[:, :, :, None, :] * eye[None, :, None, :, None]).reshape(-1, MXU_DIM, MXU_DIM)

    full = lambda shape: pl.BlockSpec(shape, lambda bi, ti: (0,) * len(shape))
    finals, xo, h2, hf = [], None, None, None
    for d in range(2):
        tile = (lambda ti: ti) if d == 0 else (lambda ti: nt - 1 - ti)
        x_spec = lambda f: pl.BlockSpec((tt, D_RNN), f)
        row = lambda bi, ti, tile=tile: (bi * nt + tile(ti), 0)
        halo = lambda f: pl.BlockSpec((8, D_RNN), f)
        in_specs = [halo(lambda bi, ti, tile=tile: (jnp.maximum((bi * nt + tile(ti)) * (tt // 8) - 1, 0), 1)),
                    x_spec(lambda bi, ti, tile=tile: (bi * nt + tile(ti), 1)),
                    halo(lambda bi, ti, tile=tile: (jnp.minimum((bi * nt + tile(ti) + 1) * (tt // 8), b * t // 8 - 1), 1)),
                    full((4, D_RNN)), full((1, D_RNN)), full((4, MXU_DIM, 2 * MXU_DIM)),
                    full((2, D_RNN)), full((1, D_RNN)),
                    pl.BlockSpec((None, 1, D_RNN), lambda bi, ti: (bi, 0, 0))]
        args = [gx, gx, gx, conv_w, conv_b.reshape(1, D_RNN),
                jnp.concatenate([bd(w_rg[d]), bd(w_ig[d])], axis=-1), jnp.stack([b_rg[d], b_ig[d]]),
                lam[d].reshape(1, D_RNN), state[:, d].reshape(b, 1, D_RNN)]
        scratch = [pltpu.VMEM((1, D_RNN), F32), pltpu.VMEM((tt, D_RNN), F32), pltpu.VMEM((tt, D_RNN), F32)]
        last_spec = pl.BlockSpec((None, 1, D_RNN), lambda bi, ti: (bi, 0, 0))
        last_shape = jax.ShapeDtypeStruct((b, 1, D_RNN), F32)
        if d == 0:
            hf, last_f = pl.pallas_call(
                functools.partial(_lru_fwd_kernel, tt=tt),
                grid=(b, nt), in_specs=in_specs,
                out_specs=[pl.BlockSpec((tt, D_RNN), row), last_spec],
                out_shape=[jax.ShapeDtypeStruct((b * t, D_RNN), F32), last_shape],
                scratch_shapes=scratch,
                compiler_params=_cparams(2, 48),
                name="lru_fwd",
            )(*args)
        else:
            in_specs += [pl.BlockSpec((tt, D_RNN), row), pl.BlockSpec((tt, D_RNN), row), pl.BlockSpec((tt, D), row),
                         pl.BlockSpec((None, 1, 6 * D), lambda bi, ti: (mod_base + bi * per_batch_mod, 0, 0)),
                         full((1, D)), full((D_RNN, D))]
            args += [gx, hf, x, mod3, g2.reshape(1, D), w_out]
            xo, h2, last_b = pl.pallas_call(
                functools.partial(_lru_bwd_kernel, tt=tt),
                grid=(b, nt), in_specs=in_specs,
                out_specs=[pl.BlockSpec((tt, D), row), pl.BlockSpec((tt, D), row), last_spec],
                out_shape=[jax.ShapeDtypeStruct((b * t, D), F32), jax.ShapeDtypeStruct((b * t, D), BF16), last_shape],
                scratch_shapes=scratch + [pltpu.VMEM((tt, D_RNN), F32), pltpu.VMEM((D_RNN, D), BF16)],
                compiler_params=_cparams(2, 48),
                name="lru_bwd_proj",
            )(*args)
    return xo, h2, jnp.concatenate([last_f, last_b], axis=1)


def _select_kernel(h_ref, wr_ref, slot_ref, aff_ref, tri_ref, *, bs, t, cap):
    @pl.when(pl.program_id(0) == 0)
    def _():
        r = lax.broadcasted_iota(I32, (t, t), 0)
        c = lax.broadcasted_iota(I32, (t, t), 1)
        tri_ref[...] = jnp.where(r < c, 1.0, 0.0).astype(BF16)

    wr = wr_ref[...].astype(BF16)
    affs = []
    for s in range(bs):
        logits = _dot_nt(wr, h_ref[s])
        ex = jnp.exp(logits - jnp.max(logits, axis=0, keepdims=True))
        affs.append(ex / jnp.sum(ex, axis=0, keepdims=True))
    aff = jnp.concatenate(affs, axis=0) if bs > 1 else affs[0]
    bits = pltpu.bitcast(aff, I32)
    count = lambda mask: jnp.sum(jnp.where(mask, 1.0, 0.0), axis=-1, keepdims=True)
    th = jnp.zeros((bs * N_EXPERTS, 1), I32)
    for bit in range(30, -1, -1):
        cand = th | (1 << bit)
        th = jnp.where(count(bits >= cand) >= cap, cand, th)
    gt, eq = bits > th, bits == th
    need = cap - count(gt)
    lane = lax.broadcasted_iota(I32, (1, t), 1)
    lim = jnp.zeros((bs * N_EXPERTS, 1), I32)
    for bit in range(t.bit_length() - 1, -1, -1):
        cand = lim | (1 << bit)
        ok = (cand <= t) & (count(eq & (lane < cand)) <= need)
        lim = jnp.where(ok, cand, lim)
    sel = gt | (eq & (lane < lim))
    pos = _dot(jnp.where(sel, 1.0, 0.0).astype(BF16), tri_ref[...])
    slot = jnp.where(sel, pos.astype(I32), -1)
    for s in range(bs):
        slot_ref[s] = slot[s * N_EXPERTS:(s + 1) * N_EXPERTS, :]
        aff_ref[s] = aff[s * N_EXPERTS:(s + 1) * N_EXPERTS, :]


def _moe_select(h2, w_router, b, t, cap, bs):
    return pl.pallas_call(
        functools.partial(_select_kernel, bs=bs, t=t, cap=cap),
        grid=(b // bs,),
        in_specs=[pl.BlockSpec((bs, t, D), lambda i: (i, 0, 0)),
                  pl.BlockSpec((N_EXPERTS, D), lambda i: (0, 0))],
        out_specs=[pl.BlockSpec((bs, N_EXPERTS, t), lambda i: (i, 0, 0))] * 2,
        out_shape=[jax.ShapeDtypeStruct((b, N_EXPERTS, t), I32), jax.ShapeDtypeStruct((b, N_EXPERTS, t), F32)],
        scratch_shapes=[pltpu.VMEM((t, t), BF16)],
        compiler_params=_cparams(1, 48),
        name="moe_select",
    )(h2, w_router.T)


def _ffn_kernel(slot_ref, aff_ref, h_ref, wg_ref, wu_ref, wd_ref, ys_ref,
                wgb_ref, wub_ref, wdb_ref, xs_ref, gc_ref, *, bg, cap):
    @pl.when(pl.program_id(1) == 0)
    def _():
        wgb_ref[...] = wg_ref[...].astype(BF16)
        wub_ref[...] = wu_ref[...].astype(BF16)
        wdb_ref[...] = wd_ref[...].astype(BF16)

    j = lax.broadcasted_iota(I32, (cap, 1), 0)
    for s in range(bg):
        onehot = slot_ref[s] == j
        xs_ref[s * cap:(s + 1) * cap, :] = _dot(jnp.where(onehot, 1.0, 0.0).astype(BF16), h_ref[s]).astype(BF16)
        gc_ref[s * cap:(s + 1) * cap, :] = jnp.sum(jnp.where(onehot, aff_ref[s], 0.0), axis=-1, keepdims=True)
    xs = xs_ref[...]
    zg = _dot(xs, wgb_ref[...])
    hid = (zg * _sigmoid(zg) * _dot(xs, wub_ref[...])).astype(BF16)
    ys_ref[...] = (_dot(hid, wdb_ref[...]) * gc_ref[...]).astype(BF16)


def _moe_ffn(slot, aff, h2, w_gate, w_up, w_down, layer, b, t, cap, bg):
    f = w_gate.shape[-1]
    sa_spec = pl.BlockSpec((bg, None, 1, t), lambda e, i: (i, e, 0, 0))
    w_spec = lambda d0, d1: pl.BlockSpec((None, None, d0, d1), lambda e, i: (layer, e, 0, 0))
    return pl.pallas_call(
        functools.partial(_ffn_kernel, bg=bg, cap=cap),
        grid=(N_EXPERTS, b // bg),
        in_specs=[sa_spec, sa_spec, pl.BlockSpec((bg, t, D), lambda e, i: (i, 0, 0)),
                  w_spec(D, f), w_spec(D, f), w_spec(f, D)],
        out_specs=pl.BlockSpec((None, bg * cap, D), lambda e, i: (e, i, 0)),
        out_shape=jax.ShapeDtypeStruct((N_EXPERTS, b * cap, D), BF16),
        scratch_shapes=[pltpu.VMEM((D, f), BF16), pltpu.VMEM((D, f), BF16), pltpu.VMEM((f, D), BF16),
                        pltpu.VMEM((bg * cap, D), BF16), pltpu.VMEM((bg * cap, 1), F32)],
        compiler_params=_cparams(2, 56),
        name="moe_ffn",
    )(slot, aff, h2, w_gate, w_up, w_down)


def _combine_kernel(slot_ref, ys_ref, x_ref, mod_ref, o_ref, *, cap):
    n = N_EXPERTS * cap
    e_of = lax.broadcasted_iota(I32, (N_EXPERTS, n), 1) >> (cap.bit_length() - 1)
    rep = jnp.where(e_of == lax.broadcasted_iota(I32, (N_EXPERTS, n), 0), 1.0, 0.0).astype(BF16)
    slot_rep = _dot(slot_ref[...].astype(F32).astype(BF16), rep)
    jn = (lax.broadcasted_iota(I32, (1, n), 1) & (cap - 1)).astype(F32)
    onehot = jnp.where(slot_rep == jn, 1.0, 0.0).astype(BF16)
    y = _dot(onehot, ys_ref[...].reshape(n, D))
    o_ref[...] = x_ref[...] + mod_ref[:, 5 * D:6 * D] * y


def _moe_combine(slot_t, ys, x, mod3, mod_base, per_batch_mod, b, t, cap, tt=256):
    nt = t // tt
    return pl.pallas_call(
        functools.partial(_combine_kernel, cap=cap),
        grid=(b, nt),
        in_specs=[pl.BlockSpec((None, tt, N_EXPERTS), lambda bi, ti: (bi, ti, 0)),
                  pl.BlockSpec((N_EXPERTS, None, cap, D), lambda bi, ti: (0, bi, 0, 0)),
                  pl.BlockSpec((tt, D), lambda bi, ti: (bi * nt + ti, 0)),
                  pl.BlockSpec((None, 1, 6 * D), lambda bi, ti: (mod_base + bi * per_batch_mod, 0, 0))],
        out_specs=pl.BlockSpec((tt, D), lambda bi, ti: (bi * nt + ti, 0)),
        out_shape=jax.ShapeDtypeStruct((b * t, D), F32),
        compiler_params=_cparams(2, 48),
        name="moe_combine",
    )(slot_t, ys, x, mod3)


def _ec_moe(x, h2, mod3, mod_base, per_batch_mod, moe_w, layer, b, t, bs, bg):
    w_router, w_gate, w_up, w_down = moe_w
    cap = EC_FACTOR * t // N_EXPERTS
    h3 = h2.reshape(b, t, D)
    slot, aff = _moe_select(h3, w_router[layer], b, t, cap, bs)
    ys = _moe_ffn(slot.reshape(b, N_EXPERTS, 1, t), aff.reshape(b, N_EXPERTS, 1, t), h3,
                  w_gate, w_up, w_down, layer, b, t, cap, bg)
    return _moe_combine(jnp.swapaxes(slot, 1, 2), ys.reshape(N_EXPERTS, b, cap, D), x,
                        mod3, mod_base, per_batch_mod, b, t, cap)


def kernel(x_prompt, x_sample, cache_a_k, cache_a_v, cache_b_k, cache_b_v, state_c_h, c, c_ctx, ada_w, ada_b, norm_mix_g, norm_ffn_g, a_w_in, a_q_norm_g, a_k_norm_g, a_lam_q, a_lam_k, a_subln_g, a_w_out, b_w_in, b_q_norm_g, b_k_norm_g, b_sink, b_w_out, c_w_in, c_conv_w, c_conv_b, c_w_rg, c_b_rg, c_w_ig, c_b_ig, c_lam, c_w_out, moe_w_router, moe_w_gate, moe_w_up, moe_w_down):
    bp, tp, _ = x_prompt.shape
    bs_, ts, _ = x_sample.shape
    past = cache_a_k.shape[2]
    cvec = jnp.concatenate([c, c_ctx[None, :], jnp.zeros((16 - bs_ - 1, D), F32)], axis=0)
    mods = _modulation(cvec, ada_w, ada_b)
    groups = {"p": (bp, tp, bs_, 0), "s": (bs_, ts, 0, 1)}
    xs = {"p": x_prompt.reshape(bp * tp, D), "s": x_sample.reshape(bs_ * ts, D)}
    ck_a = cache_a_k.reshape(bs_, -1, past, H_A * 2 * HD_A)
    cv_a = cache_a_v.reshape(bs_, -1, past, H_A * 2 * HD_A)
    ck_b = cache_b_k.reshape(bs_, -1, past, KV_B * HD_B)
    cv_b = cache_b_v.reshape(bs_, -1, past, KV_B * HD_B)
    moe_w = (moe_w_router, moe_w_gate, moe_w_up, moe_w_down)
    new_a_k, new_a_v, new_b_k, new_b_v, new_c_h = [], [], [], [], []
    ia = ib = ic = 0
    for l in range(DEPTH):
        mod3 = mods[l].reshape(16, 1, 6 * D)
        kind = l % 3
        for key in ("p", "s"):
            b, t, mbase, per_b = groups[key]
            rows_per_mod = t if per_b else b * t
            x = xs[key]
            sample = key == "s"
            if kind == 0:
                lam_init = 0.8 - 0.6 * math.exp(-0.3 * l)
                qkv = _norm_mod_matmul(x, mod3, mbase, rows_per_mod, norm_mix_g[l], a_w_in[ia])
                outs = _qk_prep(qkv, a_q_norm_g[ia], a_k_norm_g[ia], D, D, D, t, rope=sample, k_f32=not sample)
                q, k, v = outs[:3]
                if not sample:
                    new_a_k.append(outs[3].reshape(b, t, H_A, 2 * HD_A))
                    new_a_v.append(qkv[:, 2 * D:].reshape(b, t, H_A, 2 * HD_A))
                o = _diff_attention(q, k, v, (ck_a, cv_a, ia) if sample else None, a_lam_q[ia], a_lam_k[ia],
                                    a_subln_g[ia], lam_init, b, t, heads=1 if sample else H_A,
                                    tq=512 if sample else t, nsub=4 if sample else 1)
                x, h2 = _proj_residual(o, x, mod3, mbase, rows_per_mod, norm_ffn_g[l], a_w_out[ia])
            elif kind == 1:
                nq, nk = H_B * HD_B, KV_B * HD_B
                qkv = _norm_mod_matmul(x, mod3, mbase, rows_per_mod, norm_mix_g[l], b_w_in[ib])
                outs = _qk_prep(qkv, b_q_norm_g[ib], b_k_norm_g[ib], nq, nk, nk, t, rope=sample, k_f32=not sample)
                q, k, v = outs[:3]
                if not sample:
                    new_b_k.append(outs[3].reshape(b, t, KV_B, HD_B))
                    new_b_v.append(qkv[:, nq + nk:].reshape(b, t, KV_B, HD_B))
                o = _gqa_attention(q, k, v, (ck_b, cv_b, ib) if sample else None, b_sink[ib], b, t,
                                   windowed=sample, tq=256)
                x, h2 = _proj_residual(o, x, mod3, mbase, rows_per_mod, norm_ffn_g[l], b_w_out[ib])
            else:
                gx = _norm_mod_matmul(x, mod3, mbase, rows_per_mod, norm_mix_g[l], c_w_in[ic])
                state = state_c_h[:, ic] if sample else jnp.zeros((b, 2, D_RNN), F32)
                x, h2, finals = _lru_mixer(gx, x, state, c_conv_w[ic], c_conv_b[ic], c_w_rg[ic], c_b_rg[ic],
                                           c_w_ig[ic], c_b_ig[ic], c_lam[ic], c_w_out[ic], mod3, mbase, per_b,
                                           norm_ffn_g[l], b, t)
                if not sample:
                    new_c_h.append(finals)
            xs[key] = _ec_moe(x, h2, mod3, mbase, per_b, moe_w, l, b, t,
                              bs=2 if sample else 8, bg=2 if sample else 16)
        ia, ib, ic = ia + (kind == 0), ib + (kind == 1), ic + (kind == 2)
    return (xs["p"].reshape(bp, tp, D), xs["s"].reshape(bs_, ts, D),
            jnp.stack(new_a_k, axis=1), jnp.stack(new_a_v, axis=1),
            jnp.stack(new_b_k, axis=1), jnp.stack(new_b_v, axis=1), jnp.stack(new_c_h, axis=1))
```

```python
import functools
import math

import jax
import jax.numpy as jnp
from jax import lax
from jax.experimental import pallas as pl
from jax.experimental.pallas import tpu as pltpu

F32, BF16, I32 = jnp.float32, jnp.bfloat16, jnp.int32

D = 1024
DEPTH = 4
GRID_W = 64
H_A, HD_A = 8, 64
H_B, KV_B, G_B, HD_B = 16, 4, 4, 64
WINDOW = 128
D_RNN = 1024
N_BLK_C, BW_C = 16, 64
LRU_C = 8.0
N_EXPERTS = 16
EC_FACTOR = 2
ROPE_THETA = 10000.0
EPS = 1e-6
LOG2E = math.log2(math.e)
LANES = 128
MXU_DIM = 256
MIB = 1024 * 1024


def _cparams(n_axes, vmem_mib=48):
    return pltpu.CompilerParams(dimension_semantics=("arbitrary",) * n_axes,
                                vmem_limit_bytes=vmem_mib * MIB)


def _sigmoid(x):
    return 1.0 / (1.0 + jnp.exp(-x))


def _rms(x):
    return x * lax.rsqrt(jnp.mean(x * x, axis=-1, keepdims=True) + EPS)


def _dot(a, b):
    return jnp.dot(a, b, preferred_element_type=F32)


def _dot_nt(a, b):
    return lax.dot_general(a, b, (((1,), (1,)), ((), ())), preferred_element_type=F32)


def _mod_kernel(c_ref, w_ref, b_ref, o_ref):
    c = c_ref[...]
    s = (c * _sigmoid(c)).astype(BF16)
    o_ref[...] = _dot(s, w_ref[...].astype(BF16)) + b_ref[...]


def _modulation(cvec, ada_w, ada_b):
    nt = 1536
    return pl.pallas_call(
        _mod_kernel,
        grid=(DEPTH, 6 * D // nt),
        in_specs=[pl.BlockSpec((16, D), lambda l, j: (0, 0)),
                  pl.BlockSpec((None, D, nt), lambda l, j: (l, 0, j)),
                  pl.BlockSpec((None, 1, nt), lambda l, j: (l, 0, j))],
        out_specs=pl.BlockSpec((None, 16, nt), lambda l, j: (l, 0, j)),
        out_shape=jax.ShapeDtypeStruct((DEPTH, 16, 6 * D), F32),
        compiler_params=_cparams(2, 32),
        name="adaln_mod",
    )(cvec, ada_w, ada_b.reshape(DEPTH, 1, 6 * D))


def _nmm_kernel(x_ref, mod_ref, g_ref, w_ref, o_ref, wbf_ref):
    @pl.when(pl.program_id(0) == 0)
    def _():
        wbf_ref[...] = w_ref[...].astype(BF16)

    m = mod_ref[...]
    h = (_rms(x_ref[...]) * g_ref[...] * (1.0 + m[:, D:2 * D]) + m[:, 0:D]).astype(BF16)
    o_ref[...] = _dot(h, wbf_ref[...])


def _norm_mod_matmul(x, mod3, mod_base, rows_per_mod, g, w, tm=512):
    r, n = x.shape[0], w.shape[1]
    return pl.pallas_call(
        _nmm_kernel,
        grid=(r // tm,),
        in_specs=[pl.BlockSpec((tm, D), lambda i: (i, 0)),
                  pl.BlockSpec((None, 1, 6 * D), lambda i: (mod_base + (i * tm) // rows_per_mod, 0, 0)),
                  pl.BlockSpec((1, D), lambda i: (0, 0)),
                  pl.BlockSpec((D, n), lambda i: (0, 0), pipeline_mode=pl.Buffered(1))],
        out_specs=pl.BlockSpec((tm, n), lambda i: (i, 0)),
        out_shape=jax.ShapeDtypeStruct((r, n), F32),
        scratch_shapes=[pltpu.VMEM((D, n), BF16)],
        compiler_params=_cparams(1, 48),
        name="norm_mod_matmul",
    )(x, mod3, g.reshape(1, D), w)


def _proj_tail(y, x_ref, mod_ref, g2_ref, xo_ref, h2_ref):
    m = mod_ref[...]
    xn = x_ref[...] + m[:, 2 * D:3 * D] * y
    xo_ref[...] = xn
    h2_ref[...] = (_rms(xn) * g2_ref[...] * (1.0 + m[:, 4 * D:5 * D]) + m[:, 3 * D:4 * D]).astype(BF16)


def _proj_kernel(o_ref, x_ref, mod_ref, g2_ref, w_ref, xo_ref, h2_ref, wbf_ref):
    @pl.when(pl.program_id(0) == 0)
    def _():
        wbf_ref[...] = w_ref[...].astype(BF16)

    _proj_tail(_dot(o_ref[...], wbf_ref[...]), x_ref, mod_ref, g2_ref, xo_ref, h2_ref)


def _proj_residual(o, x, mod3, mod_base, rows_per_mod, g2, w, tm=512):
    r = x.shape[0]
    return pl.pallas_call(
        _proj_kernel,
        grid=(r // tm,),
        in_specs=[pl.BlockSpec((tm, D), lambda i: (i, 0)),
                  pl.BlockSpec((tm, D), lambda i: (i, 0)),
                  pl.BlockSpec((None, 1, 6 * D), lambda i: (mod_base + (i * tm) // rows_per_mod, 0, 0)),
                  pl.BlockSpec((1, D), lambda i: (0, 0)),
                  pl.BlockSpec((D, D), lambda i: (0, 0))],
        out_specs=[pl.BlockSpec((tm, D), lambda i: (i, 0)), pl.BlockSpec((tm, D), lambda i: (i, 0))],
        out_shape=[jax.ShapeDtypeStruct((r, D), F32), jax.ShapeDtypeStruct((r, D), BF16)],
        scratch_shapes=[pltpu.VMEM((D, D), BF16)],
        compiler_params=_cparams(1, 40),
        name="proj_residual",
    )(o, x, mod3, g2.reshape(1, D), w)


def _group_mean_sq(x):
    ss = x * x
    hi = ss.astype(BF16)
    lo = (ss - hi.astype(F32)).astype(BF16)
    r = lax.broadcasted_iota(I32, (LANES, LANES), 0) >> 6
    c = lax.broadcasted_iota(I32, (LANES, LANES), 1) >> 6
    ones_bd = jnp.where(r == c, 1.0, 0.0).astype(BF16)
    return (_dot(hi, ones_bd) + _dot(lo, ones_bd)) * (1.0 / HD_A)


def _rope128(y, cos, sa, sb):
    return y * cos + pltpu.roll(y, LANES - 16, 1) * sa + pltpu.roll(y, 16, 1) * sb


def _prep_kernel(*refs, nq, nk, rope, k_f32):
    it = iter(refs)
    qkv_ref, qg_ref, kg_ref = next(it), next(it), next(it)
    if rope:
        cos_ref, sa_ref, sb_ref = next(it), next(it), next(it)
    q_ref, k_ref, v_ref = next(it), next(it), next(it)
    kf_ref = next(it) if k_f32 else None
    if rope:
        cos, sa, sb = cos_ref[...], sa_ref[...], sb_ref[...]
    for j in range((nq + nk) // LANES):
        x = qkv_ref[:, j * LANES:(j + 1) * LANES]
        is_q = j < nq // LANES
        y = x * lax.rsqrt(_group_mean_sq(x) + EPS) * (qg_ref[...] if is_q else kg_ref[...])
        if rope:
            y = _rope128(y, cos, sa, sb)
        if is_q:
            q_ref[:, j * LANES:(j + 1) * LANES] = (y * (HD_A ** -0.5 * LOG2E)).astype(BF16)
        else:
            jj = j - nq // LANES
            k_ref[:, jj * LANES:(jj + 1) * LANES] = y.astype(BF16)
            if k_f32:
                kf_ref[:, jj * LANES:(jj + 1) * LANES] = y
    v_ref[...] = qkv_ref[:, nq + nk:].astype(BF16)


def _rope_tables(t):
    n_freq = HD_A // 4
    inv_freq = ROPE_THETA ** (-jnp.arange(n_freq, dtype=F32) / n_freq)
    pos_row = jnp.repeat(jnp.arange(t // GRID_W), GRID_W).astype(F32)
    pos_col = jnp.tile(jnp.arange(GRID_W), t // GRID_W).astype(F32)
    ang_r = pos_row[:, None] * inv_freq[None, :]
    ang_c = pos_col[:, None] * inv_freq[None, :]
    z = jnp.zeros_like(ang_r)
    cos64 = jnp.concatenate([jnp.cos(ang_r), jnp.cos(ang_r), jnp.cos(ang_c), jnp.cos(ang_c)], axis=-1)
    sa64 = jnp.concatenate([-jnp.sin(ang_r), z, -jnp.sin(ang_c), z], axis=-1)
    sb64 = jnp.concatenate([z, jnp.sin(ang_r), z, jnp.sin(ang_c)], axis=-1)
    return tuple(jnp.tile(a, (1, 2)) for a in (cos64, sa64, sb64))


def _qk_prep(qkv, qg, kg, nq, nk, nv, t, rope, k_f32, tm=256):
    r = qkv.shape[0]
    tile2 = lambda g: jnp.tile(g.reshape(1, HD_A), (1, 2))
    args = [qkv, tile2(qg), tile2(kg)]
    in_specs = [pl.BlockSpec((tm, nq + nk + nv), lambda i: (i, 0)),
                pl.BlockSpec((1, LANES), lambda i: (0, 0)),
                pl.BlockSpec((1, LANES), lambda i: (0, 0))]
    if rope:
        args += list(_rope_tables(t))
        in_specs += [pl.BlockSpec((tm, LANES), lambda i: (i % (t // tm), 0))] * 3
    out_shape = [jax.ShapeDtypeStruct((r, nq), BF16), jax.ShapeDtypeStruct((r, nk), BF16),
                 jax.ShapeDtypeStruct((r, nv), BF16)]
    out_specs = [pl.BlockSpec((tm, nq), lambda i: (i, 0)), pl.BlockSpec((tm, nk), lambda i: (i, 0)),
                 pl.BlockSpec((tm, nv), lambda i: (i, 0))]
    if k_f32:
        out_shape.append(jax.ShapeDtypeStruct((r, nk), F32))
        out_specs.append(pl.BlockSpec((tm, nk), lambda i: (i, 0)))
    return pl.pallas_call(
        functools.partial(_prep_kernel, nq=nq, nk=nk, rope=rope, k_f32=k_f32),
        grid=(r // tm,),
        in_specs=in_specs, out_specs=out_specs, out_shape=out_shape,
        compiler_params=_cparams(1, 32),
        name="qk_prep",
    )(*args)


def _diff_attn_kernel(*refs, heads, ctx, lam_init, nsub):
    it = iter(refs)
    q_ref, k_ref, v_ref = next(it), next(it), next(it)
    if ctx:
        ck_ref, cv_ref = next(it), next(it)
    lq_ref, lk_ref, sg_ref, o_ref = next(it), next(it), next(it), next(it)
    e = jnp.exp(jnp.sum(lq_ref[...] * lk_ref[...], axis=-1, keepdims=True))
    lam = e[0:1, :] - e[1:2, :] + lam_init
    lane = lax.broadcasted_iota(I32, (1, LANES), 1)
    tqs = q_ref.shape[0] // nsub
    units = []
    for h in range(heads):
        sl = slice(h * LANES, (h + 1) * LANES)
        k = k_ref[:, sl]
        ck = ck_ref[:, sl].astype(BF16) if ctx else None
        for j in range(nsub):
            rows = slice(j * tqs, (j + 1) * tqs)
            q = q_ref[rows, sl]
            zero = jnp.zeros_like(q)
            scores = []
            for c in range(2):
                qc = jnp.where((lane < HD_A) if c == 0 else (lane >= HD_A), q, zero)
                scores.append((_dot_nt(qc, k), _dot_nt(qc, ck) if ctx else None))
            units.append((sl, rows, scores))
    for sl, rows, scores in units:
        ps, ls = [], []
        for s, sc in scores:
            m = jnp.max(s, axis=-1, keepdims=True)
            if ctx:
                m = jnp.maximum(m, jnp.max(sc, axis=-1, keepdims=True))
            p = jnp.exp2(s - m)
            l = jnp.sum(p, axis=-1, keepdims=True)
            pc = None
            if ctx:
                pc = jnp.exp2(sc - m)
                l = l + jnp.sum(pc, axis=-1, keepdims=True)
            ps.append((p, pc))
            ls.append(l)
        ratio = lam * ls[0] / ls[1]
        o = _dot((ps[0][0] - ratio * ps[1][0]).astype(BF16), v_ref[:, sl])
        if ctx:
            o = o + _dot((ps[0][1] - ratio * ps[1][1]).astype(BF16), cv_ref[:, sl].astype(BF16))
        o = o * (1.0 / ls[0])
        o_ref[rows, sl] = (_rms(o) * sg_ref[...] * (1.0 - lam_init)).astype(BF16)


def _diff_attention(q, k, v, cache, lam_q, lam_k, subln_g, lam_init, b, t, heads, tq, nsub):
    nq = t // tq
    hb = H_A // heads
    args = [q, k, v]
    in_specs = [pl.BlockSpec((tq, heads * LANES), lambda bi, hi, qi: (bi * nq + qi, hi)),
                pl.BlockSpec((t, heads * LANES), lambda bi, hi, qi: (bi, hi)),
                pl.BlockSpec((t, heads * LANES), lambda bi, hi, qi: (bi, hi))]
    if cache is not None:
        ck, cv, la = cache
        p = ck.shape[2]
        args += [ck, cv]
        in_specs += [pl.BlockSpec((None, None, p, heads * LANES), lambda bi, hi, qi: (bi, la, 0, hi))] * 2
    args += [lam_q, lam_k, subln_g.reshape(1, LANES)]
    in_specs += [pl.BlockSpec((2, HD_A), lambda bi, hi, qi: (0, 0)),
                 pl.BlockSpec((2, HD_A), lambda bi, hi, qi: (0, 0)),
                 pl.BlockSpec((1, LANES), lambda bi, hi, qi: (0, 0))]
    return pl.pallas_call(
        functools.partial(_diff_attn_kernel, heads=heads, ctx=cache is not None, lam_init=lam_init, nsub=nsub),
        grid=(b, hb, nq),
        in_specs=in_specs,
        out_specs=pl.BlockSpec((tq, heads * LANES), lambda bi, hi, qi: (bi * nq + qi, hi)),
        out_shape=jax.ShapeDtypeStruct((b * t, H_A * LANES), BF16),
        compiler_params=_cparams(3, 48),
        name="diff_attention",
    )(*args)


def _both_halves(x, half):
    lane_half = lax.broadcasted_iota(I32, (1, LANES), 1) >> 6
    xm = jnp.where(lane_half == half, x, 0.0)
    return xm + pltpu.roll(xm, HD_B, 1)


def _value_with_ones(x, half):
    lane = lax.broadcasted_iota(I32, (1, LANES), 1)
    return jnp.where(lane < HD_B, _both_halves(x, half), jnp.where(lane == HD_B, 1.0, 0.0))


def _gqa_kernel(*refs, kvs, windowed, ctx, t, tq):
    it = iter(refs)
    q_ref, k_ref, v_ref = next(it), next(it), next(it)
    if ctx:
        ck_ref, cv_ref = next(it), next(it)
    sink_ref, o_ref = next(it), next(it)
    lane_half = lax.broadcasted_iota(I32, (1, LANES), 1) >> 6
    if windowed:
        kd_ref, vd_ref, ckd_ref, cvd_ref = next(it), next(it), next(it), next(it)
        half = pl.program_id(1) % 2

        @pl.when(pl.program_id(2) == 0)
        def _():
            kd_ref[...] = _both_halves(k_ref[...].astype(F32), half).astype(BF16)
            vd_ref[...] = _value_with_ones(v_ref[...].astype(F32), half).astype(BF16)
            ckd_ref[...] = _both_halves(ck_ref[...], half).astype(BF16)
            cvd_ref[...] = _value_with_ones(cv_ref[...], half).astype(BF16)

        span = tq + 2 * WINDOW
        q0 = pl.program_id(2) * tq
        start = pl.multiple_of(jnp.clip(q0 - WINDOW, 0, t - span), LANES)
        qpos = q0 + lax.broadcasted_iota(I32, (tq, span), 0)
        kpos = start + lax.broadcasted_iota(I32, (tq, span), 1)
        valid = jnp.abs(qpos - kpos) <= WINDOW
    units = []
    for kv in range(kvs):
        if windowed:
            kd, vd = kd_ref[pl.ds(start, span), :], vd_ref[pl.ds(start, span), :]
            ckd, cvd = ckd_ref[...], cvd_ref[...]
        else:
            kcol = slice((kv // 2) * LANES, (kv // 2 + 1) * LANES)
            kd = _both_halves(k_ref[:, kcol].astype(F32), kv % 2).astype(BF16)
            vd = _value_with_ones(v_ref[:, kcol].astype(F32), kv % 2).astype(BF16)
            ckd = cvd = None
        for g in range(G_B):
            qcol = slice((kv * G_B + g) // 2 * LANES, ((kv * G_B + g) // 2 + 1) * LANES)
            q = q_ref[:, qcol]
            qm = jnp.where(lane_half == g % 2, q, jnp.zeros_like(q))
            sink = sink_ref[kv:kv + 1, g:g + 1] if kvs > 1 else sink_ref[:, g:g + 1]
            units.append((g, qcol, sink, _dot_nt(qm, kd), _dot_nt(qm, ckd) if ctx else None, vd, cvd))
    o_pair = None
    for g, qcol, sink, s, sc, vd, cvd in units:
        sink2 = sink * LOG2E
        if windowed:
            s = jnp.where(valid, s, -jnp.inf)
        m = jnp.maximum(jnp.max(s, axis=-1, keepdims=True), sink2)
        if ctx:
            m = jnp.maximum(m, jnp.max(sc, axis=-1, keepdims=True))
        ov = _dot(jnp.exp2(s - m).astype(BF16), vd)
        if ctx:
            ov = ov + _dot(jnp.exp2(sc - m).astype(BF16), cvd)
        l = ov[:, HD_B:HD_B + 1] + jnp.exp2(sink2 - m)
        o = ov * (1.0 / l)
        if g % 2 == 0:
            o_pair = jnp.where(lane_half == 0, o, 0.0)
        else:
            o_ref[:, qcol] = (o_pair + jnp.where(lane_half == 1, pltpu.roll(o, HD_B, 1), 0.0)).astype(BF16)


def _gqa_attention(q, k, v, cache, sink, b, t, windowed, tq):
    if windowed:
        kvs, nq = 1, t // tq
        grid = (b, KV_B, nq)
        qw = G_B * HD_B
        q_spec = pl.BlockSpec((tq, qw), lambda bi, kv, qi: (bi * nq + qi, kv))
        kv_spec = pl.BlockSpec((t, LANES), lambda bi, kv, qi: (bi, kv // 2))
        sink_arr = sink.reshape(KV_B, 1, G_B)
        sink_spec = pl.BlockSpec((None, 1, G_B), lambda bi, kv, qi: (kv, 0, 0))
        o_spec = pl.BlockSpec((tq, qw), lambda bi, kv, qi: (bi * nq + qi, kv))
    else:
        kvs, tq = KV_B, t
        grid = (b, 1, 1)
        q_spec = pl.BlockSpec((t, H_B * HD_B), lambda bi, kv, qi: (bi, 0))
        kv_spec = pl.BlockSpec((t, KV_B * HD_B), lambda bi, kv, qi: (bi, 0))
        sink_arr = sink.reshape(KV_B, G_B)
        sink_spec = pl.BlockSpec((KV_B, G_B), lambda bi, kv, qi: (0, 0))
        o_spec = pl.BlockSpec((t, H_B * HD_B), lambda bi, kv, qi: (bi, 0))
    args, in_specs = [q, k, v], [q_spec, kv_spec, kv_spec]
    if cache is not None:
        ck, cv, lb = cache
        p = ck.shape[2]
        args += [ck, cv]
        in_specs += [pl.BlockSpec((None, None, p, LANES), lambda bi, kv, qi: (bi, lb, 0, kv // 2))] * 2
    args.append(sink_arr)
    in_specs.append(sink_spec)
    scratch = []
    if windowed:
        scratch = [pltpu.VMEM((t, LANES), BF16)] * 2 + [pltpu.VMEM((p, LANES), BF16)] * 2
    return pl.pallas_call(
        functools.partial(_gqa_kernel, kvs=kvs, windowed=windowed, ctx=cache is not None, t=t, tq=tq),
        grid=grid, in_specs=in_specs, out_specs=o_spec,
        out_shape=jax.ShapeDtypeStruct((b * t, H_B * HD_B), BF16),
        scratch_shapes=scratch,
        compiler_params=_cparams(3, 48),
        name="gqa_attention",
    )(*args)


def _lru_gates_kernel(xp_ref, xc_ref, xn_ref, cw_ref, cb_ref, wg_ref, bg_ref, lam_ref,
                      af_ref, uf_ref, ab_ref, ub_ref, *, tt):
    ti, nt = pl.program_id(1), pl.num_programs(1)
    nl = -lam_ref[...]
    sp = jnp.maximum(nl, 0.0) + jnp.log1p(jnp.exp(-jnp.abs(nl)))
    cur = xc_ref[...]
    prev = xp_ref[...] * (ti > 0).astype(F32)
    nxt = xn_ref[...] * (ti < nt - 1).astype(F32)
    row = lax.broadcasted_iota(I32, (tt, 1), 0)
    xm1 = jnp.where(row == 0, prev[tt - 1:tt, :], pltpu.roll(cur, 1, 0))
    xm2 = jnp.where(row == 0, prev[tt - 2:tt - 1, :],
                    jnp.where(row == 1, prev[tt - 1:tt, :], pltpu.roll(cur, 2, 0)))
    xp1 = jnp.where(row == tt - 1, nxt[0:1, :], pltpu.roll(cur, tt - 1, 0))
    cw = cw_ref[...]
    xc = xm2 * cw[0:1, :] + xm1 * cw[1:2, :] + cur * cw[2:3, :] + xp1 * cw[3:4, :] + cb_ref[...]
    xcb = xc.astype(BF16)
    outs = ((af_ref, uf_ref), (ab_ref, ub_ref))
    for j in range(D_RNN // MXU_DIM):
        cs = slice(j * MXU_DIM, (j + 1) * MXU_DIM)
        z = _dot(xcb[:, cs], wg_ref[j].astype(BF16))
        xj = xc[:, cs]
        for d in range(2):
            r = _sigmoid(z[:, (2 * d) * MXU_DIM:(2 * d + 1) * MXU_DIM] + bg_ref[2 * d:2 * d + 1, cs])
            i = _sigmoid(z[:, (2 * d + 1) * MXU_DIM:(2 * d + 2) * MXU_DIM] + bg_ref[2 * d + 1:2 * d + 2, cs])
            log_a = -LRU_C * r * sp[d:d + 1, cs]
            a = jnp.exp(log_a)
            outs[d][0][:, cs] = a
            outs[d][1][:, cs] = jnp.sqrt(1.0 - a * a) * i * xj


def _lru_gates(gx, conv_w, conv_b, w_rg, b_rg, w_ig, b_ig, lam, b, t, tt=256):
    nt = t // tt
    per_tile = MXU_DIM // BW_C
    eye = jnp.eye(per_tile, dtype=F32)

    def bd(w):
        w4 = w.reshape(N_BLK_C // per_tile, per_tile, BW_C, BW_C)
        return (w4[:, :, :, None, :] * eye[None, :, None, :, None]).reshape(-1, MXU_DIM, MXU_DIM)

    wg = jnp.concatenate([bd(w_rg[0]), bd(w_ig[0]), bd(w_rg[1]), bd(w_ig[1])], axis=-1)
    bg = jnp.stack([b_rg[0], b_ig[0], b_rg[1], b_ig[1]])
    x_spec = lambda f: pl.BlockSpec((tt, D_RNN), f)
    full = lambda shape: pl.BlockSpec(shape, lambda bi, ti: (0,) * len(shape))
    o_spec = pl.BlockSpec((tt, D_RNN), lambda bi, ti: (ti, bi))
    o_shape = jax.ShapeDtypeStruct((t, b * D_RNN), F32)
    return pl.pallas_call(
        functools.partial(_lru_gates_kernel, tt=tt),
        grid=(b, nt),
        in_specs=[x_spec(lambda bi, ti: (bi * nt + jnp.maximum(ti - 1, 0), 1)),
                  x_spec(lambda bi, ti: (bi * nt + ti, 1)),
                  x_spec(lambda bi, ti: (bi * nt + jnp.minimum(ti + 1, nt - 1), 1)),
                  full((4, D_RNN)), full((1, D_RNN)), full((4, MXU_DIM, 4 * MXU_DIM)),
                  full((4, D_RNN)), full((2, D_RNN))],
        out_specs=[o_spec] * 4, out_shape=[o_shape] * 4,
        compiler_params=_cparams(2, 48),
        name="lru_gates",
    )(gx, gx, gx, conv_w, conv_b.reshape(1, D_RNN), wg, bg, lam)


def _scan_kernel(a_ref, u_ref, h0_ref, hs_ref, h_ref, *, b, tc, reverse):
    @pl.when(pl.program_id(0) == 0)
    def _():
        h_ref[...] = h0_ref[...]

    def step(i, h):
        tl = (tc - 1 - i) if reverse else i
        rows = pl.ds(pl.multiple_of(tl * b, 8), b)
        h = a_ref[rows, :] * h + u_ref[rows, :]
        hs_ref[rows, :] = h
        return h

    h_ref[...] = lax.fori_loop(0, tc, step, h_ref[...], unroll=8)


def _lru_scan(a, u, h0, b, t, reverse, rows=512):
    tc = rows // b
    nc = t // tc
    blk = pl.BlockSpec((tc * b, D_RNN), (lambda c: (nc - 1 - c, 0)) if reverse else (lambda c: (c, 0)))
    return pl.pallas_call(
        functools.partial(_scan_kernel, b=b, tc=tc, reverse=reverse),
        grid=(nc,),
        in_specs=[blk, blk, pl.BlockSpec((b, D_RNN), lambda c: (0, 0))],
        out_specs=blk,
        out_shape=jax.ShapeDtypeStruct((t * b, D_RNN), F32),
        scratch_shapes=[pltpu.VMEM((b, D_RNN), F32)],
        compiler_params=_cparams(1, 48),
        name="lru_scan",
    )(a, u, h0)


def _lru_proj_kernel(gate_ref, hf_ref, hb_ref, x_ref, mod_ref, g2_ref, w_ref, xo_ref, h2_ref, wbf_ref):
    @pl.when((pl.program_id(0) == 0) & (pl.program_id(1) == 0))
    def _():
        wbf_ref[...] = w_ref[...].astype(BF16)

    g = gate_ref[...]
    gelu = 0.5 * g * (1.0 + jnp.tanh(math.sqrt(2.0 / math.pi) * (g + 0.044715 * (g * g * g))))
    y = (gelu * (hf_ref[...] + hb_ref[...])).astype(BF16)
    _proj_tail(_dot(y, wbf_ref[...]), x_ref, mod_ref, g2_ref, xo_ref, h2_ref)


def _lru_proj_residual(gx, hf, hb, x, mod3, mod_base, per_batch_mod, g2, w, b, t, tt=256):
    nt = t // tt
    row = lambda bi, ti: (bi * nt + ti, 0)
    return pl.pallas_call(
        _lru_proj_kernel,
        grid=(b, nt),
        in_specs=[pl.BlockSpec((tt, D_RNN), row),
                  pl.BlockSpec((tt, D_RNN), lambda bi, ti: (ti, bi)),
                  pl.BlockSpec((tt, D_RNN), lambda bi, ti: (ti, bi)),
                  pl.BlockSpec((tt, D), row),
                  pl.BlockSpec((None, 1, 6 * D), lambda bi, ti: (mod_base + bi * per_batch_mod, 0, 0)),
                  pl.BlockSpec((1, D), lambda bi, ti: (0, 0)),
                  pl.BlockSpec((D_RNN, D), lambda bi, ti: (0, 0))],
        out_specs=[pl.BlockSpec((tt, D), row), pl.BlockSpec((tt, D), row)],
        out_shape=[jax.ShapeDtypeStruct((b * t, D), F32), jax.ShapeDtypeStruct((b * t, D), BF16)],
        scratch_shapes=[pltpu.VMEM((D_RNN, D), BF16)],
        compiler_params=_cparams(2, 40),
        name="lru_proj_residual",
    )(gx, hf, hb, x, mod3, g2.reshape(1, D), w)


def _lru_conv_gates(xp_ref, xc_ref, xn_ref, cw_ref, cb_ref, wg_ref, bg_ref, lam_ref, a_ref, u_ref, ti, nt, tt):
    nl = -lam_ref[...]
    sp = jnp.maximum(nl, 0.0) + jnp.log1p(jnp.exp(-jnp.abs(nl)))
    cur = xc_ref[...]
    prev = xp_ref[...] * (ti > 0).astype(F32)
    nxt = xn_ref[...] * (ti < nt - 1).astype(F32)
    row = lax.broadcasted_iota(I32, (tt, 1), 0)
    xm1 = jnp.where(row == 0, prev[7:8, :], pltpu.roll(cur, 1, 0))
    xm2 = jnp.where(row == 0, prev[6:7, :], jnp.where(row == 1, prev[7:8, :], pltpu.roll(cur, 2, 0)))
    xp1 = jnp.where(row == tt - 1, nxt[0:1, :], pltpu.roll(cur, tt - 1, 0))
    cw = cw_ref[...]
    xc = xm2 * cw[0:1, :] + xm1 * cw[1:2, :] + cur * cw[2:3, :] + xp1 * cw[3:4, :] + cb_ref[...]
    xcb = xc.astype(BF16)
    for j in range(D_RNN // MXU_DIM):
        cs = slice(j * MXU_DIM, (j + 1) * MXU_DIM)
        z = _dot(xcb[:, cs], wg_ref[j].astype(BF16))
        r = _sigmoid(z[:, :MXU_DIM] + bg_ref[0:1, cs])
        i = _sigmoid(z[:, MXU_DIM:] + bg_ref[1:2, cs])
        a = jnp.exp(-LRU_C * r * sp[:, cs])
        a_ref[:, cs] = a
        u_ref[:, cs] = jnp.sqrt(1.0 - a * a) * i * xc[:, cs]


def _tile_scan(a_ref, u_ref, hs_ref, h, tt, reverse):
    row = lax.broadcasted_iota(I32, (8, 1), 0)
    ng = tt // 8
    for g in (range(ng - 1, -1, -1) if reverse else range(ng)):
        rows = slice(g * 8, (g + 1) * 8)
        a8, u8 = a_ref[rows, :], u_ref[rows, :]
        for s in (1, 2, 4):
            keep = (row < 8 - s) if reverse else (row >= s)
            shift = 8 - s if reverse else s
            u8 = a8 * jnp.where(keep, pltpu.roll(u8, shift, 0), 0.0) + u8
            a8 = a8 * jnp.where(keep, pltpu.roll(a8, shift, 0), 1.0)
        h8 = a8 * h + u8
        hs_ref[rows, :] = h8
        h = h8[0:1, :] if reverse else h8[7:8, :]
    return h


def _lru_fwd_kernel(xp_ref, xc_ref, xn_ref, cw_ref, cb_ref, wg_ref, bg_ref, lam_ref, h0_ref,
                    hf_ref, last_ref, carry_ref, a_ref, u_ref, *, tt):
    ti, nt = pl.program_id(1), pl.num_programs(1)

    @pl.when(ti == 0)
    def _():
        carry_ref[...] = h0_ref[...]

    _lru_conv_gates(xp_ref, xc_ref, xn_ref, cw_ref, cb_ref, wg_ref, bg_ref, lam_ref, a_ref, u_ref, ti, nt, tt)
    h = _tile_scan(a_ref, u_ref, hf_ref, carry_ref[...], tt, reverse=False)
    carry_ref[...] = h
    last_ref[...] = h


def _lru_bwd_kernel(xp_ref, xc_ref, xn_ref, cw_ref, cb_ref, wg_ref, bg_ref, lam_ref, h0_ref,
                    gate_ref, hf_ref, x_ref, mod_ref, g2_ref, w_ref,
                    xo_ref, h2_ref, last_ref, carry_ref, a_ref, u_ref, hb_ref, wbf_ref, *, tt):
    ti, nt = pl.program_id(1), pl.num_programs(1)

    @pl.when((pl.program_id(0) == 0) & (ti == 0))
    def _():
        wbf_ref[...] = w_ref[...].astype(BF16)

    @pl.when(ti == 0)
    def _():
        carry_ref[...] = h0_ref[...]

    tr = nt - 1 - ti
    _lru_conv_gates(xp_ref, xc_ref, xn_ref, cw_ref, cb_ref, wg_ref, bg_ref, lam_ref, a_ref, u_ref, tr, nt, tt)
    h = _tile_scan(a_ref, u_ref, hb_ref, carry_ref[...], tt, reverse=True)
    carry_ref[...] = h
    last_ref[...] = h
    g = gate_ref[...]
    gelu = 0.5 * g * (1.0 + jnp.tanh(math.sqrt(2.0 / math.pi) * (g + 0.044715 * (g * g * g))))
    y = (gelu * (hf_ref[...] + hb_ref[...])).astype(BF16)
    _proj_tail(_dot(y, wbf_ref[...]), x_ref, mod_ref, g2_ref, xo_ref, h2_ref)


def _lru_mixer(gx, x, state, conv_w, conv_b, w_rg, b_rg, w_ig, b_ig, lam, w_out, mod3, mod_base, per_batch_mod,
               g2, b, t, tt=256):
    nt = t // tt
    per_tile = MXU_DIM // BW_C
    eye = jnp.eye(per_tile, dtype=F32)

    def bd(w):
        w4 = w.reshape(N_BLK_C // per_tile, per_tile, BW_C, BW_C)
        return (w4[:, :, :, None, :] * eye[None, :, None, :, None]).reshape(-1, MXU_DIM, MXU_DIM)

    full = lambda shape: pl.BlockSpec(shape, lambda bi, ti: (0,) * len(shape))
    finals, xo, h2, hf = [], None, None, None
    for d in range(2):
        tile = (lambda ti: ti) if d == 0 else (lambda ti: nt - 1 - ti)
        x_spec = lambda f: pl.BlockSpec((tt, D_RNN), f)
        row = lambda bi, ti, tile=tile: (bi * nt + tile(ti), 0)
        halo = lambda f: pl.BlockSpec((8, D_RNN), f)
        in_specs = [halo(lambda bi, ti, tile=tile: (jnp.maximum((bi * nt + tile(ti)) * (tt // 8) - 1, 0), 1)),
                    x_spec(lambda bi, ti, tile=tile: (bi * nt + tile(ti), 1)),
                    halo(lambda bi, ti, tile=tile: (jnp.minimum((bi * nt + tile(ti) + 1) * (tt // 8), b * t // 8 - 1), 1)),
                    full((4, D_RNN)), full((1, D_RNN)), full((4, MXU_DIM, 2 * MXU_DIM)),
                    full((2, D_RNN)), full((1, D_RNN)),
                    pl.BlockSpec((None, 1, D_RNN), lambda bi, ti: (bi, 0, 0))]
        args = [gx, gx, gx, conv_w, conv_b.reshape(1, D_RNN),
                jnp.concatenate([bd(w_rg[d]), bd(w_ig[d])], axis=-1), jnp.stack([b_rg[d], b_ig[d]]),
                lam[d].reshape(1, D_RNN), state[:, d].reshape(b, 1, D_RNN)]
        scratch = [pltpu.VMEM((1, D_RNN), F32), pltpu.VMEM((tt, D_RNN), F32), pltpu.VMEM((tt, D_RNN), F32)]
        last_spec = pl.BlockSpec((None, 1, D_RNN), lambda bi, ti: (bi, 0, 0))
        last_shape = jax.ShapeDtypeStruct((b, 1, D_RNN), F32)
        if d == 0:
            hf, last_f = pl.pallas_call(
                functools.partial(_lru_fwd_kernel, tt=tt),
                grid=(b, nt), in_specs=in_specs,
                out_specs=[pl.BlockSpec((tt, D_RNN), row), last_spec],
                out_shape=[jax.ShapeDtypeStruct((b * t, D_RNN), F32), last_shape],
                scratch_shapes=scratch,
                compiler_params=_cparams(2, 48),
                name="lru_fwd",
            )(*args)
        else:
            in_specs += [pl.BlockSpec((tt, D_RNN), row), pl.BlockSpec((tt, D_RNN), row), pl.BlockSpec((tt, D), row),
                         pl.BlockSpec((None, 1, 6 * D), lambda bi, ti: (mod_base + bi * per_batch_mod, 0, 0)),
                         full((1, D)), full((D_RNN, D))]
            args += [gx, hf, x, mod3, g2.reshape(1, D), w_out]
            xo, h2, last_b = pl.pallas_call(
                functools.partial(_lru_bwd_kernel, tt=tt),
                grid=(b, nt), in_specs=in_specs,
                out_specs=[pl.BlockSpec((tt, D), row), pl.BlockSpec((tt, D), row), last_spec],
                out_shape=[jax.ShapeDtypeStruct((b * t, D), F32), jax.ShapeDtypeStruct((b * t, D), BF16), last_shape],
                scratch_shapes=scratch + [pltpu.VMEM((tt, D_RNN), F32), pltpu.VMEM((D_RNN, D), BF16)],
                compiler_params=_cparams(2, 48),
                name="lru_bwd_proj",
            )(*args)
    return xo, h2, jnp.concatenate([last_f, last_b], axis=1)


def _select_kernel(h_ref, wr_ref, slot_ref, aff_ref, tri_ref, *, bs, t, cap):
    @pl.when(pl.program_id(0) == 0)
    def _():
        r = lax.broadcasted_iota(I32, (t, t), 0)
        c = lax.broadcasted_iota(I32, (t, t), 1)
        tri_ref[...] = jnp.where(r < c, 1.0, 0.0).astype(BF16)

    wr = wr_ref[...].astype(BF16)
    affs = []
    for s in range(bs):
        logits = _dot_nt(wr, h_ref[s])
        ex = jnp.exp(logits - jnp.max(logits, axis=0, keepdims=True))
        affs.append(ex / jnp.sum(ex, axis=0, keepdims=True))
    aff = jnp.concatenate(affs, axis=0) if bs > 1 else affs[0]
    bits = pltpu.bitcast(aff, I32)
    count = lambda mask: jnp.sum(jnp.where(mask, 1.0, 0.0), axis=-1, keepdims=True)
    th = jnp.zeros((bs * N_EXPERTS, 1), I32)
    for bit in range(30, -1, -1):
        cand = th | (1 << bit)
        th = jnp.where(count(bits >= cand) >= cap, cand, th)
    gt, eq = bits > th, bits == th
    need = cap - count(gt)
    lane = lax.broadcasted_iota(I32, (1, t), 1)
    lim = jnp.zeros((bs * N_EXPERTS, 1), I32)
    for bit in range(t.bit_length() - 1, -1, -1):
        cand = lim | (1 << bit)
        ok = (cand <= t) & (count(eq & (lane < cand)) <= need)
        lim = jnp.where(ok, cand, lim)
    sel = gt | (eq & (lane < lim))
    pos = _dot(jnp.where(sel, 1.0, 0.0).astype(BF16), tri_ref[...])
    slot = jnp.where(sel, pos.astype(I32), -1)
    for s in range(bs):
        slot_ref[s] = slot[s * N_EXPERTS:(s + 1) * N_EXPERTS, :]
        aff_ref[s] = aff[s * N_EXPERTS:(s + 1) * N_EXPERTS, :]


def _moe_select(h2, w_router, b, t, cap, bs):
    return pl.pallas_call(
        functools.partial(_select_kernel, bs=bs, t=t, cap=cap),
        grid=(b // bs,),
        in_specs=[pl.BlockSpec((bs, t, D), lambda i: (i, 0, 0)),
                  pl.BlockSpec((N_EXPERTS, D), lambda i: (0, 0))],
        out_specs=[pl.BlockSpec((bs, N_EXPERTS, t), lambda i: (i, 0, 0))] * 2,
        out_shape=[jax.ShapeDtypeStruct((b, N_EXPERTS, t), I32), jax.ShapeDtypeStruct((b, N_EXPERTS, t), F32)],
        scratch_shapes=[pltpu.VMEM((t, t), BF16)],
        compiler_params=_cparams(1, 48),
        name="moe_select",
    )(h2, w_router.T)


def _gather_kernel(slot_ref, aff_ref, h_ref, xs_ref, gc_ref, *, bq, cap, eg):
    j = lax.broadcasted_iota(I32, (cap, 1), 0)
    for s in range(bq):
        h = h_ref[s]
        for e0 in range(0, N_EXPERTS, eg):
            hots = [slot_ref[s, e:e + 1, :] == j for e in range(e0, e0 + eg)]
            p = jnp.concatenate([jnp.where(o, 1.0, 0.0).astype(BF16) for o in hots], axis=0)
            xs = _dot(p, h).astype(BF16)
            for k, e in enumerate(range(e0, e0 + eg)):
                xs_ref[e, s * cap:(s + 1) * cap, :] = xs[k * cap:(k + 1) * cap, :]
                gc_ref[e, s * cap:(s + 1) * cap, :] = jnp.sum(jnp.where(hots[k], aff_ref[s, e:e + 1, :], 0.0),
                                                              axis=-1, keepdims=True)


def _moe_gather(slot, aff, h3, b, t, cap, bq, eg):
    sa_spec = pl.BlockSpec((bq, N_EXPERTS, t), lambda i: (i, 0, 0))
    return pl.pallas_call(
        functools.partial(_gather_kernel, bq=bq, cap=cap, eg=eg),
        grid=(b // bq,),
        in_specs=[sa_spec, sa_spec, pl.BlockSpec((bq, t, D), lambda i: (i, 0, 0))],
        out_specs=[pl.BlockSpec((N_EXPERTS, bq * cap, D), lambda i: (0, i, 0)),
                   pl.BlockSpec((N_EXPERTS, bq * cap, 1), lambda i: (0, i, 0))],
        out_shape=[jax.ShapeDtypeStruct((N_EXPERTS, b * cap, D), BF16),
                   jax.ShapeDtypeStruct((N_EXPERTS, b * cap, 1), F32)],
        compiler_params=_cparams(1, 56),
        name="moe_gather",
    )(slot, aff, h3)


def _ffn_kernel(xs_ref, gc_ref, wg_ref, wu_ref, wd_ref, ys_ref, wgb_ref, wub_ref, wdb_ref):
    @pl.when(pl.program_id(1) == 0)
    def _():
        wgb_ref[...] = wg_ref[...].astype(BF16)
        wub_ref[...] = wu_ref[...].astype(BF16)
        wdb_ref[...] = wd_ref[...].astype(BF16)

    xs = xs_ref[...]
    zg = _dot(xs, wgb_ref[...])
    hid = (zg * _sigmoid(zg) * _dot(xs, wub_ref[...])).astype(BF16)
    ys_ref[...] = (_dot(hid, wdb_ref[...]) * gc_ref[...]).astype(BF16)


def _moe_ffn(xs, gc, w_gate, w_up, w_down, layer, tmf=512):
    r, f = xs.shape[1], w_gate.shape[-1]
    w_spec = lambda d0, d1: pl.BlockSpec((None, None, d0, d1), lambda e, i: (layer, e, 0, 0))
    return pl.pallas_call(
        _ffn_kernel,
        grid=(N_EXPERTS, r // tmf),
        in_specs=[pl.BlockSpec((None, tmf, D), lambda e, i: (e, i, 0)),
                  pl.BlockSpec((None, tmf, 1), lambda e, i: (e, i, 0)),
                  w_spec(D, f), w_spec(D, f), w_spec(f, D)],
        out_specs=pl.BlockSpec((None, tmf, D), lambda e, i: (e, i, 0)),
        out_shape=jax.ShapeDtypeStruct((N_EXPERTS, r, D), BF16),
        scratch_shapes=[pltpu.VMEM((D, f), BF16), pltpu.VMEM((D, f), BF16), pltpu.VMEM((f, D), BF16)],
        compiler_params=_cparams(2, 56),
        name="moe_ffn",
    )(xs, gc, w_gate, w_up, w_down)


def _combine_kernel(slot_ref, ys_ref, x_ref, mod_ref, o_ref, *, cap):
    n = N_EXPERTS * cap
    e_of = lax.broadcasted_iota(I32, (N_EXPERTS, n), 1) >> (cap.bit_length() - 1)
    rep = jnp.where(e_of == lax.broadcasted_iota(I32, (N_EXPERTS, n), 0), 1.0, 0.0).astype(BF16)
    slot_rep = _dot(slot_ref[...].astype(F32).astype(BF16), rep)
    jn = (lax.broadcasted_iota(I32, (1, n), 1) & (cap - 1)).astype(F32)
    onehot = jnp.where(slot_rep == jn, 1.0, 0.0).astype(BF16)
    y = _dot(onehot, ys_ref[...].reshape(n, D))
    o_ref[...] = x_ref[...] + mod_ref[:, 5 * D:6 * D] * y


def _moe_combine(slot_t, ys, x, mod3, mod_base, per_batch_mod, b, t, cap, tt=256):
    nt = t // tt
    return pl.pallas_call(
        functools.partial(_combine_kernel, cap=cap),
        grid=(b, nt),
        in_specs=[pl.BlockSpec((None, tt, N_EXPERTS), lambda bi, ti: (bi, ti, 0)),
                  pl.BlockSpec((N_EXPERTS, None, cap, D), lambda bi, ti: (0, bi, 0, 0)),
                  pl.BlockSpec((tt, D), lambda bi, ti: (bi * nt + ti, 0)),
                  pl.BlockSpec((None, 1, 6 * D), lambda bi, ti: (mod_base + bi * per_batch_mod, 0, 0))],
        out_specs=pl.BlockSpec((tt, D), lambda bi, ti: (bi * nt + ti, 0)),
        out_shape=jax.ShapeDtypeStruct((b * t, D), F32),
        compiler_params=_cparams(2, 48),
        name="moe_combine",
    )(slot_t, ys, x, mod3)


def _ec_moe(x, h2, mod3, mod_base, per_batch_mod, moe_w, layer, b, t, bs, bg):
    w_router, w_gate, w_up, w_down = moe_w
    cap = EC_FACTOR * t // N_EXPERTS
    h3 = h2.reshape(b, t, D)
    slot, aff = _moe_select(h3, w_router[layer], b, t, cap, bs)
    xs, gc = _moe_gather(slot, aff, h3, b, t, cap, bq=bg, eg=4 if cap >= MXU_DIM else N_EXPERTS)
    ys = _moe_ffn(xs, gc, w_gate, w_up, w_down, layer)
    return _moe_combine(jnp.swapaxes(slot, 1, 2), ys.reshape(N_EXPERTS, b, cap, D), x,
                        mod3, mod_base, per_batch_mod, b, t, cap)


def kernel(x_prompt, x_sample, cache_a_k, cache_a_v, cache_b_k, cache_b_v, state_c_h, c, c_ctx, ada_w, ada_b, norm_mix_g, norm_ffn_g, a_w_in, a_q_norm_g, a_k_norm_g, a_lam_q, a_lam_k, a_subln_g, a_w_out, b_w_in, b_q_norm_g, b_k_norm_g, b_sink, b_w_out, c_w_in, c_conv_w, c_conv_b, c_w_rg, c_b_rg, c_w_ig, c_b_ig, c_lam, c_w_out, moe_w_router, moe_w_gate, moe_w_up, moe_w_down):
    bp, tp, _ = x_prompt.shape
    bs_, ts, _ = x_sample.shape
    past = cache_a_k.shape[2]
    cvec = jnp.concatenate([c, c_ctx[None, :], jnp.zeros((16 - bs_ - 1, D), F32)], axis=0)
    mods = _modulation(cvec, ada_w, ada_b)
    groups = {"p": (bp, tp, bs_, 0), "s": (bs_, ts, 0, 1)}
    xs = {"p": x_prompt.reshape(bp * tp, D), "s": x_sample.reshape(bs_ * ts, D)}
    ck_a = cache_a_k.reshape(bs_, -1, past, H_A * 2 * HD_A)
    cv_a = cache_a_v.reshape(bs_, -1, past, H_A * 2 * HD_A)
    ck_b = cache_b_k.reshape(bs_, -1, past, KV_B * HD_B)
    cv_b = cache_b_v.reshape(bs_, -1, past, KV_B * HD_B)
    moe_w = (moe_w_router, moe_w_gate, moe_w_up, moe_w_down)
    new_a_k, new_a_v, new_b_k, new_b_v, new_c_h = [], [], [], [], []
    ia = ib = ic = 0
    for l in range(DEPTH):
        mod3 = mods[l].reshape(16, 1, 6 * D)
        kind = l % 3
        for key in ("p", "s"):
            b, t, mbase, per_b = groups[key]
            rows_per_mod = t if per_b else b * t
            x = xs[key]
            sample = key == "s"
            if kind == 0:
                lam_init = 0.8 - 0.6 * math.exp(-0.3 * l)
                qkv = _norm_mod_matmul(x, mod3, mbase, rows_per_mod, norm_mix_g[l], a_w_in[ia])
                outs = _qk_prep(qkv, a_q_norm_g[ia], a_k_norm_g[ia], D, D, D, t, rope=sample, k_f32=not sample)
                q, k, v = outs[:3]
                if not sample:
                    new_a_k.append(outs[3].reshape(b, t, H_A, 2 * HD_A))
                    new_a_v.append(qkv[:, 2 * D:].reshape(b, t, H_A, 2 * HD_A))
                o = _diff_attention(q, k, v, (ck_a, cv_a, ia) if sample else None, a_lam_q[ia], a_lam_k[ia],
                                    a_subln_g[ia], lam_init, b, t, heads=1 if sample else H_A,
                                    tq=512 if sample else t, nsub=2 if sample else 1)
                x, h2 = _proj_residual(o, x, mod3, mbase, rows_per_mod, norm_ffn_g[l], a_w_out[ia])
            elif kind == 1:
                nq, nk = H_B * HD_B, KV_B * HD_B
                qkv = _norm_mod_matmul(x, mod3, mbase, rows_per_mod, norm_mix_g[l], b_w_in[ib])
                outs = _qk_prep(qkv, b_q_norm_g[ib], b_k_norm_g[ib], nq, nk, nk, t, rope=sample, k_f32=not sample)
                q, k, v = outs[:3]
                if not sample:
                    new_b_k.append(outs[3].reshape(b, t, KV_B, HD_B))
                    new_b_v.append(qkv[:, nq + nk:].reshape(b, t, KV_B, HD_B))
                o = _gqa_attention(q, k, v, (ck_b, cv_b, ib) if sample else None, b_sink[ib], b, t,
                                   windowed=sample, tq=256)
                x, h2 = _proj_residual(o, x, mod3, mbase, rows_per_mod, norm_ffn_g[l], b_w_out[ib])
            else:
                gx = _norm_mod_matmul(x, mod3, mbase, rows_per_mod, norm_mix_g[l], c_w_in[ic])
                state = state_c_h[:, ic] if sample else jnp.zeros((b, 2, D_RNN), F32)
                x, h2, finals = _lru_mixer(gx, x, state, c_conv_w[ic], c_conv_b[ic], c_w_rg[ic], c_b_rg[ic],
                                           c_w_ig[ic], c_b_ig[ic], c_lam[ic], c_w_out[ic], mod3, mbase, per_b,
                                           norm_ffn_g[l], b, t)
                if not sample:
                    new_c_h.append(finals)
            xs[key] = _ec_moe(x, h2, mod3, mbase, per_b, moe_w, l, b, t,
                              bs=2 if sample else 8, bg=1 if sample else 8)
        ia, ib, ic = ia + (kind == 0), ib + (kind == 1), ic + (kind == 2)
    return (xs["p"].reshape(bp, tp, D), xs["s"].reshape(bs_, ts, D),
            jnp.stack(new_a_k, axis=1), jnp.stack(new_a_v, axis=1),
            jnp.stack(new_b_k, axis=1), jnp.stack(new_b_v, axis=1), jnp.stack(new_c_h, axis=1))
```

```python
import functools
import math

import jax
import jax.numpy as jnp
from jax import lax
from jax.experimental import pallas as pl
from jax.experimental.pallas import tpu as pltpu

F32, BF16, I32 = jnp.float32, jnp.bfloat16, jnp.int32

D = 1024
DEPTH = 4
GRID_W = 64
H_A, HD_A = 8, 64
H_B, KV_B, G_B, HD_B = 16, 4, 4, 64
WINDOW = 128
D_RNN = 1024
N_BLK_C, BW_C = 16, 64
LRU_C = 8.0
N_EXPERTS = 16
EC_FACTOR = 2
ROPE_THETA = 10000.0
EPS = 1e-6
LOG2E = math.log2(math.e)
LANES = 128
MXU_DIM = 256
MIB = 1024 * 1024


def _cparams(n_axes, vmem_mib=48):
    return pltpu.CompilerParams(dimension_semantics=("arbitrary",) * n_axes,
                                vmem_limit_bytes=vmem_mib * MIB)


def _sigmoid(x):
    return 1.0 / (1.0 + jnp.exp(-x))


def _rms(x):
    return x * lax.rsqrt(jnp.mean(x * x, axis=-1, keepdims=True) + EPS)


def _dot(a, b):
    return jnp.dot(a, b, preferred_element_type=F32)


def _dot_nt(a, b):
    return lax.dot_general(a, b, (((1,), (1,)), ((), ())), preferred_element_type=F32)


def _mod_kernel(c_ref, w_ref, b_ref, o_ref):
    c = c_ref[...]
    s = (c * _sigmoid(c)).astype(BF16)
    o_ref[...] = _dot(s, w_ref[...].astype(BF16)) + b_ref[...]


def _modulation(cvec, ada_w, ada_b):
    nt = 1536
    return pl.pallas_call(
        _mod_kernel,
        grid=(DEPTH, 6 * D // nt),
        in_specs=[pl.BlockSpec((16, D), lambda l, j: (0, 0)),
                  pl.BlockSpec((None, D, nt), lambda l, j: (l, 0, j)),
                  pl.BlockSpec((None, 1, nt), lambda l, j: (l, 0, j))],
        out_specs=pl.BlockSpec((None, 16, nt), lambda l, j: (l, 0, j)),
        out_shape=jax.ShapeDtypeStruct((DEPTH, 16, 6 * D), F32),
        compiler_params=_cparams(2, 32),
        name="adaln_mod",
    )(cvec, ada_w, ada_b.reshape(DEPTH, 1, 6 * D))


def _nmm_kernel(x_ref, mod_ref, g_ref, w_ref, o_ref, wbf_ref):
    @pl.when(pl.program_id(0) == 0)
    def _():
        wbf_ref[...] = w_ref[...].astype(BF16)

    m = mod_ref[...]
    h = (_rms(x_ref[...]) * g_ref[...] * (1.0 + m[:, D:2 * D]) + m[:, 0:D]).astype(BF16)
    o_ref[...] = _dot(h, wbf_ref[...])


def _norm_mod_matmul(x, mod3, mod_base, rows_per_mod, g, w, tm=512):
    r, n = x.shape[0], w.shape[1]
    return pl.pallas_call(
        _nmm_kernel,
        grid=(r // tm,),
        in_specs=[pl.BlockSpec((tm, D), lambda i: (i, 0)),
                  pl.BlockSpec((None, 1, 6 * D), lambda i: (mod_base + (i * tm) // rows_per_mod, 0, 0)),
                  pl.BlockSpec((1, D), lambda i: (0, 0)),
                  pl.BlockSpec((D, n), lambda i: (0, 0), pipeline_mode=pl.Buffered(1))],
        out_specs=pl.BlockSpec((tm, n), lambda i: (i, 0)),
        out_shape=jax.ShapeDtypeStruct((r, n), F32),
        scratch_shapes=[pltpu.VMEM((D, n), BF16)],
        compiler_params=_cparams(1, 48),
        name="norm_mod_matmul",
    )(x, mod3, g.reshape(1, D), w)


def _proj_tail(y, x_ref, mod_ref, g2_ref, xo_ref, h2_ref):
    m = mod_ref[...]
    xn = x_ref[...] + m[:, 2 * D:3 * D] * y
    xo_ref[...] = xn
    h2_ref[...] = (_rms(xn) * g2_ref[...] * (1.0 + m[:, 4 * D:5 * D]) + m[:, 3 * D:4 * D]).astype(BF16)


def _proj_kernel(o_ref, x_ref, mod_ref, g2_ref, w_ref, xo_ref, h2_ref, wbf_ref):
    @pl.when(pl.program_id(0) == 0)
    def _():
        wbf_ref[...] = w_ref[...].astype(BF16)

    _proj_tail(_dot(o_ref[...], wbf_ref[...]), x_ref, mod_ref, g2_ref, xo_ref, h2_ref)


def _proj_residual(o, x, mod3, mod_base, rows_per_mod, g2, w, tm=512):
    r = x.shape[0]
    return pl.pallas_call(
        _proj_kernel,
        grid=(r // tm,),
        in_specs=[pl.BlockSpec((tm, D), lambda i: (i, 0)),
                  pl.BlockSpec((tm, D), lambda i: (i, 0)),
                  pl.BlockSpec((None, 1, 6 * D), lambda i: (mod_base + (i * tm) // rows_per_mod, 0, 0)),
                  pl.BlockSpec((1, D), lambda i: (0, 0)),
                  pl.BlockSpec((D, D), lambda i: (0, 0))],
        out_specs=[pl.BlockSpec((tm, D), lambda i: (i, 0)), pl.BlockSpec((tm, D), lambda i: (i, 0))],
        out_shape=[jax.ShapeDtypeStruct((r, D), F32), jax.ShapeDtypeStruct((r, D), BF16)],
        scratch_shapes=[pltpu.VMEM((D, D), BF16)],
        compiler_params=_cparams(1, 40),
        name="proj_residual",
    )(o, x, mod3, g2.reshape(1, D), w)


def _group_inv_rms(x, on_mxu):
    ss = x * x
    if not on_mxu:
        low = lax.broadcasted_iota(I32, (1, LANES), 1) < HD_A
        s_lo = jnp.sum(jnp.where(low, ss, 0.0), axis=-1, keepdims=True)
        s_hi = jnp.sum(jnp.where(low, 0.0, ss), axis=-1, keepdims=True)
        return jnp.where(low, lax.rsqrt(s_lo * (1.0 / HD_A) + EPS), lax.rsqrt(s_hi * (1.0 / HD_A) + EPS))
    hi = ss.astype(BF16)
    lo = (ss - hi.astype(F32)).astype(BF16)
    r = lax.broadcasted_iota(I32, (LANES, LANES), 0) >> 6
    c = lax.broadcasted_iota(I32, (LANES, LANES), 1) >> 6
    ones_bd = jnp.where(r == c, 1.0, 0.0).astype(BF16)
    return lax.rsqrt((_dot(hi, ones_bd) + _dot(lo, ones_bd)) * (1.0 / HD_A) + EPS)


def _rope128(y, cos, sa, sb):
    return y * cos + pltpu.roll(y, LANES - 16, 1) * sa + pltpu.roll(y, 16, 1) * sb


def _qkv_kernel(*refs, nq, nk, rope, kv_f32):
    it = iter(refs)
    x_ref, mod_ref, g_ref, w_ref, qg_ref, kg_ref = (next(it) for _ in range(6))
    if rope:
        cos_ref, sa_ref, sb_ref = next(it), next(it), next(it)
    q_ref, k_ref, v_ref = next(it), next(it), next(it)
    kf_ref, vf_ref = (next(it), next(it)) if kv_f32 else (None, None)
    wbf_ref, qkv_ref = next(it), next(it)

    @pl.when(pl.program_id(0) == 0)
    def _():
        wbf_ref[...] = w_ref[...].astype(BF16)

    m = mod_ref[...]
    h = (_rms(x_ref[...]) * g_ref[...] * (1.0 + m[:, D:2 * D]) + m[:, 0:D]).astype(BF16)
    qkv_ref[...] = _dot(h, wbf_ref[...])
    if rope:
        cos, sa, sb = cos_ref[...], sa_ref[...], sb_ref[...]
    for j in range((nq + nk) // LANES):
        x = qkv_ref[:, j * LANES:(j + 1) * LANES]
        is_q = j < nq // LANES
        y = x * _group_inv_rms(x, on_mxu=rope) * (qg_ref[...] if is_q else kg_ref[...])
        if rope:
            y = _rope128(y, cos, sa, sb)
        if is_q:
            q_ref[:, j * LANES:(j + 1) * LANES] = (y * (HD_A ** -0.5 * LOG2E)).astype(BF16)
        else:
            jj = j - nq // LANES
            k_ref[:, jj * LANES:(jj + 1) * LANES] = y.astype(BF16)
            if kv_f32:
                kf_ref[:, jj * LANES:(jj + 1) * LANES] = y
    v = qkv_ref[:, nq + nk:]
    v_ref[...] = v.astype(BF16)
    if kv_f32:
        vf_ref[...] = v


def _rope_tables(t):
    n_freq = HD_A // 4
    inv_freq = ROPE_THETA ** (-jnp.arange(n_freq, dtype=F32) / n_freq)
    pos_row = jnp.repeat(jnp.arange(t // GRID_W), GRID_W).astype(F32)
    pos_col = jnp.tile(jnp.arange(GRID_W), t // GRID_W).astype(F32)
    ang_r = pos_row[:, None] * inv_freq[None, :]
    ang_c = pos_col[:, None] * inv_freq[None, :]
    z = jnp.zeros_like(ang_r)
    cos64 = jnp.concatenate([jnp.cos(ang_r), jnp.cos(ang_r), jnp.cos(ang_c), jnp.cos(ang_c)], axis=-1)
    sa64 = jnp.concatenate([-jnp.sin(ang_r), z, -jnp.sin(ang_c), z], axis=-1)
    sb64 = jnp.concatenate([z, jnp.sin(ang_r), z, jnp.sin(ang_c)], axis=-1)
    return tuple(jnp.tile(a, (1, 2)) for a in (cos64, sa64, sb64))


def _qkv_project(x, mod3, mod_base, rows_per_mod, g, w, qg, kg, nq, nk, nv, t, rope, kv_f32, tm=256):
    r = x.shape[0]
    n = nq + nk + nv
    tile2 = lambda a: jnp.tile(a.reshape(1, HD_A), (1, 2))
    row = lambda width: pl.BlockSpec((tm, width), lambda i: (i, 0))
    args = [x, mod3, g.reshape(1, D), w, tile2(qg), tile2(kg)]
    in_specs = [row(D),
                pl.BlockSpec((None, 1, 6 * D), lambda i: (mod_base + (i * tm) // rows_per_mod, 0, 0)),
                pl.BlockSpec((1, D), lambda i: (0, 0)),
                pl.BlockSpec((D, n), lambda i: (0, 0), pipeline_mode=pl.Buffered(1)),
                pl.BlockSpec((1, LANES), lambda i: (0, 0)),
                pl.BlockSpec((1, LANES), lambda i: (0, 0))]
    if rope:
        args += list(_rope_tables(t))
        in_specs += [pl.BlockSpec((tm, LANES), lambda i: (i % (t // tm), 0))] * 3
    out_shape = [jax.ShapeDtypeStruct((r, nq), BF16), jax.ShapeDtypeStruct((r, nk), BF16),
                 jax.ShapeDtypeStruct((r, nv), BF16)]
    out_specs = [row(nq), row(nk), row(nv)]
    if kv_f32:
        out_shape += [jax.ShapeDtypeStruct((r, nk), F32), jax.ShapeDtypeStruct((r, nv), F32)]
        out_specs += [row(nk), row(nv)]
    return pl.pallas_call(
        functools.partial(_qkv_kernel, nq=nq, nk=nk, rope=rope, kv_f32=kv_f32),
        grid=(r // tm,),
        in_specs=in_specs, out_specs=out_specs, out_shape=out_shape,
        scratch_shapes=[pltpu.VMEM((D, n), BF16), pltpu.VMEM((tm, n), F32)],
        compiler_params=_cparams(1, 48),
        name="qkv_project",
    )(*args)


def _diff_attn_kernel(*refs, heads, ctx, lam_init, nsub):
    it = iter(refs)
    q_ref, k_ref, v_ref = next(it), next(it), next(it)
    if ctx:
        ck_ref, cv_ref = next(it), next(it)
    lq_ref, lk_ref, sg_ref, o_ref = next(it), next(it), next(it), next(it)
    e = jnp.exp(jnp.sum(lq_ref[...] * lk_ref[...], axis=-1, keepdims=True))
    lam = e[0:1, :] - e[1:2, :] + lam_init
    lane = lax.broadcasted_iota(I32, (1, LANES), 1)
    tqs = q_ref.shape[0] // nsub
    units = [(slice(h * LANES, (h + 1) * LANES), slice(j * tqs, (j + 1) * tqs))
             for h in range(heads) for j in range(nsub)]

    def scores(u):
        sl, rows = units[u]
        q = q_ref[rows, sl]
        out = []
        for c in range(2):
            qc = jnp.where((lane < HD_A) if c == 0 else (lane >= HD_A), q, jnp.zeros_like(q))
            out.append((_dot_nt(qc, k_ref[:, sl]), _dot_nt(qc, ck_ref[:, sl].astype(BF16)) if ctx else None))
        return out

    def weights(sc2):
        ps, ls = [], []
        for s, sc in sc2:
            m = jnp.max(s, axis=-1, keepdims=True)
            if ctx:
                m = jnp.maximum(m, jnp.max(sc, axis=-1, keepdims=True))
            p = jnp.exp2(s - m)
            l = jnp.sum(p, axis=-1, keepdims=True)
            pc = None
            if ctx:
                pc = jnp.exp2(sc - m)
                l = l + jnp.sum(pc, axis=-1, keepdims=True)
            ps.append((p, pc))
            ls.append(l)
        ratio = lam * ls[0] / ls[1]
        w_lat = (ps[0][0] - ratio * ps[1][0]).astype(BF16)
        w_ctx = (ps[0][1] - ratio * ps[1][1]).astype(BF16) if ctx else None
        return w_lat, w_ctx, 1.0 / ls[0]

    def values(u, w):
        sl, rows = units[u]
        w_lat, w_ctx, inv_l0 = w
        o = _dot(w_lat, v_ref[:, sl])
        if ctx:
            o = o + _dot(w_ctx, cv_ref[:, sl].astype(BF16))
        o_ref[rows, sl] = (_rms(o * inv_l0) * sg_ref[...] * (1.0 - lam_init)).astype(BF16)

    n = len(units)
    sc = {u: scores(u) for u in range(min(2, n))}
    for u in range(n):
        w = weights(sc.pop(u))
        if u + 2 < n:
            sc[u + 2] = scores(u + 2)
        values(u, w)


def _diff_attention(q, k, v, cache, lam_q, lam_k, subln_g, lam_init, b, t, heads, tq, nsub):
    nq = t // tq
    hb = H_A // heads
    args = [q, k, v]
    in_specs = [pl.BlockSpec((tq, heads * LANES), lambda bi, hi, qi: (bi * nq + qi, hi)),
                pl.BlockSpec((t, heads * LANES), lambda bi, hi, qi: (bi, hi)),
                pl.BlockSpec((t, heads * LANES), lambda bi, hi, qi: (bi, hi))]
    if cache is not None:
        ck, cv, la = cache
        p = ck.shape[2]
        args += [ck, cv]
        in_specs += [pl.BlockSpec((None, None, p, heads * LANES), lambda bi, hi, qi: (bi, la, 0, hi))] * 2
    args += [lam_q, lam_k, subln_g.reshape(1, LANES)]
    in_specs += [pl.BlockSpec((2, HD_A), lambda bi, hi, qi: (0, 0)),
                 pl.BlockSpec((2, HD_A), lambda bi, hi, qi: (0, 0)),
                 pl.BlockSpec((1, LANES), lambda bi, hi, qi: (0, 0))]
    return pl.pallas_call(
        functools.partial(_diff_attn_kernel, heads=heads, ctx=cache is not None, lam_init=lam_init, nsub=nsub),
        grid=(b, hb, nq),
        in_specs=in_specs,
        out_specs=pl.BlockSpec((tq, heads * LANES), lambda bi, hi, qi: (bi * nq + qi, hi)),
        out_shape=jax.ShapeDtypeStruct((b * t, H_A * LANES), BF16),
        compiler_params=_cparams(3, 48),
        name="diff_attention",
    )(*args)


def _both_halves(x, half):
    lane_half = lax.broadcasted_iota(I32, (1, LANES), 1) >> 6
    xm = jnp.where(lane_half == half, x, 0.0)
    return xm + pltpu.roll(xm, HD_B, 1)


def _value_with_ones(x, half):
    lane = lax.broadcasted_iota(I32, (1, LANES), 1)
    return jnp.where(lane < HD_B, _both_halves(x, half), jnp.where(lane == HD_B, 1.0, 0.0))


def _gqa_kernel(*refs, kvs, windowed, ctx, t, tq):
    it = iter(refs)
    q_ref, k_ref, v_ref = next(it), next(it), next(it)
    if ctx:
        ck_ref, cv_ref = next(it), next(it)
    sink_ref, o_ref = next(it), next(it)
    lane_half = lax.broadcasted_iota(I32, (1, LANES), 1) >> 6
    if windowed:
        kd_ref, vd_ref, ckd_ref, cvd_ref = next(it), next(it), next(it), next(it)
        half = pl.program_id(1) % 2

        @pl.when(pl.program_id(2) == 0)
        def _():
            kd_ref[...] = _both_halves(k_ref[...].astype(F32), half).astype(BF16)
            vd_ref[...] = _value_with_ones(v_ref[...].astype(F32), half).astype(BF16)
            ckd_ref[...] = _both_halves(ck_ref[...], half).astype(BF16)
            cvd_ref[...] = _value_with_ones(cv_ref[...], half).astype(BF16)

        span = tq + 2 * WINDOW
        q0 = pl.program_id(2) * tq
        start = pl.multiple_of(jnp.clip(q0 - WINDOW, 0, t - span), LANES)
        qpos = q0 + lax.broadcasted_iota(I32, (tq, span), 0)
        kpos = start + lax.broadcasted_iota(I32, (tq, span), 1)
        valid = jnp.abs(qpos - kpos) <= WINDOW
    units = []
    for kv in range(kvs):
        if windowed:
            kd, vd = kd_ref[pl.ds(start, span), :], vd_ref[pl.ds(start, span), :]
            ckd, cvd = ckd_ref[...], cvd_ref[...]
        else:
            kcol = slice((kv // 2) * LANES, (kv // 2 + 1) * LANES)
            kd = _both_halves(k_ref[:, kcol].astype(F32), kv % 2).astype(BF16)
            vd = _value_with_ones(v_ref[:, kcol].astype(F32), kv % 2).astype(BF16)
            ckd = cvd = None
        for g in range(G_B):
            qcol = slice((kv * G_B + g) // 2 * LANES, ((kv * G_B + g) // 2 + 1) * LANES)
            q = q_ref[:, qcol]
            qm = jnp.where(lane_half == g % 2, q, jnp.zeros_like(q))
            sink = sink_ref[kv:kv + 1, g:g + 1] if kvs > 1 else sink_ref[:, g:g + 1]
            units.append((g, qcol, sink, _dot_nt(qm, kd), _dot_nt(qm, ckd) if ctx else None, vd, cvd))
    o_pair = None
    for g, qcol, sink, s, sc, vd, cvd in units:
        sink2 = sink * LOG2E
        if windowed:
            s = jnp.where(valid, s, -jnp.inf)
        m = jnp.maximum(jnp.max(s, axis=-1, keepdims=True), sink2)
        if ctx:
            m = jnp.maximum(m, jnp.max(sc, axis=-1, keepdims=True))
        ov = _dot(jnp.exp2(s - m).astype(BF16), vd)
        if ctx:
            ov = ov + _dot(jnp.exp2(sc - m).astype(BF16), cvd)
        l = ov[:, HD_B:HD_B + 1] + jnp.exp2(sink2 - m)
        o = ov * (1.0 / l)
        if g % 2 == 0:
            o_pair = jnp.where(lane_half == 0, o, 0.0)
        else:
            o_ref[:, qcol] = (o_pair + jnp.where(lane_half == 1, pltpu.roll(o, HD_B, 1), 0.0)).astype(BF16)


def _gqa_attention(q, k, v, cache, sink, b, t, windowed, tq):
    if windowed:
        kvs, nq = 1, t // tq
        grid = (b, KV_B, nq)
        qw = G_B * HD_B
        q_spec = pl.BlockSpec((tq, qw), lambda bi, kv, qi: (bi * nq + qi, kv))
        kv_spec = pl.BlockSpec((t, LANES), lambda bi, kv, qi: (bi, kv // 2))
        sink_arr = sink.reshape(KV_B, 1, G_B)
        sink_spec = pl.BlockSpec((None, 1, G_B), lambda bi, kv, qi: (kv, 0, 0))
        o_spec = pl.BlockSpec((tq, qw), lambda bi, kv, qi: (bi * nq + qi, kv))
    else:
        kvs, tq = KV_B, t
        grid = (b, 1, 1)
        q_spec = pl.BlockSpec((t, H_B * HD_B), lambda bi, kv, qi: (bi, 0))
        kv_spec = pl.BlockSpec((t, KV_B * HD_B), lambda bi, kv, qi: (bi, 0))
        sink_arr = sink.reshape(KV_B, G_B)
        sink_spec = pl.BlockSpec((KV_B, G_B), lambda bi, kv, qi: (0, 0))
        o_spec = pl.BlockSpec((t, H_B * HD_B), lambda bi, kv, qi: (bi, 0))
    args, in_specs = [q, k, v], [q_spec, kv_spec, kv_spec]
    if cache is not None:
        ck, cv, lb = cache
        p = ck.shape[2]
        args += [ck, cv]
        in_specs += [pl.BlockSpec((None, None, p, LANES), lambda bi, kv, qi: (bi, lb, 0, kv // 2))] * 2
    args.append(sink_arr)
    in_specs.append(sink_spec)
    scratch = []
    if windowed:
        scratch = [pltpu.VMEM((t, LANES), BF16)] * 2 + [pltpu.VMEM((p, LANES), BF16)] * 2
    return pl.pallas_call(
        functools.partial(_gqa_kernel, kvs=kvs, windowed=windowed, ctx=cache is not None, t=t, tq=tq),
        grid=grid, in_specs=in_specs, out_specs=o_spec,
        out_shape=jax.ShapeDtypeStruct((b * t, H_B * HD_B), BF16),
        scratch_shapes=scratch,
        compiler_params=_cparams(3, 48),
        name="gqa_attention",
    )(*args)


def _lru_conv_gates(xp_ref, xc_ref, xn_ref, cw_ref, cb_ref, wg_ref, bg_ref, lam_ref, a_ref, u_ref, ti, nt, tt):
    nl = -lam_ref[...]
    sp = jnp.maximum(nl, 0.0) + jnp.log1p(jnp.exp(-jnp.abs(nl)))
    cur = xc_ref[...]
    prev = xp_ref[...] * (ti > 0).astype(F32)
    nxt = xn_ref[...] * (ti < nt - 1).astype(F32)
    row = lax.broadcasted_iota(I32, (tt, 1), 0)
    xm1 = jnp.where(row == 0, prev[7:8, :], pltpu.roll(cur, 1, 0))
    xm2 = jnp.where(row == 0, prev[6:7, :], jnp.where(row == 1, prev[7:8, :], pltpu.roll(cur, 2, 0)))
    xp1 = jnp.where(row == tt - 1, nxt[0:1, :], pltpu.roll(cur, tt - 1, 0))
    cw = cw_ref[...]
    xc = xm2 * cw[0:1, :] + xm1 * cw[1:2, :] + cur * cw[2:3, :] + xp1 * cw[3:4, :] + cb_ref[...]
    xcb = xc.astype(BF16)
    for j in range(D_RNN // MXU_DIM):
        cs = slice(j * MXU_DIM, (j + 1) * MXU_DIM)
        z = _dot(xcb[:, cs], wg_ref[j].astype(BF16))
        r = _sigmoid(z[:, :MXU_DIM] + bg_ref[0:1, cs])
        i = _sigmoid(z[:, MXU_DIM:] + bg_ref[1:2, cs])
        a = jnp.exp(-LRU_C * r * sp[:, cs])
        a_ref[:, cs] = a
        u_ref[:, cs] = jnp.sqrt(1.0 - a * a) * i * xc[:, cs]


def _tile_scan(a_ref, u_ref, hs_ref, h, tt, reverse):
    row = lax.broadcasted_iota(I32, (8, 1), 0)
    ng = tt // 8
    for g in (range(ng - 1, -1, -1) if reverse else range(ng)):
        rows = slice(g * 8, (g + 1) * 8)
        a8, u8 = a_ref[rows, :], u_ref[rows, :]
        for s in (1, 2, 4):
            keep = (row < 8 - s) if reverse else (row >= s)
            shift = 8 - s if reverse else s
            u8 = a8 * jnp.where(keep, pltpu.roll(u8, shift, 0), 0.0) + u8
            a8 = a8 * jnp.where(keep, pltpu.roll(a8, shift, 0), 1.0)
        h8 = a8 * h + u8
        hs_ref[rows, :] = h8
        h = h8[0:1, :] if reverse else h8[7:8, :]
    return h


def _lru_fwd_kernel(xp_ref, xc_ref, xn_ref, cw_ref, cb_ref, wg_ref, bg_ref, lam_ref, h0_ref,
                    hf_ref, last_ref, carry_ref, a_ref, u_ref, *, tt):
    ti, nt = pl.program_id(1), pl.num_programs(1)

    @pl.when(ti == 0)
    def _():
        carry_ref[...] = h0_ref[...]

    _lru_conv_gates(xp_ref, xc_ref, xn_ref, cw_ref, cb_ref, wg_ref, bg_ref, lam_ref, a_ref, u_ref, ti, nt, tt)
    h = _tile_scan(a_ref, u_ref, hf_ref, carry_ref[...], tt, reverse=False)
    carry_ref[...] = h
    last_ref[...] = h


def _lru_bwd_kernel(xp_ref, xc_ref, xn_ref, cw_ref, cb_ref, wg_ref, bg_ref, lam_ref, h0_ref,
                    gate_ref, hf_ref, x_ref, mod_ref, g2_ref, w_ref,
                    xo_ref, h2_ref, last_ref, carry_ref, a_ref, u_ref, hb_ref, wbf_ref, *, tt):
    ti, nt = pl.program_id(1), pl.num_programs(1)

    @pl.when((pl.program_id(0) == 0) & (ti == 0))
    def _():
        wbf_ref[...] = w_ref[...].astype(BF16)

    @pl.when(ti == 0)
    def _():
        carry_ref[...] = h0_ref[...]

    tr = nt - 1 - ti
    _lru_conv_gates(xp_ref, xc_ref, xn_ref, cw_ref, cb_ref, wg_ref, bg_ref, lam_ref, a_ref, u_ref, tr, nt, tt)
    h = _tile_scan(a_ref, u_ref, hb_ref, carry_ref[...], tt, reverse=True)
    carry_ref[...] = h
    last_ref[...] = h
    g = gate_ref[...]
    gelu = 0.5 * g * (1.0 + jnp.tanh(math.sqrt(2.0 / math.pi) * (g + 0.044715 * (g * g * g))))
    y = (gelu * (hf_ref[...] + hb_ref[...])).astype(BF16)
    _proj_tail(_dot(y, wbf_ref[...]), x_ref, mod_ref, g2_ref, xo_ref, h2_ref)


def _lru_mixer(gx, x, state, conv_w, conv_b, w_rg, b_rg, w_ig, b_ig, lam, w_out, mod3, mod_base, per_batch_mod,
               g2, b, t, tt=256):
    nt = t // tt
    per_tile = MXU_DIM // BW_C
    eye = jnp.eye(per_tile, dtype=F32)

    def bd(w):
        w4 = w.reshape(N_BLK_C // per_tile, per_tile, BW_C, BW_C)
        return (w4[:, :, :, None, :] * eye[None, :, None, :, None]).reshape(-1, MXU_DIM, MXU_DIM)

    full = lambda shape: pl.BlockSpec(shape, lambda bi, ti: (0,) * len(shape))
    finals, xo, h2, hf = [], None, None, None
    for d in range(2):
        tile = (lambda ti: ti) if d == 0 else (lambda ti: nt - 1 - ti)
        x_spec = lambda f: pl.BlockSpec((tt, D_RNN), f)
        row = lambda bi, ti, tile=tile: (bi * nt + tile(ti), 0)
        halo = lambda f: pl.BlockSpec((8, D_RNN), f)
        in_specs = [halo(lambda bi, ti, tile=tile: (jnp.maximum((bi * nt + tile(ti)) * (tt // 8) - 1, 0), 1)),
                    x_spec(lambda bi, ti, tile=tile: (bi * nt + tile(ti), 1)),
                    halo(lambda bi, ti, tile=tile: (jnp.minimum((bi * nt + tile(ti) + 1) * (tt // 8), b * t // 8 - 1), 1)),
                    full((4, D_RNN)), full((1, D_RNN)), full((4, MXU_DIM, 2 * MXU_DIM)),
                    full((2, D_RNN)), full((1, D_RNN)),
                    pl.BlockSpec((None, 1, D_RNN), lambda bi, ti: (bi, 0, 0))]
        args = [gx, gx, gx, conv_w, conv_b.reshape(1, D_RNN),
                jnp.concatenate([bd(w_rg[d]), bd(w_ig[d])], axis=-1), jnp.stack([b_rg[d], b_ig[d]]),
                lam[d].reshape(1, D_RNN), state[:, d].reshape(b, 1, D_RNN)]
        scratch = [pltpu.VMEM((1, D_RNN), F32), pltpu.VMEM((tt, D_RNN), F32), pltpu.VMEM((tt, D_RNN), F32)]
        last_spec = pl.BlockSpec((None, 1, D_RNN), lambda bi, ti: (bi, 0, 0))
        last_shape = jax.ShapeDtypeStruct((b, 1, D_RNN), F32)
        if d == 0:
            hf, last_f = pl.pallas_call(
                functools.partial(_lru_fwd_kernel, tt=tt),
                grid=(b, nt), in_specs=in_specs,
                out_specs=[pl.BlockSpec((tt, D_RNN), row), last_spec],
                out_shape=[jax.ShapeDtypeStruct((b * t, D_RNN), F32), last_shape],
                scratch_shapes=scratch,
                compiler_params=_cparams(2, 48),
                name="lru_fwd",
            )(*args)
        else:
            in_specs += [pl.BlockSpec((tt, D_RNN), row), pl.BlockSpec((tt, D_RNN), row), pl.BlockSpec((tt, D), row),
                         pl.BlockSpec((None, 1, 6 * D), lambda bi, ti: (mod_base + bi * per_batch_mod, 0, 0)),
                         full((1, D)), full((D_RNN, D))]
            args += [gx, hf, x, mod3, g2.reshape(1, D), w_out]
            xo, h2, last_b = pl.pallas_call(
                functools.partial(_lru_bwd_kernel, tt=tt),
                grid=(b, nt), in_specs=in_specs,
                out_specs=[pl.BlockSpec((tt, D), row), pl.BlockSpec((tt, D), row), last_spec],
                out_shape=[jax.ShapeDtypeStruct((b * t, D), F32), jax.ShapeDtypeStruct((b * t, D), BF16), last_shape],
                scratch_shapes=scratch + [pltpu.VMEM((tt, D_RNN), F32), pltpu.VMEM((D_RNN, D), BF16)],
                compiler_params=_cparams(2, 48),
                name="lru_bwd_proj",
            )(*args)
    return xo, h2, jnp.concatenate([last_f, last_b], axis=1)


def _select_kernel(h_ref, wr_ref, slot_ref, aff_ref, tri_ref, *, bs, t, cap):
    @pl.when(pl.program_id(0) == 0)
    def _():
        r = lax.broadcasted_iota(I32, (t, t), 0)
        c = lax.broadcasted_iota(I32, (t, t), 1)
        tri_ref[...] = jnp.where(r < c, 1.0, 0.0).astype(BF16)

    wr = wr_ref[...].astype(BF16)
    affs = []
    for s in range(bs):
        logits = _dot_nt(wr, h_ref[s])
        ex = jnp.exp(logits - jnp.max(logits, axis=0, keepdims=True))
        affs.append(ex / jnp.sum(ex, axis=0, keepdims=True))
    aff = jnp.concatenate(affs, axis=0) if bs > 1 else affs[0]
    bits = pltpu.bitcast(aff, I32)
    count = lambda mask: jnp.sum(jnp.where(mask, 1.0, 0.0), axis=-1, keepdims=True)
    th = jnp.zeros((bs * N_EXPERTS, 1), I32)
    for bit in range(30, -1, -1):
        cand = th | (1 << bit)
        th = jnp.where(count(bits >= cand) >= cap, cand, th)
    gt, eq = bits > th, bits == th
    need = cap - count(gt)
    lane = lax.broadcasted_iota(I32, (1, t), 1)
    lim = jnp.zeros((bs * N_EXPERTS, 1), I32)
    for bit in range(t.bit_length() - 1, -1, -1):
        cand = lim | (1 << bit)
        ok = (cand <= t) & (count(eq & (lane < cand)) <= need)
        lim = jnp.where(ok, cand, lim)
    sel = gt | (eq & (lane < lim))
    pos = _dot(jnp.where(sel, 1.0, 0.0).astype(BF16), tri_ref[...])
    slot = jnp.where(sel, pos.astype(I32), -1)
    for s in range(bs):
        slot_ref[s] = slot[s * N_EXPERTS:(s + 1) * N_EXPERTS, :]
        aff_ref[s] = aff[s * N_EXPERTS:(s + 1) * N_EXPERTS, :]


def _moe_select(h2, w_router, b, t, cap, bs):
    return pl.pallas_call(
        functools.partial(_select_kernel, bs=bs, t=t, cap=cap),
        grid=(b // bs,),
        in_specs=[pl.BlockSpec((bs, t, D), lambda i: (i, 0, 0)),
                  pl.BlockSpec((N_EXPERTS, D), lambda i: (0, 0))],
        out_specs=[pl.BlockSpec((bs, N_EXPERTS, t), lambda i: (i, 0, 0))] * 2,
        out_shape=[jax.ShapeDtypeStruct((b, N_EXPERTS, t), I32), jax.ShapeDtypeStruct((b, N_EXPERTS, t), F32)],
        scratch_shapes=[pltpu.VMEM((t, t), BF16)],
        compiler_params=_cparams(1, 48),
        name="moe_select",
    )(h2, w_router.T)


def _gather_kernel(slot_ref, aff_ref, h_ref, xs_ref, gc_ref, *, bq, cap, eg):
    j = lax.broadcasted_iota(I32, (cap, 1), 0)
    for s in range(bq):
        h = h_ref[s]
        for e0 in range(0, N_EXPERTS, eg):
            hots = [slot_ref[s, e:e + 1, :] == j for e in range(e0, e0 + eg)]
            p = jnp.concatenate([jnp.where(o, 1.0, 0.0).astype(BF16) for o in hots], axis=0)
            xs = _dot(p, h).astype(BF16)
            for k, e in enumerate(range(e0, e0 + eg)):
                xs_ref[e, s * cap:(s + 1) * cap, :] = xs[k * cap:(k + 1) * cap, :]
                gc_ref[e, s * cap:(s + 1) * cap, :] = jnp.sum(jnp.where(hots[k], aff_ref[s, e:e + 1, :], 0.0),
                                                              axis=-1, keepdims=True)


def _moe_gather(slot, aff, h3, b, t, cap, bq, eg):
    sa_spec = pl.BlockSpec((bq, N_EXPERTS, t), lambda i: (i, 0, 0))
    return pl.pallas_call(
        functools.partial(_gather_kernel, bq=bq, cap=cap, eg=eg),
        grid=(b // bq,),
        in_specs=[sa_spec, sa_spec, pl.BlockSpec((bq, t, D), lambda i: (i, 0, 0))],
        out_specs=[pl.BlockSpec((N_EXPERTS, bq * cap, D), lambda i: (0, i, 0)),
                   pl.BlockSpec((N_EXPERTS, bq * cap, 1), lambda i: (0, i, 0))],
        out_shape=[jax.ShapeDtypeStruct((N_EXPERTS, b * cap, D), BF16),
                   jax.ShapeDtypeStruct((N_EXPERTS, b * cap, 1), F32)],
        compiler_params=_cparams(1, 56),
        name="moe_gather",
    )(slot, aff, h3)


def _ffn_kernel(xs_ref, gc_ref, wg_ref, wu_ref, wd_ref, ys_ref, wgb_ref, wub_ref, wdb_ref):
    @pl.when(pl.program_id(1) == 0)
    def _():
        wgb_ref[...] = wg_ref[...].astype(BF16)
        wub_ref[...] = wu_ref[...].astype(BF16)
        wdb_ref[...] = wd_ref[...].astype(BF16)

    xs = xs_ref[...]
    acc = None
    for c in range(0, wgb_ref.shape[1], MXU_DIM):
        cs = slice(c, c + MXU_DIM)
        zg = _dot(xs, wgb_ref[:, cs])
        hid = (zg * _sigmoid(zg) * _dot(xs, wub_ref[:, cs])).astype(BF16)
        part = _dot(hid, wdb_ref[cs, :])
        acc = part if acc is None else acc + part
    ys_ref[...] = (acc * gc_ref[...]).astype(BF16)


def _moe_ffn(xs, gc, w_gate, w_up, w_down, layer, tmf=1024):
    r, f = xs.shape[1], w_gate.shape[-1]
    w_spec = lambda d0, d1: pl.BlockSpec((None, None, d0, d1), lambda e, i: (layer, e, 0, 0))
    return pl.pallas_call(
        _ffn_kernel,
        grid=(N_EXPERTS, r // tmf),
        in_specs=[pl.BlockSpec((None, tmf, D), lambda e, i: (e, i, 0)),
                  pl.BlockSpec((None, tmf, 1), lambda e, i: (e, i, 0)),
                  w_spec(D, f), w_spec(D, f), w_spec(f, D)],
        out_specs=pl.BlockSpec((None, tmf, D), lambda e, i: (e, i, 0)),
        out_shape=jax.ShapeDtypeStruct((N_EXPERTS, r, D), BF16),
        scratch_shapes=[pltpu.VMEM((D, f), BF16), pltpu.VMEM((D, f), BF16), pltpu.VMEM((f, D), BF16)],
        compiler_params=_cparams(2, 56),
        name="moe_ffn",
    )(xs, gc, w_gate, w_up, w_down)


def _scatter_rows(slot, width, ys):
    n = N_EXPERTS * width
    e_of = lax.broadcasted_iota(I32, (N_EXPERTS, n), 1) >> (width.bit_length() - 1)
    rep = jnp.where(e_of == lax.broadcasted_iota(I32, (N_EXPERTS, n), 0), 1.0, 0.0).astype(BF16)
    slot_rep = _dot(slot.astype(F32).astype(BF16), rep)
    jn = (lax.broadcasted_iota(I32, (1, n), 1) & (width - 1)).astype(F32)
    return _dot(jnp.where(slot_rep == jn, 1.0, 0.0).astype(BF16), ys)


def _combine_kernel(slot_ref, ys_ref, x_ref, mod_ref, o_ref, *, cap):
    y = _scatter_rows(slot_ref[...], cap, ys_ref[...].reshape(N_EXPERTS * cap, D))
    o_ref[...] = x_ref[...] + mod_ref[:, 5 * D:6 * D] * y


def _combine_win_kernel(st_s, ok_s, slot_ref, stv_ref, ys_ref, x_ref, mod_ref, o_ref, ysw_ref, *, cap, win, nt):
    tile = pl.program_id(0) * nt + pl.program_id(1)
    ga2 = mod_ref[:, 5 * D:6 * D]

    @pl.when(ok_s[tile] != 0)
    def _():
        for e in range(N_EXPERTS):
            st = pl.multiple_of(st_s[tile * N_EXPERTS + e], 16)
            ysw_ref[e * win:(e + 1) * win, :] = ys_ref[e, pl.ds(st, win), :]
        slot = slot_ref[...]
        rel = jnp.where(slot >= 0, slot - stv_ref[...], -1)
        o_ref[...] = x_ref[...] + ga2 * _scatter_rows(rel, win, ysw_ref[...])

    @pl.when(ok_s[tile] == 0)
    def _():
        y = _scatter_rows(slot_ref[...], cap, ys_ref[...].reshape(N_EXPERTS * cap, D))
        o_ref[...] = x_ref[...] + ga2 * y


def _moe_combine_windowed(slot, ys, x, mod3, mod_base, per_batch_mod, b, t, cap, tt=512, win=128):
    nt = t // tt
    cnt = (slot >= 0).reshape(b, N_EXPERTS, nt, tt).sum(-1).astype(I32)
    start = jnp.cumsum(cnt, axis=-1) - cnt
    st = jnp.clip((start // 16) * 16, 0, cap - win)
    ok = jnp.all(start + cnt <= st + win, axis=1).astype(I32)
    st_t = jnp.swapaxes(st, 1, 2)
    grid_spec = pltpu.PrefetchScalarGridSpec(
        num_scalar_prefetch=2,
        grid=(b, nt),
        in_specs=[pl.BlockSpec((None, tt, N_EXPERTS), lambda bi, ti, *_: (bi, ti, 0)),
                  pl.BlockSpec((None, None, 1, N_EXPERTS), lambda bi, ti, *_: (bi, ti, 0, 0)),
                  pl.BlockSpec((N_EXPERTS, None, cap, D), lambda bi, ti, *_: (0, bi, 0, 0)),
                  pl.BlockSpec((tt, D), lambda bi, ti, *_: (bi * nt + ti, 0)),
                  pl.BlockSpec((None, 1, 6 * D), lambda bi, ti, *_: (mod_base + bi * per_batch_mod, 0, 0))],
        out_specs=pl.BlockSpec((tt, D), lambda bi, ti, *_: (bi * nt + ti, 0)),
        scratch_shapes=[pltpu.VMEM((N_EXPERTS * win, D), BF16)])
    return pl.pallas_call(
        functools.partial(_combine_win_kernel, cap=cap, win=win, nt=nt),
        grid_spec=grid_spec,
        out_shape=jax.ShapeDtypeStruct((b * t, D), F32),
        compiler_params=_cparams(2, 56),
        name="moe_combine_win",
    )(st_t.reshape(-1), ok.reshape(-1), jnp.swapaxes(slot, 1, 2), st_t.reshape(b, nt, 1, N_EXPERTS), ys, x, mod3)


def _moe_combine(slot_t, ys, x, mod3, mod_base, per_batch_mod, b, t, cap, tt=512):
    tt = min(tt, t)
    nt = t // tt
    return pl.pallas_call(
        functools.partial(_combine_kernel, cap=cap),
        grid=(b, nt),
        in_specs=[pl.BlockSpec((None, tt, N_EXPERTS), lambda bi, ti: (bi, ti, 0)),
                  pl.BlockSpec((N_EXPERTS, None, cap, D), lambda bi, ti: (0, bi, 0, 0)),
                  pl.BlockSpec((tt, D), lambda bi, ti: (bi * nt + ti, 0)),
                  pl.BlockSpec((None, 1, 6 * D), lambda bi, ti: (mod_base + bi * per_batch_mod, 0, 0))],
        out_specs=pl.BlockSpec((tt, D), lambda bi, ti: (bi * nt + ti, 0)),
        out_shape=jax.ShapeDtypeStruct((b * t, D), F32),
        compiler_params=_cparams(2, 48),
        name="moe_combine",
    )(slot_t, ys, x, mod3)


def _ec_moe(x, h2, mod3, mod_base, per_batch_mod, moe_w, layer, b, t, bs, bg):
    w_router, w_gate, w_up, w_down = moe_w
    cap = EC_FACTOR * t // N_EXPERTS
    h3 = h2.reshape(b, t, D)
    slot, aff = _moe_select(h3, w_router[layer], b, t, cap, bs)
    xs, gc = _moe_gather(slot, aff, h3, b, t, cap, bq=bg, eg=4 if cap >= MXU_DIM else N_EXPERTS)
    ys = _moe_ffn(xs, gc, w_gate, w_up, w_down, layer).reshape(N_EXPERTS, b, cap, D)
    if cap >= MXU_DIM:
        return _moe_combine_windowed(slot, ys, x, mod3, mod_base, per_batch_mod, b, t, cap)
    return _moe_combine(jnp.swapaxes(slot, 1, 2), ys, x, mod3, mod_base, per_batch_mod, b, t, cap)


def kernel(x_prompt, x_sample, cache_a_k, cache_a_v, cache_b_k, cache_b_v, state_c_h, c, c_ctx, ada_w, ada_b, norm_mix_g, norm_ffn_g, a_w_in, a_q_norm_g, a_k_norm_g, a_lam_q, a_lam_k, a_subln_g, a_w_out, b_w_in, b_q_norm_g, b_k_norm_g, b_sink, b_w_out, c_w_in, c_conv_w, c_conv_b, c_w_rg, c_b_rg, c_w_ig, c_b_ig, c_lam, c_w_out, moe_w_router, moe_w_gate, moe_w_up, moe_w_down):
    bp, tp, _ = x_prompt.shape
    bs_, ts, _ = x_sample.shape
    past = cache_a_k.shape[2]
    cvec = jnp.concatenate([c, c_ctx[None, :], jnp.zeros((16 - bs_ - 1, D), F32)], axis=0)
    mods = _modulation(cvec, ada_w, ada_b)
    groups = {"p": (bp, tp, bs_, 0), "s": (bs_, ts, 0, 1)}
    xs = {"p": x_prompt.reshape(bp * tp, D), "s": x_sample.reshape(bs_ * ts, D)}
    ck_a = cache_a_k.reshape(bs_, -1, past, H_A * 2 * HD_A)
    cv_a = cache_a_v.reshape(bs_, -1, past, H_A * 2 * HD_A)
    ck_b = cache_b_k.reshape(bs_, -1, past, KV_B * HD_B)
    cv_b = cache_b_v.reshape(bs_, -1, past, KV_B * HD_B)
    moe_w = (moe_w_router, moe_w_gate, moe_w_up, moe_w_down)
    new_a_k, new_a_v, new_b_k, new_b_v, new_c_h = [], [], [], [], []
    ia = ib = ic = 0
    for l in range(DEPTH):
        mod3 = mods[l].reshape(16, 1, 6 * D)
        kind = l % 3
        for key in ("p", "s"):
            b, t, mbase, per_b = groups[key]
            rows_per_mod = t if per_b else b * t
            x = xs[key]
            sample = key == "s"
            if kind == 0:
                lam_init = 0.8 - 0.6 * math.exp(-0.3 * l)
                outs = _qkv_project(x, mod3, mbase, rows_per_mod, norm_mix_g[l], a_w_in[ia], a_q_norm_g[ia],
                                    a_k_norm_g[ia], D, D, D, t, rope=sample, kv_f32=not sample)
                q, k, v = outs[:3]
                if not sample:
                    new_a_k.append(outs[3].reshape(b, t, H_A, 2 * HD_A))
                    new_a_v.append(outs[4].reshape(b, t, H_A, 2 * HD_A))
                o = _diff_attention(q, k, v, (ck_a, cv_a, ia) if sample else None, a_lam_q[ia], a_lam_k[ia],
                                    a_subln_g[ia], lam_init, b, t, heads=1 if sample else H_A,
                                    tq=512 if sample else t, nsub=4 if sample else 1)
                x, h2 = _proj_residual(o, x, mod3, mbase, rows_per_mod, norm_ffn_g[l], a_w_out[ia])
            elif kind == 1:
                nq, nk = H_B * HD_B, KV_B * HD_B
                outs = _qkv_project(x, mod3, mbase, rows_per_mod, norm_mix_g[l], b_w_in[ib], b_q_norm_g[ib],
                                    b_k_norm_g[ib], nq, nk, nk, t, rope=sample, kv_f32=not sample)
                q, k, v = outs[:3]
                if not sample:
                    new_b_k.append(outs[3].reshape(b, t, KV_B, HD_B))
                    new_b_v.append(outs[4].reshape(b, t, KV_B, HD_B))
                o = _gqa_attention(q, k, v, (ck_b, cv_b, ib) if sample else None, b_sink[ib], b, t,
                                   windowed=sample, tq=256)
                x, h2 = _proj_residual(o, x, mod3, mbase, rows_per_mod, norm_ffn_g[l], b_w_out[ib])
            else:
                gx = _norm_mod_matmul(x, mod3, mbase, rows_per_mod, norm_mix_g[l], c_w_in[ic])
                state = state_c_h[:, ic] if sample else jnp.zeros((b, 2, D_RNN), F32)
                x, h2, finals = _lru_mixer(gx, x, state, c_conv_w[ic], c_conv_b[ic], c_w_rg[ic], c_b_rg[ic],
                                           c_w_ig[ic], c_b_ig[ic], c_lam[ic], c_w_out[ic], mod3, mbase, per_b,
                                           norm_ffn_g[l], b, t)
                if not sample:
                    new_c_h.append(finals)
            xs[key] = _ec_moe(x, h2, mod3, mbase, per_b, moe_w, l, b, t,
                              bs=2 if sample else 8, bg=1 if sample else 8)
        ia, ib, ic = ia + (kind == 0), ib + (kind == 1), ic + (kind == 2)
    return (xs["p"].reshape(bp, tp, D), xs["s"].reshape(bs_, ts, D),
            jnp.stack(new_a_k, axis=1), jnp.stack(new_a_v, axis=1),
            jnp.stack(new_b_k, axis=1), jnp.stack(new_b_v, axis=1), jnp.stack(new_c_h, axis=1))
```

```python
import functools
import math

import jax
import jax.numpy as jnp
from jax import lax
from jax.experimental import pallas as pl
from jax.experimental.pallas import tpu as pltpu

F32, BF16, I32 = jnp.float32, jnp.bfloat16, jnp.int32

D = 1024
DEPTH = 4
GRID_W = 64
H_A, HD_A = 8, 64
H_B, KV_B, G_B, HD_B = 16, 4, 4, 64
WINDOW = 128
D_RNN = 1024
N_BLK_C, BW_C = 16, 64
LRU_C = 8.0
N_EXPERTS = 16
EC_FACTOR = 2
ROPE_THETA = 10000.0
EPS = 1e-6
LOG2E = math.log2(math.e)
LANES = 128
MXU_DIM = 256
MIB = 1024 * 1024


def _cparams(n_axes, vmem_mib=48):
    return pltpu.CompilerParams(dimension_semantics=("arbitrary",) * n_axes,
                                vmem_limit_bytes=vmem_mib * MIB)


def _sigmoid(x):
    return 0.5 * jnp.tanh(0.5 * x) + 0.5


def _rms(x):
    return x * lax.rsqrt(jnp.mean(x * x, axis=-1, keepdims=True) + EPS)


def _dot(a, b):
    return jnp.dot(a, b, preferred_element_type=F32)


def _dot_nt(a, b):
    return lax.dot_general(a, b, (((1,), (1,)), ((), ())), preferred_element_type=F32)


def _mod_kernel(c_ref, w_ref, b_ref, o_ref):
    c = c_ref[...]
    s = (c * _sigmoid(c)).astype(BF16)
    o_ref[...] = _dot(s, w_ref[...].astype(BF16)) + b_ref[...]


def _modulation(cvec, ada_w, ada_b):
    nt = 1536
    return pl.pallas_call(
        _mod_kernel,
        grid=(DEPTH, 6 * D // nt),
        in_specs=[pl.BlockSpec((16, D), lambda l, j: (0, 0)),
                  pl.BlockSpec((None, D, nt), lambda l, j: (l, 0, j)),
                  pl.BlockSpec((None, 1, nt), lambda l, j: (l, 0, j))],
        out_specs=pl.BlockSpec((None, 16, nt), lambda l, j: (l, 0, j)),
        out_shape=jax.ShapeDtypeStruct((DEPTH, 16, 6 * D), F32),
        compiler_params=_cparams(2, 32),
        name="adaln_mod",
    )(cvec, ada_w, ada_b.reshape(DEPTH, 1, 6 * D))


def _nmm_kernel(x_ref, mod_ref, g_ref, w_ref, o_ref, wbf_ref):
    @pl.when(pl.program_id(0) == 0)
    def _():
        wbf_ref[...] = w_ref[...].astype(BF16)

    m = mod_ref[...]
    h = (_rms(x_ref[...]) * g_ref[...] * (1.0 + m[:, D:2 * D]) + m[:, 0:D]).astype(BF16)
    o_ref[...] = _dot(h, wbf_ref[...])


def _norm_mod_matmul(x, mod3, mod_base, rows_per_mod, g, w, tm=512):
    r, n = x.shape[0], w.shape[1]
    return pl.pallas_call(
        _nmm_kernel,
        grid=(r // tm,),
        in_specs=[pl.BlockSpec((tm, D), lambda i: (i, 0)),
                  pl.BlockSpec((None, 1, 6 * D), lambda i: (mod_base + (i * tm) // rows_per_mod, 0, 0)),
                  pl.BlockSpec((1, D), lambda i: (0, 0)),
                  pl.BlockSpec((D, n), lambda i: (0, 0), pipeline_mode=pl.Buffered(1))],
        out_specs=pl.BlockSpec((tm, n), lambda i: (i, 0)),
        out_shape=jax.ShapeDtypeStruct((r, n), F32),
        scratch_shapes=[pltpu.VMEM((D, n), BF16)],
        compiler_params=_cparams(1, 48),
        name="norm_mod_matmul",
    )(x, mod3, g.reshape(1, D), w)


def _proj_tail(y, x_ref, mod_ref, g2_ref, xo_ref, h2_ref):
    m = mod_ref[...]
    xn = x_ref[...] + m[:, 2 * D:3 * D] * y
    xo_ref[...] = xn
    h2_ref[...] = (_rms(xn) * g2_ref[...] * (1.0 + m[:, 4 * D:5 * D]) + m[:, 3 * D:4 * D]).astype(BF16)


def _proj_kernel(o_ref, x_ref, mod_ref, g2_ref, w_ref, xo_ref, h2_ref, wbf_ref):
    @pl.when(pl.program_id(0) == 0)
    def _():
        wbf_ref[...] = w_ref[...].astype(BF16)

    _proj_tail(_dot(o_ref[...], wbf_ref[...]), x_ref, mod_ref, g2_ref, xo_ref, h2_ref)


def _proj_residual(o, x, mod3, mod_base, rows_per_mod, g2, w, tm=512):
    r = x.shape[0]
    return pl.pallas_call(
        _proj_kernel,
        grid=(r // tm,),
        in_specs=[pl.BlockSpec((tm, D), lambda i: (i, 0)),
                  pl.BlockSpec((tm, D), lambda i: (i, 0)),
                  pl.BlockSpec((None, 1, 6 * D), lambda i: (mod_base + (i * tm) // rows_per_mod, 0, 0)),
                  pl.BlockSpec((1, D), lambda i: (0, 0)),
                  pl.BlockSpec((D, D), lambda i: (0, 0))],
        out_specs=[pl.BlockSpec((tm, D), lambda i: (i, 0)), pl.BlockSpec((tm, D), lambda i: (i, 0))],
        out_shape=[jax.ShapeDtypeStruct((r, D), F32), jax.ShapeDtypeStruct((r, D), BF16)],
        scratch_shapes=[pltpu.VMEM((D, D), BF16)],
        compiler_params=_cparams(1, 40),
        name="proj_residual",
    )(o, x, mod3, g2.reshape(1, D), w)


def _group_inv_rms(x, on_mxu):
    ss = x * x
    if not on_mxu:
        low = lax.broadcasted_iota(I32, (1, LANES), 1) < HD_A
        s_lo = jnp.sum(jnp.where(low, ss, 0.0), axis=-1, keepdims=True)
        s_hi = jnp.sum(jnp.where(low, 0.0, ss), axis=-1, keepdims=True)
        return jnp.where(low, lax.rsqrt(s_lo * (1.0 / HD_A) + EPS), lax.rsqrt(s_hi * (1.0 / HD_A) + EPS))
    hi = ss.astype(BF16)
    lo = (ss - hi.astype(F32)).astype(BF16)
    r = lax.broadcasted_iota(I32, (LANES, LANES), 0) >> 6
    c = lax.broadcasted_iota(I32, (LANES, LANES), 1) >> 6
    ones_bd = jnp.where(r == c, 1.0, 0.0).astype(BF16)
    return lax.rsqrt((_dot(hi, ones_bd) + _dot(lo, ones_bd)) * (1.0 / HD_A) + EPS)


def _rope128(y, cos, sa, sb):
    return y * cos + pltpu.roll(y, LANES - 16, 1) * sa + pltpu.roll(y, 16, 1) * sb


def _qkv_kernel(*refs, nq, nk, rope, kv_f32, nsub):
    it = iter(refs)
    x_ref, mod_ref, g_ref, w_ref, qg_ref, kg_ref = (next(it) for _ in range(6))
    if rope:
        cos_ref, sa_ref, sb_ref = next(it), next(it), next(it)
    q_ref, k_ref, v_ref = next(it), next(it), next(it)
    kf_ref, vf_ref = (next(it), next(it)) if kv_f32 else (None, None)
    wbf_ref, qkv_ref = next(it), next(it)

    @pl.when(pl.program_id(0) == 0)
    def _():
        wbf_ref[...] = w_ref[...].astype(BF16)

    m = mod_ref[...]
    tm = x_ref.shape[0]
    parts = [slice(i * (tm // nsub), (i + 1) * (tm // nsub)) for i in range(nsub)]
    for rows in parts:
        h = (_rms(x_ref[rows, :]) * g_ref[...] * (1.0 + m[:, D:2 * D]) + m[:, 0:D]).astype(BF16)
        qkv_ref[rows, :] = _dot(h, wbf_ref[...])
    q_gain = qg_ref[...] * (HD_A ** -0.5 * LOG2E)
    for rows in parts:
        if rope:
            cos, sa, sb = cos_ref[rows, :], sa_ref[rows, :], sb_ref[rows, :]
        for j in range((nq + nk) // LANES):
            x = qkv_ref[rows, j * LANES:(j + 1) * LANES]
            is_q = j < nq // LANES
            y = x * _group_inv_rms(x, on_mxu=rope) * (q_gain if is_q else kg_ref[...])
            if rope:
                y = _rope128(y, cos, sa, sb)
            if is_q:
                q_ref[rows, j * LANES:(j + 1) * LANES] = y.astype(BF16)
            else:
                jj = j - nq // LANES
                k_ref[rows, jj * LANES:(jj + 1) * LANES] = y.astype(BF16)
                if kv_f32:
                    kf_ref[rows, jj * LANES:(jj + 1) * LANES] = y
        v = qkv_ref[rows, nq + nk:]
        v_ref[rows, :] = v.astype(BF16)
        if kv_f32:
            vf_ref[rows, :] = v


def _rope_tables(t):
    n_freq = HD_A // 4
    inv_freq = ROPE_THETA ** (-jnp.arange(n_freq, dtype=F32) / n_freq)
    pos_row = jnp.repeat(jnp.arange(t // GRID_W), GRID_W).astype(F32)
    pos_col = jnp.tile(jnp.arange(GRID_W), t // GRID_W).astype(F32)
    ang_r = pos_row[:, None] * inv_freq[None, :]
    ang_c = pos_col[:, None] * inv_freq[None, :]
    z = jnp.zeros_like(ang_r)
    cos64 = jnp.concatenate([jnp.cos(ang_r), jnp.cos(ang_r), jnp.cos(ang_c), jnp.cos(ang_c)], axis=-1)
    sa64 = jnp.concatenate([-jnp.sin(ang_r), z, -jnp.sin(ang_c), z], axis=-1)
    sb64 = jnp.concatenate([z, jnp.sin(ang_r), z, jnp.sin(ang_c)], axis=-1)
    return tuple(jnp.tile(a, (1, 2)) for a in (cos64, sa64, sb64))


def _qkv_project(x, mod3, mod_base, rows_per_mod, g, w, qg, kg, nq, nk, nv, t, rope, kv_f32, tm=512):
    r = x.shape[0]
    n = nq + nk + nv
    tile2 = lambda a: jnp.tile(a.reshape(1, HD_A), (1, 2))
    row = lambda width: pl.BlockSpec((tm, width), lambda i: (i, 0))
    args = [x, mod3, g.reshape(1, D), w, tile2(qg), tile2(kg)]
    in_specs = [row(D),
                pl.BlockSpec((None, 1, 6 * D), lambda i: (mod_base + (i * tm) // rows_per_mod, 0, 0)),
                pl.BlockSpec((1, D), lambda i: (0, 0)),
                pl.BlockSpec((D, n), lambda i: (0, 0), pipeline_mode=pl.Buffered(1)),
                pl.BlockSpec((1, LANES), lambda i: (0, 0)),
                pl.BlockSpec((1, LANES), lambda i: (0, 0))]
    if rope:
        args += list(_rope_tables(t))
        in_specs += [pl.BlockSpec((tm, LANES), lambda i: (i % (t // tm), 0))] * 3
    out_shape = [jax.ShapeDtypeStruct((r, nq), BF16), jax.ShapeDtypeStruct((r, nk), BF16),
                 jax.ShapeDtypeStruct((r, nv), BF16)]
    out_specs = [row(nq), row(nk), row(nv)]
    if kv_f32:
        out_shape += [jax.ShapeDtypeStruct((r, nk), F32), jax.ShapeDtypeStruct((r, nv), F32)]
        out_specs += [row(nk), row(nv)]
    return pl.pallas_call(
        functools.partial(_qkv_kernel, nq=nq, nk=nk, rope=rope, kv_f32=kv_f32, nsub=2),
        grid=(r // tm,),
        in_specs=in_specs, out_specs=out_specs, out_shape=out_shape,
        scratch_shapes=[pltpu.VMEM((D, n), BF16), pltpu.VMEM((tm, n), F32)],
        compiler_params=_cparams(1, 48),
        name="qkv_project",
    )(*args)


def _diff_attn_kernel(*refs, heads, ctx, lam_init, nsub):
    it = iter(refs)
    q_ref, k_ref, v_ref = next(it), next(it), next(it)
    if ctx:
        ck_ref, cv_ref = next(it), next(it)
    lq_ref, lk_ref, sg_ref, o_ref = next(it), next(it), next(it), next(it)
    e = jnp.exp(jnp.sum(lq_ref[...] * lk_ref[...], axis=-1, keepdims=True))
    lam = e[0:1, :] - e[1:2, :] + lam_init
    lane = lax.broadcasted_iota(I32, (1, LANES), 1)
    tqs = q_ref.shape[0] // nsub
    units = [(slice(h * LANES, (h + 1) * LANES), slice(j * tqs, (j + 1) * tqs))
             for h in range(heads) for j in range(nsub)]

    def scores(u):
        sl, rows = units[u]
        q = q_ref[rows, sl]
        out = []
        for c in range(2):
            qc = jnp.where((lane < HD_A) if c == 0 else (lane >= HD_A), q, jnp.zeros_like(q))
            out.append((_dot_nt(qc, k_ref[:, sl]), _dot_nt(qc, ck_ref[:, sl].astype(BF16)) if ctx else None))
        return out

    def weights(sc2):
        ps, ls = [], []
        for s, sc in sc2:
            m = jnp.max(s, axis=-1, keepdims=True)
            if ctx:
                m = jnp.maximum(m, jnp.max(sc, axis=-1, keepdims=True))
            p = jnp.exp2(s - m)
            l = jnp.sum(p, axis=-1, keepdims=True)
            pc = None
            if ctx:
                pc = jnp.exp2(sc - m)
                l = l + jnp.sum(pc, axis=-1, keepdims=True)
            ps.append((p, pc))
            ls.append(l)
        ratio = lam * ls[0] / ls[1]
        w_lat = (ps[0][0] - ratio * ps[1][0]).astype(BF16)
        w_ctx = (ps[0][1] - ratio * ps[1][1]).astype(BF16) if ctx else None
        return w_lat, w_ctx, 1.0 / ls[0]

    def values(u, w):
        sl, rows = units[u]
        w_lat, w_ctx, inv_l0 = w
        o = _dot(w_lat, v_ref[:, sl])
        if ctx:
            o = o + _dot(w_ctx, cv_ref[:, sl].astype(BF16))
        o_ref[rows, sl] = (_rms(o * inv_l0) * sg_ref[...] * (1.0 - lam_init)).astype(BF16)

    if not ctx:
        s_ref = next(it)
        tk = k_ref.shape[0]
        for u in range(len(units)):
            for c, (s, _) in enumerate(scores(u)):
                s_ref[(2 * u + c) * tqs:(2 * u + c + 1) * tqs, :] = s
        s = s_ref[...]
        p = jnp.exp2(s - jnp.max(s, axis=-1, keepdims=True))
        l = jnp.sum(p, axis=-1, keepdims=True)
        for u in range(len(units)):
            r0, r1 = slice(2 * u * tqs, (2 * u + 1) * tqs), slice((2 * u + 1) * tqs, (2 * u + 2) * tqs)
            w = (p[r0, :] - (lam * l[r0, :] / l[r1, :]) * p[r1, :]).astype(BF16)
            values(u, (w, None, 1.0 / l[r0, :]))
        return
    n = len(units)
    sc = {u: scores(u) for u in range(min(2, n))}
    for u in range(n):
        w = weights(sc.pop(u))
        if u + 2 < n:
            sc[u + 2] = scores(u + 2)
        values(u, w)


def _diff_attention(q, k, v, cache, lam_q, lam_k, subln_g, lam_init, b, t, heads, tq, nsub):
    nq = t // tq
    hb = H_A // heads
    args = [q, k, v]
    in_specs = [pl.BlockSpec((tq, heads * LANES), lambda bi, hi, qi: (bi * nq + qi, hi)),
                pl.BlockSpec((t, heads * LANES), lambda bi, hi, qi: (bi, hi)),
                pl.BlockSpec((t, heads * LANES), lambda bi, hi, qi: (bi, hi))]
    if cache is not None:
        ck, cv, la = cache
        p = ck.shape[2]
        args += [ck, cv]
        in_specs += [pl.BlockSpec((None, None, p, heads * LANES), lambda bi, hi, qi: (bi, la, 0, hi))] * 2
    args += [lam_q, lam_k, subln_g.reshape(1, LANES)]
    in_specs += [pl.BlockSpec((2, HD_A), lambda bi, hi, qi: (0, 0)),
                 pl.BlockSpec((2, HD_A), lambda bi, hi, qi: (0, 0)),
                 pl.BlockSpec((1, LANES), lambda bi, hi, qi: (0, 0))]
    return pl.pallas_call(
        functools.partial(_diff_attn_kernel, heads=heads, ctx=cache is not None, lam_init=lam_init, nsub=nsub),
        grid=(b, hb, nq),
        in_specs=in_specs,
        out_specs=pl.BlockSpec((tq, heads * LANES), lambda bi, hi, qi: (bi * nq + qi, hi)),
        out_shape=jax.ShapeDtypeStruct((b * t, H_A * LANES), BF16),
        scratch_shapes=[] if cache is not None else [pltpu.VMEM((2 * heads * tq, t), F32)],
        compiler_params=_cparams(3, 48),
        name="diff_attention",
    )(*args)


def _both_halves(x, half):
    lane_half = lax.broadcasted_iota(I32, (1, LANES), 1) >> 6
    xm = jnp.where(lane_half == half, x, 0.0)
    return xm + pltpu.roll(xm, HD_B, 1)


def _value_with_ones(x, half):
    lane = lax.broadcasted_iota(I32, (1, LANES), 1)
    return jnp.where(lane < HD_B, _both_halves(x, half), jnp.where(lane == HD_B, 1.0, 0.0))


def _gqa_dense_body(q_ref, k_ref, v_ref, sink_ref, o_ref, s_ref, t):
    lane_half = lax.broadcasted_iota(I32, (1, LANES), 1) >> 6
    vds, sinks = [], []
    for kv in range(KV_B):
        kcol = slice((kv // 2) * LANES, (kv // 2 + 1) * LANES)
        kd = _both_halves(k_ref[:, kcol].astype(F32), kv % 2).astype(BF16)
        vds.append(_value_with_ones(v_ref[:, kcol].astype(F32), kv % 2).astype(BF16))
        for g in range(G_B):
            h = kv * G_B + g
            q = q_ref[:, (h // 2) * LANES:(h // 2 + 1) * LANES]
            qm = jnp.where(lane_half == h % 2, q, jnp.zeros_like(q))
            s_ref[h * t:(h + 1) * t, :] = _dot_nt(qm, kd)
            sinks.append(jnp.broadcast_to(sink_ref[kv:kv + 1, g:g + 1] * LOG2E, (t, 1)))
    s = s_ref[...]
    sink2 = jnp.concatenate(sinks, axis=0)
    m = jnp.maximum(jnp.max(s, axis=-1, keepdims=True), sink2)
    p = jnp.exp2(s - m).astype(BF16)
    tail = jnp.exp2(sink2 - m)
    for hp in range(H_B // 2):
        halves = []
        for h in (2 * hp, 2 * hp + 1):
            rows = slice(h * t, (h + 1) * t)
            ov = _dot(p[rows, :], vds[h // G_B])
            halves.append(ov * (1.0 / (ov[:, HD_B:HD_B + 1] + tail[rows, :])))
        o_ref[:, hp * LANES:(hp + 1) * LANES] = jnp.where(
            lane_half == 0, halves[0], pltpu.roll(halves[1], HD_B, 1)).astype(BF16)


def _gqa_kernel(*refs, kvs, windowed, ctx, t, tq):
    it = iter(refs)
    q_ref, k_ref, v_ref = next(it), next(it), next(it)
    if ctx:
        ck_ref, cv_ref = next(it), next(it)
    sink_ref, o_ref = next(it), next(it)
    lane_half = lax.broadcasted_iota(I32, (1, LANES), 1) >> 6
    if not windowed:
        _gqa_dense_body(q_ref, k_ref, v_ref, sink_ref, o_ref, next(it), t)
        return
    kd_ref, vd_ref, ckd_ref, cvd_ref = next(it), next(it), next(it), next(it)

    @pl.when(pl.program_id(2) == 0)
    def _():
        for half in range(2):
            kd_ref[half] = _both_halves(k_ref[...].astype(F32), half).astype(BF16)
            vd_ref[half] = _value_with_ones(v_ref[...].astype(F32), half).astype(BF16)
            ckd_ref[half] = _both_halves(ck_ref[...], half).astype(BF16)
            cvd_ref[half] = _value_with_ones(cv_ref[...], half).astype(BF16)

    span = tq + 2 * WINDOW
    q0 = pl.program_id(2) * tq
    start = pl.multiple_of(jnp.clip(q0 - WINDOW, 0, t - span), LANES)
    qpos = q0 + lax.broadcasted_iota(I32, (tq, span), 0)
    kpos = start + lax.broadcasted_iota(I32, (tq, span), 1)
    valid = jnp.abs(qpos - kpos) <= WINDOW
    units = []
    for half in range(2):
        kd, ckd = kd_ref[half, pl.ds(start, span), :], ckd_ref[half]
        for g in range(G_B):
            hl = half * G_B + g
            qcol = slice(hl // 2 * LANES, (hl // 2 + 1) * LANES)
            q = q_ref[:, qcol]
            qm = jnp.where(lane_half == hl % 2, q, jnp.zeros_like(q))
            units.append((half, hl, qcol, sink_ref[half:half + 1, g:g + 1], _dot_nt(qm, kd), _dot_nt(qm, ckd)))
    o_pair = None
    for half, hl, qcol, sink, s, sc in units:
        sink2 = sink * LOG2E
        s = jnp.where(valid, s, -jnp.inf)
        m = jnp.maximum(jnp.maximum(jnp.max(s, axis=-1, keepdims=True), sink2), jnp.max(sc, axis=-1, keepdims=True))
        ov = (_dot(jnp.exp2(s - m).astype(BF16), vd_ref[half, pl.ds(start, span), :])
              + _dot(jnp.exp2(sc - m).astype(BF16), cvd_ref[half]))
        o = ov * (1.0 / (ov[:, HD_B:HD_B + 1] + jnp.exp2(sink2 - m)))
        if hl % 2 == 0:
            o_pair = jnp.where(lane_half == 0, o, 0.0)
        else:
            o_ref[:, qcol] = (o_pair + jnp.where(lane_half == 1, pltpu.roll(o, HD_B, 1), 0.0)).astype(BF16)


def _gqa_attention(q, k, v, cache, sink, b, t, windowed, tq):
    if windowed:
        kvs, nq = 2, t // tq
        grid = (b, KV_B // 2, nq)
        qw = 2 * G_B * HD_B
        q_spec = pl.BlockSpec((tq, qw), lambda bi, kv, qi: (bi * nq + qi, kv))
        kv_spec = pl.BlockSpec((t, LANES), lambda bi, kv, qi: (bi, kv))
        sink_arr = sink.reshape(KV_B // 2, 2, G_B)
        sink_spec = pl.BlockSpec((None, 2, G_B), lambda bi, kv, qi: (kv, 0, 0))
        o_spec = pl.BlockSpec((tq, qw), lambda bi, kv, qi: (bi * nq + qi, kv))
    else:
        kvs, tq = KV_B, t
        grid = (b, 1, 1)
        q_spec = pl.BlockSpec((t, H_B * HD_B), lambda bi, kv, qi: (bi, 0))
        kv_spec = pl.BlockSpec((t, KV_B * HD_B), lambda bi, kv, qi: (bi, 0))
        sink_arr = sink.reshape(KV_B, G_B)
        sink_spec = pl.BlockSpec((KV_B, G_B), lambda bi, kv, qi: (0, 0))
        o_spec = pl.BlockSpec((t, H_B * HD_B), lambda bi, kv, qi: (bi, 0))
    args, in_specs = [q, k, v], [q_spec, kv_spec, kv_spec]
    if cache is not None:
        ck, cv, lb = cache
        p = ck.shape[2]
        args += [ck, cv]
        in_specs += [pl.BlockSpec((None, None, p, LANES), lambda bi, kv, qi: (bi, lb, 0, kv))] * 2
    args.append(sink_arr)
    in_specs.append(sink_spec)
    scratch = [pltpu.VMEM((H_B * t, t), F32)]
    if windowed:
        scratch = [pltpu.VMEM((2, t, LANES), BF16)] * 2 + [pltpu.VMEM((2, p, LANES), BF16)] * 2
    return pl.pallas_call(
        functools.partial(_gqa_kernel, kvs=kvs, windowed=windowed, ctx=cache is not None, t=t, tq=tq),
        grid=grid, in_specs=in_specs, out_specs=o_spec,
        out_shape=jax.ShapeDtypeStruct((b * t, H_B * HD_B), BF16),
        scratch_shapes=scratch,
        compiler_params=_cparams(3, 48),
        name="gqa_attention",
    )(*args)


def _lru_conv_gates(xp_ref, xc_ref, xn_ref, cw_ref, cb_ref, wg_ref, bg_ref, lam_ref, a_ref, u_ref, ti, nt, tt):
    nl = -lam_ref[...]
    sp = jnp.maximum(nl, 0.0) + jnp.log1p(jnp.exp(-jnp.abs(nl)))
    cur = xc_ref[...]
    prev = xp_ref[...] * (ti > 0).astype(F32)
    nxt = xn_ref[...] * (ti < nt - 1).astype(F32)
    row = lax.broadcasted_iota(I32, (tt, 1), 0)
    xm1 = jnp.where(row == 0, prev[7:8, :], pltpu.roll(cur, 1, 0))
    xm2 = jnp.where(row == 0, prev[6:7, :], jnp.where(row == 1, prev[7:8, :], pltpu.roll(cur, 2, 0)))
    xp1 = jnp.where(row == tt - 1, nxt[0:1, :], pltpu.roll(cur, tt - 1, 0))
    cw = cw_ref[...]
    xc = xm2 * cw[0:1, :] + xm1 * cw[1:2, :] + cur * cw[2:3, :] + xp1 * cw[3:4, :] + cb_ref[...]
    xcb = xc.astype(BF16)
    for j in range(D_RNN // MXU_DIM):
        cs = slice(j * MXU_DIM, (j + 1) * MXU_DIM)
        z = _dot(xcb[:, cs], wg_ref[j].astype(BF16))
        r = _sigmoid(z[:, :MXU_DIM] + bg_ref[0:1, cs])
        i = _sigmoid(z[:, MXU_DIM:] + bg_ref[1:2, cs])
        a = jnp.exp(-LRU_C * r * sp[:, cs])
        a_ref[:, cs] = a
        u_ref[:, cs] = jnp.sqrt(1.0 - a * a) * i * xc[:, cs]


def _tile_scan(a_ref, u_ref, hs_ref, h, tt, reverse):
    row = lax.broadcasted_iota(I32, (8, 1), 0)
    ng = tt // 8
    for g in (range(ng - 1, -1, -1) if reverse else range(ng)):
        rows = slice(g * 8, (g + 1) * 8)
        a8, u8 = a_ref[rows, :], u_ref[rows, :]
        for s in (1, 2, 4):
            keep = (row < 8 - s) if reverse else (row >= s)
            shift = 8 - s if reverse else s
            u8 = a8 * jnp.where(keep, pltpu.roll(u8, shift, 0), 0.0) + u8
            a8 = a8 * jnp.where(keep, pltpu.roll(a8, shift, 0), 1.0)
        h8 = a8 * h + u8
        hs_ref[rows, :] = h8
        h = h8[0:1, :] if reverse else h8[7:8, :]
    return h


def _lru_fwd_kernel(xp_ref, xc_ref, xn_ref, cw_ref, cb_ref, wg_ref, bg_ref, lam_ref, h0_ref,
                    hf_ref, last_ref, carry_ref, a_ref, u_ref, *, tt):
    ti, nt = pl.program_id(1), pl.num_programs(1)

    @pl.when(ti == 0)
    def _():
        carry_ref[...] = h0_ref[...]

    _lru_conv_gates(xp_ref, xc_ref, xn_ref, cw_ref, cb_ref, wg_ref, bg_ref, lam_ref, a_ref, u_ref, ti, nt, tt)
    h = _tile_scan(a_ref, u_ref, hf_ref, carry_ref[...], tt, reverse=False)
    carry_ref[...] = h
    last_ref[...] = h


def _lru_bwd_kernel(xp_ref, xc_ref, xn_ref, cw_ref, cb_ref, wg_ref, bg_ref, lam_ref, h0_ref,
                    gate_ref, hf_ref, x_ref, mod_ref, g2_ref, w_ref,
                    xo_ref, h2_ref, last_ref, carry_ref, a_ref, u_ref, hb_ref, wbf_ref, *, tt):
    ti, nt = pl.program_id(1), pl.num_programs(1)

    @pl.when((pl.program_id(0) == 0) & (ti == 0))
    def _():
        wbf_ref[...] = w_ref[...].astype(BF16)

    @pl.when(ti == 0)
    def _():
        carry_ref[...] = h0_ref[...]

    tr = nt - 1 - ti
    _lru_conv_gates(xp_ref, xc_ref, xn_ref, cw_ref, cb_ref, wg_ref, bg_ref, lam_ref, a_ref, u_ref, tr, nt, tt)
    h = _tile_scan(a_ref, u_ref, hb_ref, carry_ref[...], tt, reverse=True)
    carry_ref[...] = h
    last_ref[...] = h
    g = gate_ref[...]
    gelu = 0.5 * g * (1.0 + jnp.tanh(math.sqrt(2.0 / math.pi) * (g + 0.044715 * (g * g * g))))
    y = (gelu * (hf_ref[...] + hb_ref[...])).astype(BF16)
    _proj_tail(_dot(y, wbf_ref[...]), x_ref, mod_ref, g2_ref, xo_ref, h2_ref)


def _lru_mixer(gx, x, state, conv_w, conv_b, w_rg, b_rg, w_ig, b_ig, lam, w_out, mod3, mod_base, per_batch_mod,
               g2, b, t, tt=256):
    nt = t // tt
    per_tile = MXU_DIM // BW_C
    eye = jnp.eye(per_tile, dtype=F32)

    def bd(w):
        w4 = w.reshape(N_BLK_C // per_tile, per_tile, BW_C, BW_C)
        return (w4[:, :, :, None, :] * eye[None, :, None, :, None]).reshape(-1, MXU_DIM, MXU_DIM)

    full = lambda shape: pl.BlockSpec(shape, lambda bi, ti: (0,) * len(shape))
    finals, xo, h2, hf = [], None, None, None
    for d in range(2):
        tile = (lambda ti: ti) if d == 0 else (lambda ti: nt - 1 - ti)
        x_spec = lambda f: pl.BlockSpec((tt, D_RNN), f)
        row = lambda bi, ti, tile=tile: (bi * nt + tile(ti), 0)
        halo = lambda f: pl.BlockSpec((8, D_RNN), f)
        in_specs = [halo(lambda bi, ti, tile=tile: (jnp.maximum((bi * nt + tile(ti)) * (tt // 8) - 1, 0), 1)),
                    x_spec(lambda bi, ti, tile=tile: (bi * nt + tile(ti), 1)),
                    halo(lambda bi, ti, tile=tile: (jnp.minimum((bi * nt + tile(ti) + 1) * (tt // 8), b * t // 8 - 1), 1)),
                    full((4, D_RNN)), full((1, D_RNN)), full((4, MXU_DIM, 2 * MXU_DIM)),
                    full((2, D_RNN)), full((1, D_RNN)),
                    pl.BlockSpec((None, 1, D_RNN), lambda bi, ti: (bi, 0, 0))]
        args = [gx, gx, gx, conv_w, conv_b.reshape(1, D_RNN),
                jnp.concatenate([bd(w_rg[d]), bd(w_ig[d])], axis=-1), jnp.stack([b_rg[d], b_ig[d]]),
                lam[d].reshape(1, D_RNN), state[:, d].reshape(b, 1, D_RNN)]
        scratch = [pltpu.VMEM((1, D_RNN), F32), pltpu.VMEM((tt, D_RNN), F32), pltpu.VMEM((tt, D_RNN), F32)]
        last_spec = pl.BlockSpec((None, 1, D_RNN), lambda bi, ti: (bi, 0, 0))
        last_shape = jax.ShapeDtypeStruct((b, 1, D_RNN), F32)
        if d == 0:
            hf, last_f = pl.pallas_call(
                functools.partial(_lru_fwd_kernel, tt=tt),
                grid=(b, nt), in_specs=in_specs,
                out_specs=[pl.BlockSpec((tt, D_RNN), row), last_spec],
                out_shape=[jax.ShapeDtypeStruct((b * t, D_RNN), F32), last_shape],
                scratch_shapes=scratch,
                compiler_params=_cparams(2, 48),
                name="lru_fwd",
            )(*args)
        else:
            in_specs += [pl.BlockSpec((tt, D_RNN), row), pl.BlockSpec((tt, D_RNN), row), pl.BlockSpec((tt, D), row),
                         pl.BlockSpec((None, 1, 6 * D), lambda bi, ti: (mod_base + bi * per_batch_mod, 0, 0)),
                         full((1, D)), full((D_RNN, D))]
            args += [gx, hf, x, mod3, g2.reshape(1, D), w_out]
            xo, h2, last_b = pl.pallas_call(
                functools.partial(_lru_bwd_kernel, tt=tt),
                grid=(b, nt), in_specs=in_specs,
                out_specs=[pl.BlockSpec((tt, D), row), pl.BlockSpec((tt, D), row), last_spec],
                out_shape=[jax.ShapeDtypeStruct((b * t, D), F32), jax.ShapeDtypeStruct((b * t, D), BF16), last_shape],
                scratch_shapes=scratch + [pltpu.VMEM((tt, D_RNN), F32), pltpu.VMEM((D_RNN, D), BF16)],
                compiler_params=_cparams(2, 48),
                name="lru_bwd_proj",
            )(*args)
    return xo, h2, jnp.concatenate([last_f, last_b], axis=1)


def _select_kernel(h_ref, wr_ref, slot_ref, aff_ref, tri_ref, *, bs, t, cap):
    @pl.when(pl.program_id(0) == 0)
    def _():
        r = lax.broadcasted_iota(I32, (t, t), 0)
        c = lax.broadcasted_iota(I32, (t, t), 1)
        tri_ref[...] = jnp.where(r < c, 1.0, 0.0).astype(BF16)

    wr = wr_ref[...].astype(BF16)
    affs = []
    for s in range(bs):
        logits = _dot_nt(wr, h_ref[s])
        ex = jnp.exp(logits - jnp.max(logits, axis=0, keepdims=True))
        affs.append(ex / jnp.sum(ex, axis=0, keepdims=True))
    aff = jnp.concatenate(affs, axis=0) if bs > 1 else affs[0]
    bits = pltpu.bitcast(aff, I32)
    count = lambda mask: jnp.sum(jnp.where(mask, 1.0, 0.0), axis=-1, keepdims=True)
    th = jnp.zeros((bs * N_EXPERTS, 1), I32)
    for bit in range(30, -1, -1):
        cand = th | (1 << bit)
        th = jnp.where(count(bits >= cand) >= cap, cand, th)
    gt, eq = bits > th, bits == th
    need = cap - count(gt)
    lane = lax.broadcasted_iota(I32, (1, t), 1)
    lim = jnp.zeros((bs * N_EXPERTS, 1), I32)
    for bit in range(t.bit_length() - 1, -1, -1):
        cand = lim | (1 << bit)
        ok = (cand <= t) & (count(eq & (lane < cand)) <= need)
        lim = jnp.where(ok, cand, lim)
    sel = gt | (eq & (lane < lim))
    pos = _dot(jnp.where(sel, 1.0, 0.0).astype(BF16), tri_ref[...])
    slot = jnp.where(sel, pos.astype(I32), -1)
    for s in range(bs):
        slot_ref[s] = slot[s * N_EXPERTS:(s + 1) * N_EXPERTS, :]
        aff_ref[s] = aff[s * N_EXPERTS:(s + 1) * N_EXPERTS, :]


def _moe_select(h2, w_router, b, t, cap, bs):
    return pl.pallas_call(
        functools.partial(_select_kernel, bs=bs, t=t, cap=cap),
        grid=(b // bs,),
        in_specs=[pl.BlockSpec((bs, t, D), lambda i: (i, 0, 0)),
                  pl.BlockSpec((N_EXPERTS, D), lambda i: (0, 0))],
        out_specs=[pl.BlockSpec((bs, N_EXPERTS, t), lambda i: (i, 0, 0))] * 2,
        out_shape=[jax.ShapeDtypeStruct((b, N_EXPERTS, t), I32), jax.ShapeDtypeStruct((b, N_EXPERTS, t), F32)],
        scratch_shapes=[pltpu.VMEM((t, t), BF16)],
        compiler_params=_cparams(1, 48),
        name="moe_select",
    )(h2, w_router.T)


def _gather_kernel(slot_ref, aff_ref, h_ref, xs_ref, gc_ref, *, bq, cap, eg):
    j = lax.broadcasted_iota(I32, (cap, 1), 0)
    for s in range(bq):
        h = h_ref[s]
        for e0 in range(0, N_EXPERTS, eg):
            hots = [slot_ref[s, e:e + 1, :] == j for e in range(e0, e0 + eg)]
            p = jnp.concatenate([jnp.where(o, 1.0, 0.0).astype(BF16) for o in hots], axis=0)
            xs = _dot(p, h).astype(BF16)
            for k, e in enumerate(range(e0, e0 + eg)):
                xs_ref[e, s * cap:(s + 1) * cap, :] = xs[k * cap:(k + 1) * cap, :]
                gc_ref[e, s * cap:(s + 1) * cap, :] = jnp.sum(jnp.where(hots[k], aff_ref[s, e:e + 1, :], 0.0),
                                                              axis=-1, keepdims=True)


def _moe_gather(slot, aff, h3, b, t, cap, bq, eg):
    sa_spec = pl.BlockSpec((bq, N_EXPERTS, t), lambda i: (i, 0, 0))
    return pl.pallas_call(
        functools.partial(_gather_kernel, bq=bq, cap=cap, eg=eg),
        grid=(b // bq,),
        in_specs=[sa_spec, sa_spec, pl.BlockSpec((bq, t, D), lambda i: (i, 0, 0))],
        out_specs=[pl.BlockSpec((N_EXPERTS, bq * cap, D), lambda i: (0, i, 0)),
                   pl.BlockSpec((N_EXPERTS, bq * cap, 1), lambda i: (0, i, 0))],
        out_shape=[jax.ShapeDtypeStruct((N_EXPERTS, b * cap, D), BF16),
                   jax.ShapeDtypeStruct((N_EXPERTS, b * cap, 1), F32)],
        compiler_params=_cparams(1, 56),
        name="moe_gather",
    )(slot, aff, h3)


def _ffn_kernel(xs_ref, gc_ref, wg_ref, wu_ref, wd_ref, ys_ref, wgb_ref, wub_ref, wdb_ref):
    @pl.when(pl.program_id(1) == 0)
    def _():
        wgb_ref[...] = wg_ref[...].astype(BF16)
        wub_ref[...] = wu_ref[...].astype(BF16)
        wdb_ref[...] = wd_ref[...].astype(BF16)

    xs = xs_ref[...]
    acc = None
    for c in range(0, wgb_ref.shape[1], MXU_DIM):
        cs = slice(c, c + MXU_DIM)
        zg = _dot(xs, wgb_ref[:, cs])
        hid = (zg * _sigmoid(zg) * _dot(xs, wub_ref[:, cs])).astype(BF16)
        part = _dot(hid, wdb_ref[cs, :])
        acc = part if acc is None else acc + part
    ys_ref[...] = (acc * gc_ref[...]).astype(BF16)


def _moe_ffn(xs, gc, w_gate, w_up, w_down, layer, tmf=1024):
    r, f = xs.shape[1], w_gate.shape[-1]
    w_spec = lambda d0, d1: pl.BlockSpec((None, None, d0, d1), lambda e, i: (layer, e, 0, 0))
    return pl.pallas_call(
        _ffn_kernel,
        grid=(N_EXPERTS, r // tmf),
        in_specs=[pl.BlockSpec((None, tmf, D), lambda e, i: (e, i, 0)),
                  pl.BlockSpec((None, tmf, 1), lambda e, i: (e, i, 0)),
                  w_spec(D, f), w_spec(D, f), w_spec(f, D)],
        out_specs=pl.BlockSpec((None, tmf, D), lambda e, i: (e, i, 0)),
        out_shape=jax.ShapeDtypeStruct((N_EXPERTS, r, D), BF16),
        scratch_shapes=[pltpu.VMEM((D, f), BF16), pltpu.VMEM((D, f), BF16), pltpu.VMEM((f, D), BF16)],
        compiler_params=_cparams(2, 56),
        name="moe_ffn",
    )(xs, gc, w_gate, w_up, w_down)


def _scatter_rows(slot, width, ys):
    n = N_EXPERTS * width
    e_of = lax.broadcasted_iota(I32, (N_EXPERTS, n), 1) >> (width.bit_length() - 1)
    rep = jnp.where(e_of == lax.broadcasted_iota(I32, (N_EXPERTS, n), 0), 1.0, 0.0).astype(BF16)
    slot_rep = _dot(slot.astype(F32).astype(BF16), rep)
    jn = (lax.broadcasted_iota(I32, (1, n), 1) & (width - 1)).astype(F32)
    return _dot(jnp.where(slot_rep == jn, 1.0, 0.0).astype(BF16), ys)


def _combine_kernel(slot_ref, ys_ref, x_ref, mod_ref, o_ref, *, cap):
    y = _scatter_rows(slot_ref[...], cap, ys_ref[...].reshape(N_EXPERTS * cap, D))
    o_ref[...] = x_ref[...] + mod_ref[:, 5 * D:6 * D] * y


def _combine_win_kernel(st_s, ok_s, slot_ref, stv_ref, ys_ref, x_ref, mod_ref, o_ref, ysw_ref, *, cap, win, nt):
    tile = pl.program_id(0) * nt + pl.program_id(1)
    ga2 = mod_ref[:, 5 * D:6 * D]

    @pl.when(ok_s[tile] != 0)
    def _():
        for e in range(N_EXPERTS):
            st = pl.multiple_of(st_s[tile * N_EXPERTS + e], 16)
            ysw_ref[e * win:(e + 1) * win, :] = ys_ref[e, pl.ds(st, win), :]
        slot = slot_ref[...]
        rel = jnp.where(slot >= 0, slot - stv_ref[...], -1)
        o_ref[...] = x_ref[...] + ga2 * _scatter_rows(rel, win, ysw_ref[...])

    @pl.when(ok_s[tile] == 0)
    def _():
        y = _scatter_rows(slot_ref[...], cap, ys_ref[...].reshape(N_EXPERTS * cap, D))
        o_ref[...] = x_ref[...] + ga2 * y


def _moe_combine_windowed(slot, ys, x, mod3, mod_base, per_batch_mod, b, t, cap, tt=512, win=128):
    nt = t // tt
    cnt = (slot >= 0).reshape(b, N_EXPERTS, nt, tt).sum(-1).astype(I32)
    start = jnp.cumsum(cnt, axis=-1) - cnt
    st = jnp.clip((start // 16) * 16, 0, cap - win)
    ok = jnp.all(start + cnt <= st + win, axis=1).astype(I32)
    st_t = jnp.swapaxes(st, 1, 2)
    grid_spec = pltpu.PrefetchScalarGridSpec(
        num_scalar_prefetch=2,
        grid=(b, nt),
        in_specs=[pl.BlockSpec((None, tt, N_EXPERTS), lambda bi, ti, *_: (bi, ti, 0)),
                  pl.BlockSpec((None, None, 1, N_EXPERTS), lambda bi, ti, *_: (bi, ti, 0, 0)),
                  pl.BlockSpec((N_EXPERTS, None, cap, D), lambda bi, ti, *_: (0, bi, 0, 0)),
                  pl.BlockSpec((tt, D), lambda bi, ti, *_: (bi * nt + ti, 0)),
                  pl.BlockSpec((None, 1, 6 * D), lambda bi, ti, *_: (mod_base + bi * per_batch_mod, 0, 0))],
        out_specs=pl.BlockSpec((tt, D), lambda bi, ti, *_: (bi * nt + ti, 0)),
        scratch_shapes=[pltpu.VMEM((N_EXPERTS * win, D), BF16)])
    return pl.pallas_call(
        functools.partial(_combine_win_kernel, cap=cap, win=win, nt=nt),
        grid_spec=grid_spec,
        out_shape=jax.ShapeDtypeStruct((b * t, D), F32),
        compiler_params=_cparams(2, 56),
        name="moe_combine_win",
    )(st_t.reshape(-1), ok.reshape(-1), jnp.swapaxes(slot, 1, 2), st_t.reshape(b, nt, 1, N_EXPERTS), ys, x, mod3)


def _moe_combine(slot_t, ys, x, mod3, mod_base, per_batch_mod, b, t, cap, tt=512):
    tt = min(tt, t)
    nt = t // tt
    return pl.pallas_call(
        functools.partial(_combine_kernel, cap=cap),
        grid=(b, nt),
        in_specs=[pl.BlockSpec((None, tt, N_EXPERTS), lambda bi, ti: (bi, ti, 0)),
                  pl.BlockSpec((N_EXPERTS, None, cap, D), lambda bi, ti: (0, bi, 0, 0)),
                  pl.BlockSpec((tt, D), lambda bi, ti: (bi * nt + ti, 0)),
                  pl.BlockSpec((None, 1, 6 * D), lambda bi, ti: (mod_base + bi * per_batch_mod, 0, 0))],
        out_specs=pl.BlockSpec((tt, D), lambda bi, ti: (bi * nt + ti, 0)),
        out_shape=jax.ShapeDtypeStruct((b * t, D), F32),
        compiler_params=_cparams(2, 48),
        name="moe_combine",
    )(slot_t, ys, x, mod3)


def _ec_moe(x, h2, mod3, mod_base, per_batch_mod, moe_w, layer, b, t, bs, bg):
    w_router, w_gate, w_up, w_down = moe_w
    cap = EC_FACTOR * t // N_EXPERTS
    h3 = h2.reshape(b, t, D)
    slot, aff = _moe_select(h3, w_router[layer], b, t, cap, bs)
    xs, gc = _moe_gather(slot, aff, h3, b, t, cap, bq=bg, eg=4 if cap >= MXU_DIM else N_EXPERTS)
    ys = _moe_ffn(xs, gc, w_gate, w_up, w_down, layer).reshape(N_EXPERTS, b, cap, D)
    if cap >= MXU_DIM:
        return _moe_combine_windowed(slot, ys, x, mod3, mod_base, per_batch_mod, b, t, cap)
    return _moe_combine(jnp.swapaxes(slot, 1, 2), ys, x, mod3, mod_base, per_batch_mod, b, t, cap)


def kernel(x_prompt, x_sample, cache_a_k, cache_a_v, cache_b_k, cache_b_v, state_c_h, c, c_ctx, ada_w, ada_b, norm_mix_g, norm_ffn_g, a_w_in, a_q_norm_g, a_k_norm_g, a_lam_q, a_lam_k, a_subln_g, a_w_out, b_w_in, b_q_norm_g, b_k_norm_g, b_sink, b_w_out, c_w_in, c_conv_w, c_conv_b, c_w_rg, c_b_rg, c_w_ig, c_b_ig, c_lam, c_w_out, moe_w_router, moe_w_gate, moe_w_up, moe_w_down):
    bp, tp, _ = x_prompt.shape
    bs_, ts, _ = x_sample.shape
    past = cache_a_k.shape[2]
    cvec = jnp.concatenate([c, c_ctx[None, :], jnp.zeros((16 - bs_ - 1, D), F32)], axis=0)
    mods = _modulation(cvec, ada_w, ada_b)
    groups = {"p": (bp, tp, bs_, 0), "s": (bs_, ts, 0, 1)}
    xs = {"p": x_prompt.reshape(bp * tp, D), "s": x_sample.reshape(bs_ * ts, D)}
    ck_a = cache_a_k.reshape(bs_, -1, past, H_A * 2 * HD_A)
    cv_a = cache_a_v.reshape(bs_, -1, past, H_A * 2 * HD_A)
    ck_b = cache_b_k.reshape(bs_, -1, past, KV_B * HD_B)
    cv_b = cache_b_v.reshape(bs_, -1, past, KV_B * HD_B)
    moe_w = (moe_w_router, moe_w_gate, moe_w_up, moe_w_down)
    new_a_k, new_a_v, new_b_k, new_b_v, new_c_h = [], [], [], [], []
    ia = ib = ic = 0
    for l in range(DEPTH):
        mod3 = mods[l].reshape(16, 1, 6 * D)
        kind = l % 3
        for key in ("p", "s"):
            b, t, mbase, per_b = groups[key]
            rows_per_mod = t if per_b else b * t
            x = xs[key]
            sample = key == "s"
            if kind == 0:
                lam_init = 0.8 - 0.6 * math.exp(-0.3 * l)
                outs = _qkv_project(x, mod3, mbase, rows_per_mod, norm_mix_g[l], a_w_in[ia], a_q_norm_g[ia],
                                    a_k_norm_g[ia], D, D, D, t, rope=sample, kv_f32=not sample)
                q, k, v = outs[:3]
                if not sample:
                    new_a_k.append(outs[3].reshape(b, t, H_A, 2 * HD_A))
                    new_a_v.append(outs[4].reshape(b, t, H_A, 2 * HD_A))
                o = _diff_attention(q, k, v, (ck_a, cv_a, ia) if sample else None, a_lam_q[ia], a_lam_k[ia],
                                    a_subln_g[ia], lam_init, b, t, heads=1 if sample else H_A,
                                    tq=2048 if sample else t, nsub=16 if sample else 1)
                x, h2 = _proj_residual(o, x, mod3, mbase, rows_per_mod, norm_ffn_g[l], a_w_out[ia])
            elif kind == 1:
                nq, nk = H_B * HD_B, KV_B * HD_B
                outs = _qkv_project(x, mod3, mbase, rows_per_mod, norm_mix_g[l], b_w_in[ib], b_q_norm_g[ib],
                                    b_k_norm_g[ib], nq, nk, nk, t, rope=sample, kv_f32=not sample)
                q, k, v = outs[:3]
                if not sample:
                    new_b_k.append(outs[3].reshape(b, t, KV_B, HD_B))
                    new_b_v.append(outs[4].reshape(b, t, KV_B, HD_B))
                o = _gqa_attention(q, k, v, (ck_b, cv_b, ib) if sample else None, b_sink[ib], b, t,
                                   windowed=sample, tq=256)
                x, h2 = _proj_residual(o, x, mod3, mbase, rows_per_mod, norm_ffn_g[l], b_w_out[ib])
            else:
                gx = _norm_mod_matmul(x, mod3, mbase, rows_per_mod, norm_mix_g[l], c_w_in[ic])
                state = state_c_h[:, ic] if sample else jnp.zeros((b, 2, D_RNN), F32)
                x, h2, finals = _lru_mixer(gx, x, state, c_conv_w[ic], c_conv_b[ic], c_w_rg[ic], c_b_rg[ic],
                                           c_w_ig[ic], c_b_ig[ic], c_lam[ic], c_w_out[ic], mod3, mbase, per_b,
                                           norm_ffn_g[l], b, t)
                if not sample:
                    new_c_h.append(finals)
            xs[key] = _ec_moe(x, h2, mod3, mbase, per_b, moe_w, l, b, t,
                              bs=2 if sample else 8, bg=1 if sample else 8)
        ia, ib, ic = ia + (kind == 0), ib + (kind == 1), ic + (kind == 2)
    return (xs["p"].reshape(bp, tp, D), xs["s"].reshape(bs_, ts, D),
            jnp.stack(new_a_k, axis=1), jnp.stack(new_a_v, axis=1),
            jnp.stack(new_b_k, axis=1), jnp.stack(new_b_v, axis=1), jnp.stack(new_c_h, axis=1))
```

```python
import functools
import math

import jax
import jax.numpy as jnp
from jax import lax
from jax.experimental import pallas as pl
from jax.experimental.pallas import tpu as pltpu

F32, BF16, I32 = jnp.float32, jnp.bfloat16, jnp.int32

D = 1024
DEPTH = 4
GRID_W = 64
H_A, HD_A = 8, 64
H_B, KV_B, G_B, HD_B = 16, 4, 4, 64
WINDOW = 128
D_RNN = 1024
N_BLK_C, BW_C = 16, 64
LRU_C = 8.0
N_EXPERTS = 16
EC_FACTOR = 2
ROPE_THETA = 10000.0
EPS = 1e-6
LOG2E = math.log2(math.e)
LANES = 128
MXU_DIM = 256
MIB = 1024 * 1024


def _cparams(n_axes, vmem_mib=48):
    return pltpu.CompilerParams(dimension_semantics=("arbitrary",) * n_axes,
                                vmem_limit_bytes=vmem_mib * MIB)


def _sigmoid(x):
    return 0.5 * jnp.tanh(0.5 * x) + 0.5


def _rms(x):
    return x * lax.rsqrt(jnp.mean(x * x, axis=-1, keepdims=True) + EPS)


def _dot(a, b):
    return jnp.dot(a, b, preferred_element_type=F32)


def _dot_nt(a, b):
    return lax.dot_general(a, b, (((1,), (1,)), ((), ())), preferred_element_type=F32)


def _mod_kernel(c_ref, w_ref, b_ref, o_ref):
    c = c_ref[...]
    s = (c * _sigmoid(c)).astype(BF16)
    o_ref[...] = _dot(s, w_ref[...].astype(BF16)) + b_ref[...]


def _modulation(cvec, ada_w, ada_b):
    nt = 1536
    return pl.pallas_call(
        _mod_kernel,
        grid=(DEPTH, 6 * D // nt),
        in_specs=[pl.BlockSpec((16, D), lambda l, j: (0, 0)),
                  pl.BlockSpec((None, D, nt), lambda l, j: (l, 0, j)),
                  pl.BlockSpec((None, 1, nt), lambda l, j: (l, 0, j))],
        out_specs=pl.BlockSpec((None, 16, nt), lambda l, j: (l, 0, j)),
        out_shape=jax.ShapeDtypeStruct((DEPTH, 16, 6 * D), F32),
        compiler_params=_cparams(2, 32),
        name="adaln_mod",
    )(cvec, ada_w, ada_b.reshape(DEPTH, 1, 6 * D))


def _nmm_kernel(x_ref, mod_ref, g_ref, w_ref, o_ref, wbf_ref):
    @pl.when(pl.program_id(0) == 0)
    def _():
        wbf_ref[...] = w_ref[...].astype(BF16)

    m = mod_ref[...]
    h = (_rms(x_ref[...]) * g_ref[...] * (1.0 + m[:, D:2 * D]) + m[:, 0:D]).astype(BF16)
    o_ref[...] = _dot(h, wbf_ref[...])


def _norm_mod_matmul(x, mod3, mod_base, rows_per_mod, g, w, tm=512):
    r, n = x.shape[0], w.shape[1]
    return pl.pallas_call(
        _nmm_kernel,
        grid=(r // tm,),
        in_specs=[pl.BlockSpec((tm, D), lambda i: (i, 0)),
                  pl.BlockSpec((None, 1, 6 * D), lambda i: (mod_base + (i * tm) // rows_per_mod, 0, 0)),
                  pl.BlockSpec((1, D), lambda i: (0, 0)),
                  pl.BlockSpec((D, n), lambda i: (0, 0), pipeline_mode=pl.Buffered(1))],
        out_specs=pl.BlockSpec((tm, n), lambda i: (i, 0)),
        out_shape=jax.ShapeDtypeStruct((r, n), F32),
        scratch_shapes=[pltpu.VMEM((D, n), BF16)],
        compiler_params=_cparams(1, 48),
        name="norm_mod_matmul",
    )(x, mod3, g.reshape(1, D), w)


def _proj_tail(y, x_ref, mod_ref, g2_ref, xo_ref, h2_ref):
    m = mod_ref[...]
    xn = x_ref[...] + m[:, 2 * D:3 * D] * y
    xo_ref[...] = xn
    h2_ref[...] = (_rms(xn) * g2_ref[...] * (1.0 + m[:, 4 * D:5 * D]) + m[:, 3 * D:4 * D]).astype(BF16)


def _proj_kernel(o_ref, x_ref, mod_ref, g2_ref, w_ref, xo_ref, h2_ref, wbf_ref):
    @pl.when(pl.program_id(0) == 0)
    def _():
        wbf_ref[...] = w_ref[...].astype(BF16)

    _proj_tail(_dot(o_ref[...], wbf_ref[...]), x_ref, mod_ref, g2_ref, xo_ref, h2_ref)


def _proj_residual(o, x, mod3, mod_base, rows_per_mod, g2, w, tm=512):
    r = x.shape[0]
    return pl.pallas_call(
        _proj_kernel,
        grid=(r // tm,),
        in_specs=[pl.BlockSpec((tm, D), lambda i: (i, 0)),
                  pl.BlockSpec((tm, D), lambda i: (i, 0)),
                  pl.BlockSpec((None, 1, 6 * D), lambda i: (mod_base + (i * tm) // rows_per_mod, 0, 0)),
                  pl.BlockSpec((1, D), lambda i: (0, 0)),
                  pl.BlockSpec((D, D), lambda i: (0, 0))],
        out_specs=[pl.BlockSpec((tm, D), lambda i: (i, 0)), pl.BlockSpec((tm, D), lambda i: (i, 0))],
        out_shape=[jax.ShapeDtypeStruct((r, D), F32), jax.ShapeDtypeStruct((r, D), BF16)],
        scratch_shapes=[pltpu.VMEM((D, D), BF16)],
        compiler_params=_cparams(1, 40),
        name="proj_residual",
    )(o, x, mod3, g2.reshape(1, D), w)


def _group_inv_rms(x, on_mxu):
    ss = x * x
    if not on_mxu:
        low = lax.broadcasted_iota(I32, (1, LANES), 1) < HD_A
        s_lo = jnp.sum(jnp.where(low, ss, 0.0), axis=-1, keepdims=True)
        s_hi = jnp.sum(jnp.where(low, 0.0, ss), axis=-1, keepdims=True)
        return jnp.where(low, lax.rsqrt(s_lo * (1.0 / HD_A) + EPS), lax.rsqrt(s_hi * (1.0 / HD_A) + EPS))
    hi = ss.astype(BF16)
    lo = (ss - hi.astype(F32)).astype(BF16)
    r = lax.broadcasted_iota(I32, (LANES, LANES), 0) >> 6
    c = lax.broadcasted_iota(I32, (LANES, LANES), 1) >> 6
    ones_bd = jnp.where(r == c, 1.0, 0.0).astype(BF16)
    return lax.rsqrt((_dot(hi, ones_bd) + _dot(lo, ones_bd)) * (1.0 / HD_A) + EPS)


def _rope128(y, cos, sa, sb):
    return y * cos + pltpu.roll(y, LANES - 16, 1) * sa + pltpu.roll(y, 16, 1) * sb


def _qkv_kernel(*refs, nq, nk, rope, kv_f32, nsub):
    it = iter(refs)
    x_ref, mod_ref, g_ref, w_ref, qg_ref, kg_ref = (next(it) for _ in range(6))
    if rope:
        cos_ref, sa_ref, sb_ref = next(it), next(it), next(it)
    q_ref, k_ref, v_ref = next(it), next(it), next(it)
    kf_ref, vf_ref = (next(it), next(it)) if kv_f32 else (None, None)
    wbf_ref, qkv_ref = next(it), next(it)

    @pl.when(pl.program_id(0) == 0)
    def _():
        wbf_ref[...] = w_ref[...].astype(BF16)

    m = mod_ref[...]
    tm = x_ref.shape[0]
    parts = [slice(i * (tm // nsub), (i + 1) * (tm // nsub)) for i in range(nsub)]
    for rows in parts:
        h = (_rms(x_ref[rows, :]) * g_ref[...] * (1.0 + m[:, D:2 * D]) + m[:, 0:D]).astype(BF16)
        qkv_ref[rows, :] = _dot(h, wbf_ref[...])
    q_gain = qg_ref[...] * (HD_A ** -0.5 * LOG2E)
    for rows in parts:
        if rope:
            cos, sa, sb = cos_ref[rows, :], sa_ref[rows, :], sb_ref[rows, :]
        for j in range((nq + nk) // LANES):
            x = qkv_ref[rows, j * LANES:(j + 1) * LANES]
            is_q = j < nq // LANES
            y = x * _group_inv_rms(x, on_mxu=rope) * (q_gain if is_q else kg_ref[...])
            if rope:
                y = _rope128(y, cos, sa, sb)
            if is_q:
                q_ref[rows, j * LANES:(j + 1) * LANES] = y.astype(BF16)
            else:
                jj = j - nq // LANES
                k_ref[rows, jj * LANES:(jj + 1) * LANES] = y.astype(BF16)
                if kv_f32:
                    kf_ref[rows, jj * LANES:(jj + 1) * LANES] = y
        v = qkv_ref[rows, nq + nk:]
        v_ref[rows, :] = v.astype(BF16)
        if kv_f32:
            vf_ref[rows, :] = v


def _rope_tables(t):
    n_freq = HD_A // 4
    inv_freq = ROPE_THETA ** (-jnp.arange(n_freq, dtype=F32) / n_freq)
    pos_row = jnp.repeat(jnp.arange(t // GRID_W), GRID_W).astype(F32)
    pos_col = jnp.tile(jnp.arange(GRID_W), t // GRID_W).astype(F32)
    ang_r = pos_row[:, None] * inv_freq[None, :]
    ang_c = pos_col[:, None] * inv_freq[None, :]
    z = jnp.zeros_like(ang_r)
    cos64 = jnp.concatenate([jnp.cos(ang_r), jnp.cos(ang_r), jnp.cos(ang_c), jnp.cos(ang_c)], axis=-1)
    sa64 = jnp.concatenate([-jnp.sin(ang_r), z, -jnp.sin(ang_c), z], axis=-1)
    sb64 = jnp.concatenate([z, jnp.sin(ang_r), z, jnp.sin(ang_c)], axis=-1)
    return tuple(jnp.tile(a, (1, 2)) for a in (cos64, sa64, sb64))


def _qkv_project(x, mod3, mod_base, rows_per_mod, g, w, qg, kg, nq, nk, nv, t, rope, kv_f32, tm=512):
    r = x.shape[0]
    n = nq + nk + nv
    tile2 = lambda a: jnp.tile(a.reshape(1, HD_A), (1, 2))
    row = lambda width: pl.BlockSpec((tm, width), lambda i: (i, 0))
    args = [x, mod3, g.reshape(1, D), w, tile2(qg), tile2(kg)]
    in_specs = [row(D),
                pl.BlockSpec((None, 1, 6 * D), lambda i: (mod_base + (i * tm) // rows_per_mod, 0, 0)),
                pl.BlockSpec((1, D), lambda i: (0, 0)),
                pl.BlockSpec((D, n), lambda i: (0, 0), pipeline_mode=pl.Buffered(1)),
                pl.BlockSpec((1, LANES), lambda i: (0, 0)),
                pl.BlockSpec((1, LANES), lambda i: (0, 0))]
    if rope:
        args += list(_rope_tables(t))
        in_specs += [pl.BlockSpec((tm, LANES), lambda i: (i % (t // tm), 0))] * 3
    out_shape = [jax.ShapeDtypeStruct((r, nq), BF16), jax.ShapeDtypeStruct((r, nk), BF16),
                 jax.ShapeDtypeStruct((r, nv), BF16)]
    out_specs = [row(nq), row(nk), row(nv)]
    if kv_f32:
        out_shape += [jax.ShapeDtypeStruct((r, nk), F32), jax.ShapeDtypeStruct((r, nv), F32)]
        out_specs += [row(nk), row(nv)]
    return pl.pallas_call(
        functools.partial(_qkv_kernel, nq=nq, nk=nk, rope=rope, kv_f32=kv_f32, nsub=2),
        grid=(r // tm,),
        in_specs=in_specs, out_specs=out_specs, out_shape=out_shape,
        scratch_shapes=[pltpu.VMEM((D, n), BF16), pltpu.VMEM((tm, n), F32)],
        compiler_params=_cparams(1, 48),
        name="qkv_project",
    )(*args)


def _diff_attn_kernel(*refs, heads, ctx, lam_init, nsub):
    it = iter(refs)
    q_ref, k_ref, v_ref = next(it), next(it), next(it)
    if ctx:
        ck_ref, cv_ref = next(it), next(it)
    lq_ref, lk_ref, sg_ref, o_ref = next(it), next(it), next(it), next(it)
    e = jnp.exp(jnp.sum(lq_ref[...] * lk_ref[...], axis=-1, keepdims=True))
    lam = e[0:1, :] - e[1:2, :] + lam_init
    lane = lax.broadcasted_iota(I32, (1, LANES), 1)
    tqs = q_ref.shape[0] // nsub
    units = [(slice(h * LANES, (h + 1) * LANES), slice(j * tqs, (j + 1) * tqs))
             for h in range(heads) for j in range(nsub)]

    def scores(u):
        sl, rows = units[u]
        q = q_ref[rows, sl]
        out = []
        for c in range(2):
            qc = jnp.where((lane < HD_A) if c == 0 else (lane >= HD_A), q, jnp.zeros_like(q))
            out.append((_dot_nt(qc, k_ref[:, sl]), _dot_nt(qc, ck_ref[:, sl].astype(BF16)) if ctx else None))
        return out

    def weights(sc2):
        ps, ls = [], []
        for s, sc in sc2:
            m = jnp.max(s, axis=-1, keepdims=True)
            if ctx:
                m = jnp.maximum(m, jnp.max(sc, axis=-1, keepdims=True))
            p = jnp.exp2(s - m)
            l = jnp.sum(p, axis=-1, keepdims=True)
            pc = None
            if ctx:
                pc = jnp.exp2(sc - m)
                l = l + jnp.sum(pc, axis=-1, keepdims=True)
            ps.append((p, pc))
            ls.append(l)
        ratio = lam * ls[0] / ls[1]
        w_lat = (ps[0][0] - ratio * ps[1][0]).astype(BF16)
        w_ctx = (ps[0][1] - ratio * ps[1][1]).astype(BF16) if ctx else None
        return w_lat, w_ctx, 1.0 / ls[0]

    def values(u, w):
        sl, rows = units[u]
        w_lat, w_ctx, inv_l0 = w
        o = _dot(w_lat, v_ref[:, sl])
        if ctx:
            o = o + _dot(w_ctx, cv_ref[:, sl].astype(BF16))
        o_ref[rows, sl] = (_rms(o * inv_l0) * sg_ref[...] * (1.0 - lam_init)).astype(BF16)

    if not ctx:
        s_ref = next(it)
        tk = k_ref.shape[0]
        for u in range(len(units)):
            for c, (s, _) in enumerate(scores(u)):
                s_ref[(2 * u + c) * tqs:(2 * u + c + 1) * tqs, :] = s
        s = s_ref[...]
        p = jnp.exp2(s - jnp.max(s, axis=-1, keepdims=True))
        l = jnp.sum(p, axis=-1, keepdims=True)
        for u in range(len(units)):
            r0, r1 = slice(2 * u * tqs, (2 * u + 1) * tqs), slice((2 * u + 1) * tqs, (2 * u + 2) * tqs)
            w = (p[r0, :] - (lam * l[r0, :] / l[r1, :]) * p[r1, :]).astype(BF16)
            values(u, (w, None, 1.0 / l[r0, :]))
        return
    n = len(units)
    sc = {u: scores(u) for u in range(min(2, n))}
    for u in range(n):
        w = weights(sc.pop(u))
        if u + 2 < n:
            sc[u + 2] = scores(u + 2)
        values(u, w)


def _diff_attention(q, k, v, cache, lam_q, lam_k, subln_g, lam_init, b, t, heads, tq, nsub):
    nq = t // tq
    hb = H_A // heads
    args = [q, k, v]
    in_specs = [pl.BlockSpec((tq, heads * LANES), lambda bi, hi, qi: (bi * nq + qi, hi)),
                pl.BlockSpec((t, heads * LANES), lambda bi, hi, qi: (bi, hi)),
                pl.BlockSpec((t, heads * LANES), lambda bi, hi, qi: (bi, hi))]
    if cache is not None:
        ck, cv, la = cache
        p = ck.shape[2]
        args += [ck, cv]
        in_specs += [pl.BlockSpec((None, None, p, heads * LANES), lambda bi, hi, qi: (bi, la, 0, hi))] * 2
    args += [lam_q, lam_k, subln_g.reshape(1, LANES)]
    in_specs += [pl.BlockSpec((2, HD_A), lambda bi, hi, qi: (0, 0)),
                 pl.BlockSpec((2, HD_A), lambda bi, hi, qi: (0, 0)),
                 pl.BlockSpec((1, LANES), lambda bi, hi, qi: (0, 0))]
    return pl.pallas_call(
        functools.partial(_diff_attn_kernel, heads=heads, ctx=cache is not None, lam_init=lam_init, nsub=nsub),
        grid=(b, hb, nq),
        in_specs=in_specs,
        out_specs=pl.BlockSpec((tq, heads * LANES), lambda bi, hi, qi: (bi * nq + qi, hi)),
        out_shape=jax.ShapeDtypeStruct((b * t, H_A * LANES), BF16),
        scratch_shapes=[] if cache is not None else [pltpu.VMEM((2 * heads * tq, t), F32)],
        compiler_params=_cparams(3, 48),
        name="diff_attention",
    )(*args)


def _both_halves(x, half):
    lane_half = lax.broadcasted_iota(I32, (1, LANES), 1) >> 6
    xm = jnp.where(lane_half == half, x, 0.0)
    return xm + pltpu.roll(xm, HD_B, 1)


def _value_with_ones(x, half):
    lane = lax.broadcasted_iota(I32, (1, LANES), 1)
    return jnp.where(lane < HD_B, _both_halves(x, half), jnp.where(lane == HD_B, 1.0, 0.0))


def _gqa_dense_body(q_ref, k_ref, v_ref, sink_ref, o_ref, s_ref, t):
    lane_half = lax.broadcasted_iota(I32, (1, LANES), 1) >> 6
    vds, sinks = [], []
    for kv in range(KV_B):
        kcol = slice((kv // 2) * LANES, (kv // 2 + 1) * LANES)
        kd = _both_halves(k_ref[:, kcol].astype(F32), kv % 2).astype(BF16)
        vds.append(_value_with_ones(v_ref[:, kcol].astype(F32), kv % 2).astype(BF16))
        for g in range(G_B):
            h = kv * G_B + g
            q = q_ref[:, (h // 2) * LANES:(h // 2 + 1) * LANES]
            qm = jnp.where(lane_half == h % 2, q, jnp.zeros_like(q))
            s_ref[h * t:(h + 1) * t, :] = _dot_nt(qm, kd)
            sinks.append(jnp.broadcast_to(sink_ref[kv:kv + 1, g:g + 1] * LOG2E, (t, 1)))
    s = s_ref[...]
    sink2 = jnp.concatenate(sinks, axis=0)
    m = jnp.maximum(jnp.max(s, axis=-1, keepdims=True), sink2)
    p = jnp.exp2(s - m).astype(BF16)
    tail = jnp.exp2(sink2 - m)
    for hp in range(H_B // 2):
        halves = []
        for h in (2 * hp, 2 * hp + 1):
            rows = slice(h * t, (h + 1) * t)
            ov = _dot(p[rows, :], vds[h // G_B])
            halves.append(ov * (1.0 / (ov[:, HD_B:HD_B + 1] + tail[rows, :])))
        o_ref[:, hp * LANES:(hp + 1) * LANES] = jnp.where(
            lane_half == 0, halves[0], pltpu.roll(halves[1], HD_B, 1)).astype(BF16)


def _gqa_kernel(*refs, kvs, windowed, ctx, t, tq):
    it = iter(refs)
    q_ref, k_ref, v_ref = next(it), next(it), next(it)
    if ctx:
        ck_ref, cv_ref = next(it), next(it)
    sink_ref, o_ref = next(it), next(it)
    lane_half = lax.broadcasted_iota(I32, (1, LANES), 1) >> 6
    if not windowed:
        _gqa_dense_body(q_ref, k_ref, v_ref, sink_ref, o_ref, next(it), t)
        return
    kd_ref, vd_ref, ckd_ref, cvd_ref = next(it), next(it), next(it), next(it)

    @pl.when(pl.program_id(2) == 0)
    def _():
        for half in range(2):
            kd_ref[half] = _both_halves(k_ref[...].astype(F32), half).astype(BF16)
            vd_ref[half] = _value_with_ones(v_ref[...].astype(F32), half).astype(BF16)
            ckd_ref[half] = _both_halves(ck_ref[...], half).astype(BF16)
            cvd_ref[half] = _value_with_ones(cv_ref[...], half).astype(BF16)

    span = tq + 2 * WINDOW
    q0 = pl.program_id(2) * tq
    start = pl.multiple_of(jnp.clip(q0 - WINDOW, 0, t - span), LANES)
    qpos = q0 + lax.broadcasted_iota(I32, (tq, span), 0)
    kpos = start + lax.broadcasted_iota(I32, (tq, span), 1)
    valid = jnp.abs(qpos - kpos) <= WINDOW
    s_ref = next(it)
    sinks = []
    for half in range(2):
        kd, ckd = kd_ref[half, pl.ds(start, span), :], ckd_ref[half]
        for g in range(G_B):
            hl = half * G_B + g
            q = q_ref[:, hl // 2 * LANES:(hl // 2 + 1) * LANES]
            qm = jnp.where(lane_half == hl % 2, q, jnp.zeros_like(q))
            s_ref[hl * tq:(hl + 1) * tq, 0:span] = jnp.where(valid, _dot_nt(qm, kd), -jnp.inf)
            s_ref[hl * tq:(hl + 1) * tq, span:] = _dot_nt(qm, ckd)
            sinks.append(jnp.broadcast_to(sink_ref[half:half + 1, g:g + 1] * LOG2E, (tq, 1)))
    s = s_ref[...]
    sink2 = jnp.concatenate(sinks, axis=0)
    m = jnp.maximum(jnp.max(s, axis=-1, keepdims=True), sink2)
    p = jnp.exp2(s - m).astype(BF16)
    tail = jnp.exp2(sink2 - m)
    for hp in range(G_B):
        halves = []
        for hl in (2 * hp, 2 * hp + 1):
            rows, half = slice(hl * tq, (hl + 1) * tq), hl // G_B
            ov = (_dot(p[rows, 0:span], vd_ref[half, pl.ds(start, span), :]) + _dot(p[rows, span:], cvd_ref[half]))
            halves.append(ov * (1.0 / (ov[:, HD_B:HD_B + 1] + tail[rows, :])))
        o_ref[:, hp * LANES:(hp + 1) * LANES] = jnp.where(
            lane_half == 0, halves[0], pltpu.roll(halves[1], HD_B, 1)).astype(BF16)


def _gqa_attention(q, k, v, cache, sink, b, t, windowed, tq):
    if windowed:
        kvs, nq = 2, t // tq
        grid = (b, KV_B // 2, nq)
        qw = 2 * G_B * HD_B
        q_spec = pl.BlockSpec((tq, qw), lambda bi, kv, qi: (bi * nq + qi, kv))
        kv_spec = pl.BlockSpec((t, LANES), lambda bi, kv, qi: (bi, kv))
        sink_arr = sink.reshape(KV_B // 2, 2, G_B)
        sink_spec = pl.BlockSpec((None, 2, G_B), lambda bi, kv, qi: (kv, 0, 0))
        o_spec = pl.BlockSpec((tq, qw), lambda bi, kv, qi: (bi * nq + qi, kv))
    else:
        kvs, tq = KV_B, t
        grid = (b, 1, 1)
        q_spec = pl.BlockSpec((t, H_B * HD_B), lambda bi, kv, qi: (bi, 0))
        kv_spec = pl.BlockSpec((t, KV_B * HD_B), lambda bi, kv, qi: (bi, 0))
        sink_arr = sink.reshape(KV_B, G_B)
        sink_spec = pl.BlockSpec((KV_B, G_B), lambda bi, kv, qi: (0, 0))
        o_spec = pl.BlockSpec((t, H_B * HD_B), lambda bi, kv, qi: (bi, 0))
    args, in_specs = [q, k, v], [q_spec, kv_spec, kv_spec]
    if cache is not None:
        ck, cv, lb = cache
        p = ck.shape[2]
        args += [ck, cv]
        in_specs += [pl.BlockSpec((None, None, p, LANES), lambda bi, kv, qi: (bi, lb, 0, kv))] * 2
    args.append(sink_arr)
    in_specs.append(sink_spec)
    scratch = [pltpu.VMEM((H_B * t, t), F32)]
    if windowed:
        scratch = ([pltpu.VMEM((2, t, LANES), BF16)] * 2 + [pltpu.VMEM((2, p, LANES), BF16)] * 2
                   + [pltpu.VMEM((2 * G_B * tq, tq + 2 * WINDOW + p), F32)])
    return pl.pallas_call(
        functools.partial(_gqa_kernel, kvs=kvs, windowed=windowed, ctx=cache is not None, t=t, tq=tq),
        grid=grid, in_specs=in_specs, out_specs=o_spec,
        out_shape=jax.ShapeDtypeStruct((b * t, H_B * HD_B), BF16),
        scratch_shapes=scratch,
        compiler_params=_cparams(3, 48),
        name="gqa_attention",
    )(*args)


def _lru_conv_gates(xp_ref, xc_ref, xn_ref, cw_ref, cb_ref, wg_ref, bg_ref, lam_ref, a_ref, u_ref, ti, nt, tt):
    nl = -lam_ref[...]
    sp = jnp.maximum(nl, 0.0) + jnp.log1p(jnp.exp(-jnp.abs(nl)))
    cur = xc_ref[...]
    prev = xp_ref[...] * (ti > 0).astype(F32)
    nxt = xn_ref[...] * (ti < nt - 1).astype(F32)
    row = lax.broadcasted_iota(I32, (tt, 1), 0)
    xm1 = jnp.where(row == 0, prev[7:8, :], pltpu.roll(cur, 1, 0))
    xm2 = jnp.where(row == 0, prev[6:7, :], jnp.where(row == 1, prev[7:8, :], pltpu.roll(cur, 2, 0)))
    xp1 = jnp.where(row == tt - 1, nxt[0:1, :], pltpu.roll(cur, tt - 1, 0))
    cw = cw_ref[...]
    xc = xm2 * cw[0:1, :] + xm1 * cw[1:2, :] + cur * cw[2:3, :] + xp1 * cw[3:4, :] + cb_ref[...]
    xcb = xc.astype(BF16)
    for j in range(D_RNN // MXU_DIM):
        cs = slice(j * MXU_DIM, (j + 1) * MXU_DIM)
        z = _dot(xcb[:, cs], wg_ref[j].astype(BF16))
        r = _sigmoid(z[:, :MXU_DIM] + bg_ref[0:1, cs])
        i = _sigmoid(z[:, MXU_DIM:] + bg_ref[1:2, cs])
        a = jnp.exp(-LRU_C * r * sp[:, cs])
        a_ref[:, cs] = a
        u_ref[:, cs] = jnp.sqrt(1.0 - a * a) * i * xc[:, cs]


def _tile_scan(a_ref, u_ref, hs_ref, h, tt, reverse):
    row = lax.broadcasted_iota(I32, (8, 1), 0)
    ng = tt // 8
    for g in (range(ng - 1, -1, -1) if reverse else range(ng)):
        rows = slice(g * 8, (g + 1) * 8)
        a8, u8 = a_ref[rows, :], u_ref[rows, :]
        for s in (1, 2, 4):
            keep = (row < 8 - s) if reverse else (row >= s)
            shift = 8 - s if reverse else s
            u8 = a8 * jnp.where(keep, pltpu.roll(u8, shift, 0), 0.0) + u8
            a8 = a8 * jnp.where(keep, pltpu.roll(a8, shift, 0), 1.0)
        h8 = a8 * h + u8
        hs_ref[rows, :] = h8
        h = h8[0:1, :] if reverse else h8[7:8, :]
    return h


def _lru_fwd_kernel(xp_ref, xc_ref, xn_ref, cw_ref, cb_ref, wg_ref, bg_ref, lam_ref, h0_ref,
                    hf_ref, last_ref, carry_ref, a_ref, u_ref, *, tt):
    ti, nt = pl.program_id(1), pl.num_programs(1)

    @pl.when(ti == 0)
    def _():
        carry_ref[...] = h0_ref[...]

    _lru_conv_gates(xp_ref, xc_ref, xn_ref, cw_ref, cb_ref, wg_ref, bg_ref, lam_ref, a_ref, u_ref, ti, nt, tt)
    h = _tile_scan(a_ref, u_ref, hf_ref, carry_ref[...], tt, reverse=False)
    carry_ref[...] = h
    last_ref[...] = h


def _lru_bwd_kernel(xp_ref, xc_ref, xn_ref, cw_ref, cb_ref, wg_ref, bg_ref, lam_ref, h0_ref,
                    gate_ref, hf_ref, x_ref, mod_ref, g2_ref, w_ref,
                    xo_ref, h2_ref, last_ref, carry_ref, a_ref, u_ref, hb_ref, wbf_ref, *, tt):
    ti, nt = pl.program_id(1), pl.num_programs(1)

    @pl.when((pl.program_id(0) == 0) & (ti == 0))
    def _():
        wbf_ref[...] = w_ref[...].astype(BF16)

    @pl.when(ti == 0)
    def _():
        carry_ref[...] = h0_ref[...]

    tr = nt - 1 - ti
    _lru_conv_gates(xp_ref, xc_ref, xn_ref, cw_ref, cb_ref, wg_ref, bg_ref, lam_ref, a_ref, u_ref, tr, nt, tt)
    h = _tile_scan(a_ref, u_ref, hb_ref, carry_ref[...], tt, reverse=True)
    carry_ref[...] = h
    last_ref[...] = h
    g = gate_ref[...]
    gelu = 0.5 * g * (1.0 + jnp.tanh(math.sqrt(2.0 / math.pi) * (g + 0.044715 * (g * g * g))))
    y = (gelu * (hf_ref[...] + hb_ref[...])).astype(BF16)
    _proj_tail(_dot(y, wbf_ref[...]), x_ref, mod_ref, g2_ref, xo_ref, h2_ref)


def _lru_mixer(gx, x, state, conv_w, conv_b, w_rg, b_rg, w_ig, b_ig, lam, w_out, mod3, mod_base, per_batch_mod,
               g2, b, t, tt=256):
    nt = t // tt
    per_tile = MXU_DIM // BW_C
    eye = jnp.eye(per_tile, dtype=F32)

    def bd(w):
        w4 = w.reshape(N_BLK_C // per_tile, per_tile, BW_C, BW_C)
        return (w4[:, :, :, None, :] * eye[None, :, None, :, None]).reshape(-1, MXU_DIM, MXU_DIM)

    full = lambda shape: pl.BlockSpec(shape, lambda bi, ti: (0,) * len(shape))
    finals, xo, h2, hf = [], None, None, None
    for d in range(2):
        tile = (lambda ti: ti) if d == 0 else (lambda ti: nt - 1 - ti)
        x_spec = lambda f: pl.BlockSpec((tt, D_RNN), f)
        row = lambda bi, ti, tile=tile: (bi * nt + tile(ti), 0)
        halo = lambda f: pl.BlockSpec((8, D_RNN), f)
        in_specs = [halo(lambda bi, ti, tile=tile: (jnp.maximum((bi * nt + tile(ti)) * (tt // 8) - 1, 0), 1)),
                    x_spec(lambda bi, ti, tile=tile: (bi * nt + tile(ti), 1)),
                    halo(lambda bi, ti, tile=tile: (jnp.minimum((bi * nt + tile(ti) + 1) * (tt // 8), b * t // 8 - 1), 1)),
                    full((4, D_RNN)), full((1, D_RNN)), full((4, MXU_DIM, 2 * MXU_DIM)),
                    full((2, D_RNN)), full((1, D_RNN)),
                    pl.BlockSpec((None, 1, D_RNN), lambda bi, ti: (bi, 0, 0))]
        args = [gx, gx, gx, conv_w, conv_b.reshape(1, D_RNN),
                jnp.concatenate([bd(w_rg[d]), bd(w_ig[d])], axis=-1), jnp.stack([b_rg[d], b_ig[d]]),
                lam[d].reshape(1, D_RNN), state[:, d].reshape(b, 1, D_RNN)]
        scratch = [pltpu.VMEM((1, D_RNN), F32), pltpu.VMEM((tt, D_RNN), F32), pltpu.VMEM((tt, D_RNN), F32)]
        last_spec = pl.BlockSpec((None, 1, D_RNN), lambda bi, ti: (bi, 0, 0))
        last_shape = jax.ShapeDtypeStruct((b, 1, D_RNN), F32)
        if d == 0:
            hf, last_f = pl.pallas_call(
                functools.partial(_lru_fwd_kernel, tt=tt),
                grid=(b, nt), in_specs=in_specs,
                out_specs=[pl.BlockSpec((tt, D_RNN), row), last_spec],
                out_shape=[jax.ShapeDtypeStruct((b * t, D_RNN), F32), last_shape],
                scratch_shapes=scratch,
                compiler_params=_cparams(2, 48),
                name="lru_fwd",
            )(*args)
        else:
            in_specs += [pl.BlockSpec((tt, D_RNN), row), pl.BlockSpec((tt, D_RNN), row), pl.BlockSpec((tt, D), row),
                         pl.BlockSpec((None, 1, 6 * D), lambda bi, ti: (mod_base + bi * per_batch_mod, 0, 0)),
                         full((1, D)), full((D_RNN, D))]
            args += [gx, hf, x, mod3, g2.reshape(1, D), w_out]
            xo, h2, last_b = pl.pallas_call(
                functools.partial(_lru_bwd_kernel, tt=tt),
                grid=(b, nt), in_specs=in_specs,
                out_specs=[pl.BlockSpec((tt, D), row), pl.BlockSpec((tt, D), row), last_spec],
                out_shape=[jax.ShapeDtypeStruct((b * t, D), F32), jax.ShapeDtypeStruct((b * t, D), BF16), last_shape],
                scratch_shapes=scratch + [pltpu.VMEM((tt, D_RNN), F32), pltpu.VMEM((D_RNN, D), BF16)],
                compiler_params=_cparams(2, 48),
                name="lru_bwd_proj",
            )(*args)
    return xo, h2, jnp.concatenate([last_f, last_b], axis=1)


def _select_kernel(h_ref, wr_ref, slot_ref, aff_ref, tri_ref, *, bs, t, cap):
    @pl.when(pl.program_id(0) == 0)
    def _():
        r = lax.broadcasted_iota(I32, (t, t), 0)
        c = lax.broadcasted_iota(I32, (t, t), 1)
        tri_ref[...] = jnp.where(r < c, 1.0, 0.0).astype(BF16)

    wr = wr_ref[...].astype(BF16)
    affs = []
    for s in range(bs):
        logits = _dot_nt(wr, h_ref[s])
        ex = jnp.exp(logits - jnp.max(logits, axis=0, keepdims=True))
        affs.append(ex / jnp.sum(ex, axis=0, keepdims=True))
    aff = jnp.concatenate(affs, axis=0) if bs > 1 else affs[0]
    bits = pltpu.bitcast(aff, I32)
    count = lambda mask: jnp.sum(jnp.where(mask, 1.0, 0.0), axis=-1, keepdims=True)
    th = jnp.zeros((bs * N_EXPERTS, 1), I32)
    for bit in range(30, -1, -1):
        cand = th | (1 << bit)
        th = jnp.where(count(bits >= cand) >= cap, cand, th)
    gt, eq = bits > th, bits == th
    need = cap - count(gt)
    lane = lax.broadcasted_iota(I32, (1, t), 1)
    lim = jnp.zeros((bs * N_EXPERTS, 1), I32)
    for bit in range(t.bit_length() - 1, -1, -1):
        cand = lim | (1 << bit)
        ok = (cand <= t) & (count(eq & (lane < cand)) <= need)
        lim = jnp.where(ok, cand, lim)
    sel = gt | (eq & (lane < lim))
    pos = _dot(jnp.where(sel, 1.0, 0.0).astype(BF16), tri_ref[...])
    slot = jnp.where(sel, pos.astype(I32), -1)
    for s in range(bs):
        slot_ref[s] = slot[s * N_EXPERTS:(s + 1) * N_EXPERTS, :]
        aff_ref[s] = aff[s * N_EXPERTS:(s + 1) * N_EXPERTS, :]


def _moe_select(h2, w_router, b, t, cap, bs):
    return pl.pallas_call(
        functools.partial(_select_kernel, bs=bs, t=t, cap=cap),
        grid=(b // bs,),
        in_specs=[pl.BlockSpec((bs, t, D), lambda i: (i, 0, 0)),
                  pl.BlockSpec((N_EXPERTS, D), lambda i: (0, 0))],
        out_specs=[pl.BlockSpec((bs, N_EXPERTS, t), lambda i: (i, 0, 0))] * 2,
        out_shape=[jax.ShapeDtypeStruct((b, N_EXPERTS, t), I32), jax.ShapeDtypeStruct((b, N_EXPERTS, t), F32)],
        scratch_shapes=[pltpu.VMEM((t, t), BF16)],
        compiler_params=_cparams(1, 48),
        name="moe_select",
    )(h2, w_router.T)


def _gather_kernel(slot_ref, aff_ref, h_ref, xs_ref, gc_ref, *, bq, cap, eg):
    j = lax.broadcasted_iota(I32, (cap, 1), 0)
    for s in range(bq):
        h = h_ref[s]
        for e0 in range(0, N_EXPERTS, eg):
            hots = [slot_ref[s, e:e + 1, :] == j for e in range(e0, e0 + eg)]
            p = jnp.concatenate([jnp.where(o, 1.0, 0.0).astype(BF16) for o in hots], axis=0)
            xs = _dot(p, h).astype(BF16)
            for k, e in enumerate(range(e0, e0 + eg)):
                xs_ref[e, s * cap:(s + 1) * cap, :] = xs[k * cap:(k + 1) * cap, :]
                gc_ref[e, s * cap:(s + 1) * cap, :] = jnp.sum(jnp.where(hots[k], aff_ref[s, e:e + 1, :], 0.0),
                                                              axis=-1, keepdims=True)


def _moe_gather(slot, aff, h3, b, t, cap, bq, eg):
    sa_spec = pl.BlockSpec((bq, N_EXPERTS, t), lambda i: (i, 0, 0))
    return pl.pallas_call(
        functools.partial(_gather_kernel, bq=bq, cap=cap, eg=eg),
        grid=(b // bq,),
        in_specs=[sa_spec, sa_spec, pl.BlockSpec((bq, t, D), lambda i: (i, 0, 0))],
        out_specs=[pl.BlockSpec((N_EXPERTS, bq * cap, D), lambda i: (0, i, 0)),
                   pl.BlockSpec((N_EXPERTS, bq * cap, 1), lambda i: (0, i, 0))],
        out_shape=[jax.ShapeDtypeStruct((N_EXPERTS, b * cap, D), BF16),
                   jax.ShapeDtypeStruct((N_EXPERTS, b * cap, 1), F32)],
        compiler_params=_cparams(1, 56),
        name="moe_gather",
    )(slot, aff, h3)


def _ffn_kernel(xs_ref, gc_ref, wg_ref, wu_ref, wd_ref, ys_ref, wgb_ref, wub_ref, wdb_ref):
    @pl.when(pl.program_id(1) == 0)
    def _():
        wgb_ref[...] = wg_ref[...].astype(BF16)
        wub_ref[...] = wu_ref[...].astype(BF16)
        wdb_ref[...] = wd_ref[...].astype(BF16)

    xs = xs_ref[...]
    acc = None
    for c in range(0, wgb_ref.shape[1], MXU_DIM):
        cs = slice(c, c + MXU_DIM)
        zg = _dot(xs, wgb_ref[:, cs])
        hid = (zg * _sigmoid(zg) * _dot(xs, wub_ref[:, cs])).astype(BF16)
        part = _dot(hid, wdb_ref[cs, :])
        acc = part if acc is None else acc + part
    ys_ref[...] = (acc * gc_ref[...]).astype(BF16)


def _moe_ffn(xs, gc, w_gate, w_up, w_down, layer, tmf=1024):
    r, f = xs.shape[1], w_gate.shape[-1]
    w_spec = lambda d0, d1: pl.BlockSpec((None, None, d0, d1), lambda e, i: (layer, e, 0, 0))
    return pl.pallas_call(
        _ffn_kernel,
        grid=(N_EXPERTS, r // tmf),
        in_specs=[pl.BlockSpec((None, tmf, D), lambda e, i: (e, i, 0)),
                  pl.BlockSpec((None, tmf, 1), lambda e, i: (e, i, 0)),
                  w_spec(D, f), w_spec(D, f), w_spec(f, D)],
        out_specs=pl.BlockSpec((None, tmf, D), lambda e, i: (e, i, 0)),
        out_shape=jax.ShapeDtypeStruct((N_EXPERTS, r, D), BF16),
        scratch_shapes=[pltpu.VMEM((D, f), BF16), pltpu.VMEM((D, f), BF16), pltpu.VMEM((f, D), BF16)],
        compiler_params=_cparams(2, 56),
        name="moe_ffn",
    )(xs, gc, w_gate, w_up, w_down)


def _scatter_rows(slot, width, ys):
    n = N_EXPERTS * width
    e_of = lax.broadcasted_iota(I32, (N_EXPERTS, n), 1) >> (width.bit_length() - 1)
    rep = jnp.where(e_of == lax.broadcasted_iota(I32, (N_EXPERTS, n), 0), 1.0, 0.0).astype(BF16)
    slot_rep = _dot(slot.astype(F32).astype(BF16), rep)
    jn = (lax.broadcasted_iota(I32, (1, n), 1) & (width - 1)).astype(F32)
    return _dot(jnp.where(slot_rep == jn, 1.0, 0.0).astype(BF16), ys)


def _combine_kernel(slot_ref, ys_ref, x_ref, mod_ref, o_ref, *, cap):
    y = _scatter_rows(slot_ref[...], cap, ys_ref[...].reshape(N_EXPERTS * cap, D))
    o_ref[...] = x_ref[...] + mod_ref[:, 5 * D:6 * D] * y


def _combine_win_kernel(st_s, ok_s, slot_ref, stv_ref, ys_ref, x_ref, mod_ref, o_ref, ysw_ref, *, cap, win, nt):
    tile = pl.program_id(0) * nt + pl.program_id(1)
    ga2 = mod_ref[:, 5 * D:6 * D]

    @pl.when(ok_s[tile] != 0)
    def _():
        for e in range(N_EXPERTS):
            st = pl.multiple_of(st_s[tile * N_EXPERTS + e], 16)
            ysw_ref[e * win:(e + 1) * win, :] = ys_ref[e, pl.ds(st, win), :]
        slot = slot_ref[...]
        rel = jnp.where(slot >= 0, slot - stv_ref[...], -1)
        o_ref[...] = x_ref[...] + ga2 * _scatter_rows(rel, win, ysw_ref[...])

    @pl.when(ok_s[tile] == 0)
    def _():
        y = _scatter_rows(slot_ref[...], cap, ys_ref[...].reshape(N_EXPERTS * cap, D))
        o_ref[...] = x_ref[...] + ga2 * y


def _moe_combine_windowed(slot, ys, x, mod3, mod_base, per_batch_mod, b, t, cap, tt=512, win=128):
    nt = t // tt
    cnt = (slot >= 0).reshape(b, N_EXPERTS, nt, tt).sum(-1).astype(I32)
    start = jnp.cumsum(cnt, axis=-1) - cnt
    st = jnp.clip((start // 16) * 16, 0, cap - win)
    ok = jnp.all(start + cnt <= st + win, axis=1).astype(I32)
    st_t = jnp.swapaxes(st, 1, 2)
    grid_spec = pltpu.PrefetchScalarGridSpec(
        num_scalar_prefetch=2,
        grid=(b, nt),
        in_specs=[pl.BlockSpec((None, tt, N_EXPERTS), lambda bi, ti, *_: (bi, ti, 0)),
                  pl.BlockSpec((None, None, 1, N_EXPERTS), lambda bi, ti, *_: (bi, ti, 0, 0)),
                  pl.BlockSpec((N_EXPERTS, None, cap, D), lambda bi, ti, *_: (0, bi, 0, 0)),
                  pl.BlockSpec((tt, D), lambda bi, ti, *_: (bi * nt + ti, 0)),
                  pl.BlockSpec((None, 1, 6 * D), lambda bi, ti, *_: (mod_base + bi * per_batch_mod, 0, 0))],
        out_specs=pl.BlockSpec((tt, D), lambda bi, ti, *_: (bi * nt + ti, 0)),
        scratch_shapes=[pltpu.VMEM((N_EXPERTS * win, D), BF16)])
    return pl.pallas_call(
        functools.partial(_combine_win_kernel, cap=cap, win=win, nt=nt),
        grid_spec=grid_spec,
        out_shape=jax.ShapeDtypeStruct((b * t, D), F32),
        compiler_params=_cparams(2, 56),
        name="moe_combine_win",
    )(st_t.reshape(-1), ok.reshape(-1), jnp.swapaxes(slot, 1, 2), st_t.reshape(b, nt, 1, N_EXPERTS), ys, x, mod3)


def _moe_combine(slot_t, ys, x, mod3, mod_base, per_batch_mod, b, t, cap, tt=512):
    tt = min(tt, t)
    nt = t // tt
    return pl.pallas_call(
        functools.partial(_combine_kernel, cap=cap),
        grid=(b, nt),
        in_specs=[pl.BlockSpec((None, tt, N_EXPERTS), lambda bi, ti: (bi, ti, 0)),
                  pl.BlockSpec((N_EXPERTS, None, cap, D), lambda bi, ti: (0, bi, 0, 0)),
                  pl.BlockSpec((tt, D), lambda bi, ti: (bi * nt + ti, 0)),
                  pl.BlockSpec((None, 1, 6 * D), lambda bi, ti: (mod_base + bi * per_batch_mod, 0, 0))],
        out_specs=pl.BlockSpec((tt, D), lambda bi, ti: (bi * nt + ti, 0)),
        out_shape=jax.ShapeDtypeStruct((b * t, D), F32),
        compiler_params=_cparams(2, 48),
        name="moe_combine",
    )(slot_t, ys, x, mod3)


def _ec_moe(x, h2, mod3, mod_base, per_batch_mod, moe_w, layer, b, t, bs, bg):
    w_router, w_gate, w_up, w_down = moe_w
    cap = EC_FACTOR * t // N_EXPERTS
    h3 = h2.reshape(b, t, D)
    slot, aff = _moe_select(h3, w_router[layer], b, t, cap, bs)
    xs, gc = _moe_gather(slot, aff, h3, b, t, cap, bq=bg, eg=4 if cap >= MXU_DIM else N_EXPERTS)
    ys = _moe_ffn(xs, gc, w_gate, w_up, w_down, layer).reshape(N_EXPERTS, b, cap, D)
    if cap >= MXU_DIM:
        return _moe_combine_windowed(slot, ys, x, mod3, mod_base, per_batch_mod, b, t, cap)
    return _moe_combine(jnp.swapaxes(slot, 1, 2), ys, x, mod3, mod_base, per_batch_mod, b, t, cap)


def kernel(x_prompt, x_sample, cache_a_k, cache_a_v, cache_b_k, cache_b_v, state_c_h, c, c_ctx, ada_w, ada_b, norm_mix_g, norm_ffn_g, a_w_in, a_q_norm_g, a_k_norm_g, a_lam_q, a_lam_k, a_subln_g, a_w_out, b_w_in, b_q_norm_g, b_k_norm_g, b_sink, b_w_out, c_w_in, c_conv_w, c_conv_b, c_w_rg, c_b_rg, c_w_ig, c_b_ig, c_lam, c_w_out, moe_w_router, moe_w_gate, moe_w_up, moe_w_down):
    bp, tp, _ = x_prompt.shape
    bs_, ts, _ = x_sample.shape
    past = cache_a_k.shape[2]
    cvec = jnp.concatenate([c, c_ctx[None, :], jnp.zeros((16 - bs_ - 1, D), F32)], axis=0)
    mods = _modulation(cvec, ada_w, ada_b)
    groups = {"p": (bp, tp, bs_, 0), "s": (bs_, ts, 0, 1)}
    xs = {"p": x_prompt.reshape(bp * tp, D), "s": x_sample.reshape(bs_ * ts, D)}
    ck_a = cache_a_k.reshape(bs_, -1, past, H_A * 2 * HD_A)
    cv_a = cache_a_v.reshape(bs_, -1, past, H_A * 2 * HD_A)
    ck_b = cache_b_k.reshape(bs_, -1, past, KV_B * HD_B)
    cv_b = cache_b_v.reshape(bs_, -1, past, KV_B * HD_B)
    moe_w = (moe_w_router, moe_w_gate, moe_w_up, moe_w_down)
    new_a_k, new_a_v, new_b_k, new_b_v, new_c_h = [], [], [], [], []
    ia = ib = ic = 0
    for l in range(DEPTH):
        mod3 = mods[l].reshape(16, 1, 6 * D)
        kind = l % 3
        for key in ("p", "s"):
            b, t, mbase, per_b = groups[key]
            rows_per_mod = t if per_b else b * t
            x = xs[key]
            sample = key == "s"
            if kind == 0:
                lam_init = 0.8 - 0.6 * math.exp(-0.3 * l)
                outs = _qkv_project(x, mod3, mbase, rows_per_mod, norm_mix_g[l], a_w_in[ia], a_q_norm_g[ia],
                                    a_k_norm_g[ia], D, D, D, t, rope=sample, kv_f32=not sample)
                q, k, v = outs[:3]
                if not sample:
                    new_a_k.append(outs[3].reshape(b, t, H_A, 2 * HD_A))
                    new_a_v.append(outs[4].reshape(b, t, H_A, 2 * HD_A))
                o = _diff_attention(q, k, v, (ck_a, cv_a, ia) if sample else None, a_lam_q[ia], a_lam_k[ia],
                                    a_subln_g[ia], lam_init, b, t, heads=1 if sample else H_A,
                                    tq=2048 if sample else t, nsub=16 if sample else 1)
                x, h2 = _proj_residual(o, x, mod3, mbase, rows_per_mod, norm_ffn_g[l], a_w_out[ia])
            elif kind == 1:
                nq, nk = H_B * HD_B, KV_B * HD_B
                outs = _qkv_project(x, mod3, mbase, rows_per_mod, norm_mix_g[l], b_w_in[ib], b_q_norm_g[ib],
                                    b_k_norm_g[ib], nq, nk, nk, t, rope=sample, kv_f32=not sample)
                q, k, v = outs[:3]
                if not sample:
                    new_b_k.append(outs[3].reshape(b, t, KV_B, HD_B))
                    new_b_v.append(outs[4].reshape(b, t, KV_B, HD_B))
                o = _gqa_attention(q, k, v, (ck_b, cv_b, ib) if sample else None, b_sink[ib], b, t,
                                   windowed=sample, tq=256)
                x, h2 = _proj_residual(o, x, mod3, mbase, rows_per_mod, norm_ffn_g[l], b_w_out[ib])
            else:
                gx = _norm_mod_matmul(x, mod3, mbase, rows_per_mod, norm_mix_g[l], c_w_in[ic])
                state = state_c_h[:, ic] if sample else jnp.zeros((b, 2, D_RNN), F32)
                x, h2, finals = _lru_mixer(gx, x, state, c_conv_w[ic], c_conv_b[ic], c_w_rg[ic], c_b_rg[ic],
                                           c_w_ig[ic], c_b_ig[ic], c_lam[ic], c_w_out[ic], mod3, mbase, per_b,
                                           norm_ffn_g[l], b, t)
                if not sample:
                    new_c_h.append(finals)
            xs[key] = _ec_moe(x, h2, mod3, mbase, per_b, moe_w, l, b, t,
                              bs=4 if sample else 16, bg=1 if sample else 8)
        ia, ib, ic = ia + (kind == 0), ib + (kind == 1), ic + (kind == 2)
    return (xs["p"].reshape(bp, tp, D), xs["s"].reshape(bs_, ts, D),
            jnp.stack(new_a_k, axis=1), jnp.stack(new_a_v, axis=1),
            jnp.stack(new_b_k, axis=1), jnp.stack(new_b_v, axis=1), jnp.stack(new_c_h, axis=1))
```

```python
import functools
import math

import jax
import jax.numpy as jnp
from jax import lax
from jax.experimental import pallas as pl
from jax.experimental.pallas import tpu as pltpu

F32, BF16, I32 = jnp.float32, jnp.bfloat16, jnp.int32

D = 1024
DEPTH = 4
GRID_W = 64
H_A, HD_A = 8, 64
H_B, KV_B, G_B, HD_B = 16, 4, 4, 64
WINDOW = 128
D_RNN = 1024
N_BLK_C, BW_C = 16, 64
LRU_C = 8.0
N_EXPERTS = 16
EC_FACTOR = 2
ROPE_THETA = 10000.0
EPS = 1e-6
LOG2E = math.log2(math.e)
LANES = 128
MXU_DIM = 256
MIB = 1024 * 1024


def _cparams(n_axes, vmem_mib=48):
    return pltpu.CompilerParams(dimension_semantics=("arbitrary",) * n_axes,
                                vmem_limit_bytes=vmem_mib * MIB)


def _sigmoid(x):
    return 0.5 * jnp.tanh(0.5 * x) + 0.5


def _rms(x):
    return x * lax.rsqrt(jnp.mean(x * x, axis=-1, keepdims=True) + EPS)


def _dot(a, b):
    return jnp.dot(a, b, preferred_element_type=F32)


def _dot_nt(a, b):
    return lax.dot_general(a, b, (((1,), (1,)), ((), ())), preferred_element_type=F32)


def _mod_kernel(c_ref, w_ref, b_ref, o_ref):
    c = c_ref[...]
    s = (c * _sigmoid(c)).astype(BF16)
    o_ref[...] = _dot(s, w_ref[...].astype(BF16)) + b_ref[...]


def _modulation(cvec, ada_w, ada_b):
    nt = 1536
    return pl.pallas_call(
        _mod_kernel,
        grid=(DEPTH, 6 * D // nt),
        in_specs=[pl.BlockSpec((16, D), lambda l, j: (0, 0)),
                  pl.BlockSpec((None, D, nt), lambda l, j: (l, 0, j)),
                  pl.BlockSpec((None, 1, nt), lambda l, j: (l, 0, j))],
        out_specs=pl.BlockSpec((None, 16, nt), lambda l, j: (l, 0, j)),
        out_shape=jax.ShapeDtypeStruct((DEPTH, 16, 6 * D), F32),
        compiler_params=_cparams(2, 32),
        name="adaln_mod",
    )(cvec, ada_w, ada_b.reshape(DEPTH, 1, 6 * D))


def _nmm_kernel(x_ref, mod_ref, g_ref, w_ref, o_ref, wbf_ref):
    @pl.when(pl.program_id(0) == 0)
    def _():
        wbf_ref[...] = w_ref[...].astype(BF16)

    m = mod_ref[...]
    h = (_rms(x_ref[...]) * g_ref[...] * (1.0 + m[:, D:2 * D]) + m[:, 0:D]).astype(BF16)
    o_ref[...] = _dot(h, wbf_ref[...])


def _norm_mod_matmul(x, mod3, mod_base, rows_per_mod, g, w, tm=512):
    r, n = x.shape[0], w.shape[1]
    return pl.pallas_call(
        _nmm_kernel,
        grid=(r // tm,),
        in_specs=[pl.BlockSpec((tm, D), lambda i: (i, 0)),
                  pl.BlockSpec((None, 1, 6 * D), lambda i: (mod_base + (i * tm) // rows_per_mod, 0, 0)),
                  pl.BlockSpec((1, D), lambda i: (0, 0)),
                  pl.BlockSpec((D, n), lambda i: (0, 0), pipeline_mode=pl.Buffered(1))],
        out_specs=pl.BlockSpec((tm, n), lambda i: (i, 0)),
        out_shape=jax.ShapeDtypeStruct((r, n), F32),
        scratch_shapes=[pltpu.VMEM((D, n), BF16)],
        compiler_params=_cparams(1, 48),
        name="norm_mod_matmul",
    )(x, mod3, g.reshape(1, D), w)


def _proj_tail(y, x_ref, mod_ref, g2_ref, xo_ref, h2_ref):
    m = mod_ref[...]
    xn = x_ref[...] + m[:, 2 * D:3 * D] * y
    xo_ref[...] = xn
    h2_ref[...] = (_rms(xn) * g2_ref[...] * (1.0 + m[:, 4 * D:5 * D]) + m[:, 3 * D:4 * D]).astype(BF16)


def _proj_kernel(o_ref, x_ref, mod_ref, g2_ref, w_ref, xo_ref, h2_ref, wbf_ref):
    @pl.when(pl.program_id(0) == 0)
    def _():
        wbf_ref[...] = w_ref[...].astype(BF16)

    _proj_tail(_dot(o_ref[...], wbf_ref[...]), x_ref, mod_ref, g2_ref, xo_ref, h2_ref)


def _proj_residual(o, x, mod3, mod_base, rows_per_mod, g2, w, tm=512):
    r = x.shape[0]
    return pl.pallas_call(
        _proj_kernel,
        grid=(r // tm,),
        in_specs=[pl.BlockSpec((tm, D), lambda i: (i, 0)),
                  pl.BlockSpec((tm, D), lambda i: (i, 0)),
                  pl.BlockSpec((None, 1, 6 * D), lambda i: (mod_base + (i * tm) // rows_per_mod, 0, 0)),
                  pl.BlockSpec((1, D), lambda i: (0, 0)),
                  pl.BlockSpec((D, D), lambda i: (0, 0))],
        out_specs=[pl.BlockSpec((tm, D), lambda i: (i, 0)), pl.BlockSpec((tm, D), lambda i: (i, 0))],
        out_shape=[jax.ShapeDtypeStruct((r, D), F32), jax.ShapeDtypeStruct((r, D), BF16)],
        scratch_shapes=[pltpu.VMEM((D, D), BF16)],
        compiler_params=_cparams(1, 40),
        name="proj_residual",
    )(o, x, mod3, g2.reshape(1, D), w)


def _group_inv_rms(x, on_mxu):
    ss = x * x
    if not on_mxu:
        low = lax.broadcasted_iota(I32, (1, LANES), 1) < HD_A
        s_lo = jnp.sum(jnp.where(low, ss, 0.0), axis=-1, keepdims=True)
        s_hi = jnp.sum(jnp.where(low, 0.0, ss), axis=-1, keepdims=True)
        return jnp.where(low, lax.rsqrt(s_lo * (1.0 / HD_A) + EPS), lax.rsqrt(s_hi * (1.0 / HD_A) + EPS))
    hi = ss.astype(BF16)
    lo = (ss - hi.astype(F32)).astype(BF16)
    r = lax.broadcasted_iota(I32, (LANES, LANES), 0) >> 6
    c = lax.broadcasted_iota(I32, (LANES, LANES), 1) >> 6
    ones_bd = jnp.where(r == c, 1.0, 0.0).astype(BF16)
    return lax.rsqrt((_dot(hi, ones_bd) + _dot(lo, ones_bd)) * (1.0 / HD_A) + EPS)


def _rope128(y, cos, sa, sb):
    return y * cos + pltpu.roll(y, LANES - 16, 1) * sa + pltpu.roll(y, 16, 1) * sb


def _qkv_kernel(*refs, nq, nk, rope, kv_f32, nsub):
    it = iter(refs)
    x_ref, mod_ref, g_ref, w_ref, qg_ref, kg_ref = (next(it) for _ in range(6))
    if rope:
        cos_ref, sa_ref, sb_ref = next(it), next(it), next(it)
    q_ref, k_ref, v_ref = next(it), next(it), next(it)
    kf_ref, vf_ref = (next(it), next(it)) if kv_f32 else (None, None)
    wbf_ref, qkv_ref = next(it), next(it)

    @pl.when(pl.program_id(0) == 0)
    def _():
        wbf_ref[...] = w_ref[...].astype(BF16)

    m = mod_ref[...]
    tm = x_ref.shape[0]
    parts = [slice(i * (tm // nsub), (i + 1) * (tm // nsub)) for i in range(nsub)]
    for rows in parts:
        h = (_rms(x_ref[rows, :]) * g_ref[...] * (1.0 + m[:, D:2 * D]) + m[:, 0:D]).astype(BF16)
        qkv_ref[rows, :] = _dot(h, wbf_ref[...])
    q_gain = qg_ref[...] * (HD_A ** -0.5 * LOG2E)
    for rows in parts:
        if rope:
            cos, sa, sb = cos_ref[rows, :], sa_ref[rows, :], sb_ref[rows, :]
        for j in range((nq + nk) // LANES):
            x = qkv_ref[rows, j * LANES:(j + 1) * LANES]
            is_q = j < nq // LANES
            y = x * _group_inv_rms(x, on_mxu=rope) * (q_gain if is_q else kg_ref[...])
            if rope:
                y = _rope128(y, cos, sa, sb)
            if is_q:
                q_ref[rows, j * LANES:(j + 1) * LANES] = y.astype(BF16)
            else:
                jj = j - nq // LANES
                k_ref[rows, jj * LANES:(jj + 1) * LANES] = y.astype(BF16)
                if kv_f32:
                    kf_ref[rows, jj * LANES:(jj + 1) * LANES] = y
        v = qkv_ref[rows, nq + nk:]
        v_ref[rows, :] = v.astype(BF16)
        if kv_f32:
            vf_ref[rows, :] = v


def _rope_tables(t):
    n_freq = HD_A // 4
    inv_freq = ROPE_THETA ** (-jnp.arange(n_freq, dtype=F32) / n_freq)
    pos_row = jnp.repeat(jnp.arange(t // GRID_W), GRID_W).astype(F32)
    pos_col = jnp.tile(jnp.arange(GRID_W), t // GRID_W).astype(F32)
    ang_r = pos_row[:, None] * inv_freq[None, :]
    ang_c = pos_col[:, None] * inv_freq[None, :]
    z = jnp.zeros_like(ang_r)
    cos64 = jnp.concatenate([jnp.cos(ang_r), jnp.cos(ang_r), jnp.cos(ang_c), jnp.cos(ang_c)], axis=-1)
    sa64 = jnp.concatenate([-jnp.sin(ang_r), z, -jnp.sin(ang_c), z], axis=-1)
    sb64 = jnp.concatenate([z, jnp.sin(ang_r), z, jnp.sin(ang_c)], axis=-1)
    return tuple(jnp.tile(a, (1, 2)) for a in (cos64, sa64, sb64))


def _qkv_project(x, mod3, mod_base, rows_per_mod, g, w, qg, kg, nq, nk, nv, t, rope, kv_f32, tm=512):
    r = x.shape[0]
    n = nq + nk + nv
    tile2 = lambda a: jnp.tile(a.reshape(1, HD_A), (1, 2))
    row = lambda width: pl.BlockSpec((tm, width), lambda i: (i, 0))
    args = [x, mod3, g.reshape(1, D), w, tile2(qg), tile2(kg)]
    in_specs = [row(D),
                pl.BlockSpec((None, 1, 6 * D), lambda i: (mod_base + (i * tm) // rows_per_mod, 0, 0)),
                pl.BlockSpec((1, D), lambda i: (0, 0)),
                pl.BlockSpec((D, n), lambda i: (0, 0), pipeline_mode=pl.Buffered(1)),
                pl.BlockSpec((1, LANES), lambda i: (0, 0)),
                pl.BlockSpec((1, LANES), lambda i: (0, 0))]
    if rope:
        args += list(_rope_tables(t))
        in_specs += [pl.BlockSpec((tm, LANES), lambda i: (i % (t // tm), 0))] * 3
    out_shape = [jax.ShapeDtypeStruct((r, nq), BF16), jax.ShapeDtypeStruct((r, nk), BF16),
                 jax.ShapeDtypeStruct((r, nv), BF16)]
    out_specs = [row(nq), row(nk), row(nv)]
    if kv_f32:
        out_shape += [jax.ShapeDtypeStruct((r, nk), F32), jax.ShapeDtypeStruct((r, nv), F32)]
        out_specs += [row(nk), row(nv)]
    return pl.pallas_call(
        functools.partial(_qkv_kernel, nq=nq, nk=nk, rope=rope, kv_f32=kv_f32, nsub=2),
        grid=(r // tm,),
        in_specs=in_specs, out_specs=out_specs, out_shape=out_shape,
        scratch_shapes=[pltpu.VMEM((D, n), BF16), pltpu.VMEM((tm, n), F32)],
        compiler_params=_cparams(1, 48),
        name="qkv_project",
    )(*args)


def _diff_attn_kernel(*refs, heads, ctx, lam_init, nsub):
    it = iter(refs)
    q_ref, k_ref, v_ref = next(it), next(it), next(it)
    if ctx:
        ck_ref, cv_ref = next(it), next(it)
    lq_ref, lk_ref, sg_ref, o_ref = next(it), next(it), next(it), next(it)
    e = jnp.exp(jnp.sum(lq_ref[...] * lk_ref[...], axis=-1, keepdims=True))
    lam = e[0:1, :] - e[1:2, :] + lam_init
    lane = lax.broadcasted_iota(I32, (1, LANES), 1)
    tqs = q_ref.shape[0] // nsub
    units = [(slice(h * LANES, (h + 1) * LANES), slice(j * tqs, (j + 1) * tqs))
             for h in range(heads) for j in range(nsub)]

    def scores(u):
        sl, rows = units[u]
        q = q_ref[rows, sl]
        out = []
        for c in range(2):
            qc = jnp.where((lane < HD_A) if c == 0 else (lane >= HD_A), q, jnp.zeros_like(q))
            out.append((_dot_nt(qc, k_ref[:, sl]), _dot_nt(qc, ck_ref[:, sl].astype(BF16)) if ctx else None))
        return out

    def weights(sc2):
        ps, ls = [], []
        for s, sc in sc2:
            m = jnp.max(s, axis=-1, keepdims=True)
            if ctx:
                m = jnp.maximum(m, jnp.max(sc, axis=-1, keepdims=True))
            p = jnp.exp2(s - m)
            l = jnp.sum(p, axis=-1, keepdims=True)
            pc = None
            if ctx:
                pc = jnp.exp2(sc - m)
                l = l + jnp.sum(pc, axis=-1, keepdims=True)
            ps.append((p, pc))
            ls.append(l)
        ratio = lam * ls[0] / ls[1]
        w_lat = (ps[0][0] - ratio * ps[1][0]).astype(BF16)
        w_ctx = (ps[0][1] - ratio * ps[1][1]).astype(BF16) if ctx else None
        return w_lat, w_ctx, 1.0 / ls[0]

    def values(u, w):
        sl, rows = units[u]
        w_lat, w_ctx, inv_l0 = w
        o = _dot(w_lat, v_ref[:, sl])
        if ctx:
            o = o + _dot(w_ctx, cv_ref[:, sl].astype(BF16))
        o_ref[rows, sl] = (_rms(o * inv_l0) * sg_ref[...] * (1.0 - lam_init)).astype(BF16)

    if not ctx:
        s_ref = next(it)
        tk = k_ref.shape[0]
        for u in range(len(units)):
            for c, (s, _) in enumerate(scores(u)):
                s_ref[(2 * u + c) * tqs:(2 * u + c + 1) * tqs, :] = s
        s = s_ref[...]
        p = jnp.exp2(s - jnp.max(s, axis=-1, keepdims=True))
        l = jnp.sum(p, axis=-1, keepdims=True)
        for u in range(len(units)):
            r0, r1 = slice(2 * u * tqs, (2 * u + 1) * tqs), slice((2 * u + 1) * tqs, (2 * u + 2) * tqs)
            w = (p[r0, :] - (lam * l[r0, :] / l[r1, :]) * p[r1, :]).astype(BF16)
            values(u, (w, None, 1.0 / l[r0, :]))
        return
    n = len(units)
    sc = {u: scores(u) for u in range(min(2, n))}
    for u in range(n):
        w = weights(sc.pop(u))
        if u + 2 < n:
            sc[u + 2] = scores(u + 2)
        values(u, w)


def _diff_attention(q, k, v, cache, lam_q, lam_k, subln_g, lam_init, b, t, heads, tq, nsub):
    nq = t // tq
    hb = H_A // heads
    args = [q, k, v]
    in_specs = [pl.BlockSpec((tq, heads * LANES), lambda bi, hi, qi: (bi * nq + qi, hi)),
                pl.BlockSpec((t, heads * LANES), lambda bi, hi, qi: (bi, hi)),
                pl.BlockSpec((t, heads * LANES), lambda bi, hi, qi: (bi, hi))]
    if cache is not None:
        ck, cv, la = cache
        p = ck.shape[2]
        args += [ck, cv]
        in_specs += [pl.BlockSpec((None, None, p, heads * LANES), lambda bi, hi, qi: (bi, la, 0, hi))] * 2
    args += [lam_q, lam_k, subln_g.reshape(1, LANES)]
    in_specs += [pl.BlockSpec((2, HD_A), lambda bi, hi, qi: (0, 0)),
                 pl.BlockSpec((2, HD_A), lambda bi, hi, qi: (0, 0)),
                 pl.BlockSpec((1, LANES), lambda bi, hi, qi: (0, 0))]
    return pl.pallas_call(
        functools.partial(_diff_attn_kernel, heads=heads, ctx=cache is not None, lam_init=lam_init, nsub=nsub),
        grid=(b, hb, nq),
        in_specs=in_specs,
        out_specs=pl.BlockSpec((tq, heads * LANES), lambda bi, hi, qi: (bi * nq + qi, hi)),
        out_shape=jax.ShapeDtypeStruct((b * t, H_A * LANES), BF16),
        scratch_shapes=[] if cache is not None else [pltpu.VMEM((2 * heads * tq, t), F32)],
        compiler_params=_cparams(3, 48),
        name="diff_attention",
    )(*args)


def _both_halves(x, half):
    lane_half = lax.broadcasted_iota(I32, (1, LANES), 1) >> 6
    xm = jnp.where(lane_half == half, x, 0.0)
    return xm + pltpu.roll(xm, HD_B, 1)


def _value_with_ones(x, half):
    lane = lax.broadcasted_iota(I32, (1, LANES), 1)
    return jnp.where(lane < HD_B, _both_halves(x, half), jnp.where(lane == HD_B, 1.0, 0.0))


def _gqa_dense_body(q_ref, k_ref, v_ref, sink_ref, o_ref, s_ref, t):
    lane_half = lax.broadcasted_iota(I32, (1, LANES), 1) >> 6
    vds, sinks = [], []
    for kv in range(KV_B):
        kcol = slice((kv // 2) * LANES, (kv // 2 + 1) * LANES)
        kd = _both_halves(k_ref[:, kcol].astype(F32), kv % 2).astype(BF16)
        vds.append(_value_with_ones(v_ref[:, kcol].astype(F32), kv % 2).astype(BF16))
        for g in range(G_B):
            h = kv * G_B + g
            q = q_ref[:, (h // 2) * LANES:(h // 2 + 1) * LANES]
            qm = jnp.where(lane_half == h % 2, q, jnp.zeros_like(q))
            s_ref[h * t:(h + 1) * t, :] = _dot_nt(qm, kd)
            sinks.append(jnp.broadcast_to(sink_ref[kv:kv + 1, g:g + 1] * LOG2E, (t, 1)))
    s = s_ref[...]
    sink2 = jnp.concatenate(sinks, axis=0)
    m = jnp.maximum(jnp.max(s, axis=-1, keepdims=True), sink2)
    p = jnp.exp2(s - m).astype(BF16)
    tail = jnp.exp2(sink2 - m)
    for hp in range(H_B // 2):
        halves = []
        for h in (2 * hp, 2 * hp + 1):
            rows = slice(h * t, (h + 1) * t)
            ov = _dot(p[rows, :], vds[h // G_B])
            halves.append(ov * (1.0 / (ov[:, HD_B:HD_B + 1] + tail[rows, :])))
        o_ref[:, hp * LANES:(hp + 1) * LANES] = jnp.where(
            lane_half == 0, halves[0], pltpu.roll(halves[1], HD_B, 1)).astype(BF16)


def _gqa_kernel(*refs, kvs, windowed, ctx, t, tq):
    it = iter(refs)
    q_ref, k_ref, v_ref = next(it), next(it), next(it)
    if ctx:
        ck_ref, cv_ref = next(it), next(it)
    sink_ref, o_ref = next(it), next(it)
    lane_half = lax.broadcasted_iota(I32, (1, LANES), 1) >> 6
    if not windowed:
        _gqa_dense_body(q_ref, k_ref, v_ref, sink_ref, o_ref, next(it), t)
        return
    kd_ref, vd_ref, ckd_ref, cvd_ref = next(it), next(it), next(it), next(it)

    @pl.when(pl.program_id(2) == 0)
    def _():
        for half in range(2):
            kd_ref[half] = _both_halves(k_ref[...].astype(F32), half).astype(BF16)
            vd_ref[half] = _value_with_ones(v_ref[...].astype(F32), half).astype(BF16)
            ckd_ref[half] = _both_halves(ck_ref[...], half).astype(BF16)
            cvd_ref[half] = _value_with_ones(cv_ref[...], half).astype(BF16)

    span = tq + 2 * WINDOW
    q0 = pl.program_id(2) * tq
    start = pl.multiple_of(jnp.clip(q0 - WINDOW, 0, t - span), LANES)
    qpos = q0 + lax.broadcasted_iota(I32, (tq, span), 0)
    kpos = start + lax.broadcasted_iota(I32, (tq, span), 1)
    valid = jnp.abs(qpos - kpos) <= WINDOW
    s_ref = next(it)
    sinks = []
    for half in range(2):
        kd, ckd = kd_ref[half, pl.ds(start, span), :], ckd_ref[half]
        for g in range(G_B):
            hl = half * G_B + g
            q = q_ref[:, hl // 2 * LANES:(hl // 2 + 1) * LANES]
            qm = jnp.where(lane_half == hl % 2, q, jnp.zeros_like(q))
            s_ref[hl * tq:(hl + 1) * tq, 0:span] = jnp.where(valid, _dot_nt(qm, kd), -jnp.inf)
            s_ref[hl * tq:(hl + 1) * tq, span:] = _dot_nt(qm, ckd)
            sinks.append(jnp.broadcast_to(sink_ref[half:half + 1, g:g + 1] * LOG2E, (tq, 1)))
    s = s_ref[...]
    sink2 = jnp.concatenate(sinks, axis=0)
    m = jnp.maximum(jnp.max(s, axis=-1, keepdims=True), sink2)
    p = jnp.exp2(s - m).astype(BF16)
    tail = jnp.exp2(sink2 - m)
    for hp in range(G_B):
        halves = []
        for hl in (2 * hp, 2 * hp + 1):
            rows, half = slice(hl * tq, (hl + 1) * tq), hl // G_B
            ov = (_dot(p[rows, 0:span], vd_ref[half, pl.ds(start, span), :]) + _dot(p[rows, span:], cvd_ref[half]))
            halves.append(ov * (1.0 / (ov[:, HD_B:HD_B + 1] + tail[rows, :])))
        o_ref[:, hp * LANES:(hp + 1) * LANES] = jnp.where(
            lane_half == 0, halves[0], pltpu.roll(halves[1], HD_B, 1)).astype(BF16)


def _gqa_attention(q, k, v, cache, sink, b, t, windowed, tq):
    if windowed:
        kvs, nq = 2, t // tq
        grid = (b, KV_B // 2, nq)
        qw = 2 * G_B * HD_B
        q_spec = pl.BlockSpec((tq, qw), lambda bi, kv, qi: (bi * nq + qi, kv))
        kv_spec = pl.BlockSpec((t, LANES), lambda bi, kv, qi: (bi, kv))
        sink_arr = sink.reshape(KV_B // 2, 2, G_B)
        sink_spec = pl.BlockSpec((None, 2, G_B), lambda bi, kv, qi: (kv, 0, 0))
        o_spec = pl.BlockSpec((tq, qw), lambda bi, kv, qi: (bi * nq + qi, kv))
    else:
        kvs, tq = KV_B, t
        grid = (b, 1, 1)
        q_spec = pl.BlockSpec((t, H_B * HD_B), lambda bi, kv, qi: (bi, 0))
        kv_spec = pl.BlockSpec((t, KV_B * HD_B), lambda bi, kv, qi: (bi, 0))
        sink_arr = sink.reshape(KV_B, G_B)
        sink_spec = pl.BlockSpec((KV_B, G_B), lambda bi, kv, qi: (0, 0))
        o_spec = pl.BlockSpec((t, H_B * HD_B), lambda bi, kv, qi: (bi, 0))
    args, in_specs = [q, k, v], [q_spec, kv_spec, kv_spec]
    if cache is not None:
        ck, cv, lb = cache
        p = ck.shape[2]
        args += [ck, cv]
        in_specs += [pl.BlockSpec((None, None, p, LANES), lambda bi, kv, qi: (bi, lb, 0, kv))] * 2
    args.append(sink_arr)
    in_specs.append(sink_spec)
    scratch = [pltpu.VMEM((H_B * t, t), F32)]
    if windowed:
        scratch = ([pltpu.VMEM((2, t, LANES), BF16)] * 2 + [pltpu.VMEM((2, p, LANES), BF16)] * 2
                   + [pltpu.VMEM((2 * G_B * tq, tq + 2 * WINDOW + p), F32)])
    return pl.pallas_call(
        functools.partial(_gqa_kernel, kvs=kvs, windowed=windowed, ctx=cache is not None, t=t, tq=tq),
        grid=grid, in_specs=in_specs, out_specs=o_spec,
        out_shape=jax.ShapeDtypeStruct((b * t, H_B * HD_B), BF16),
        scratch_shapes=scratch,
        compiler_params=_cparams(3, 48),
        name="gqa_attention",
    )(*args)


def _lru_conv_gates(xp_ref, xc_ref, xn_ref, cw_ref, cb_ref, wg_ref, bg_ref, lam_ref, a_ref, u_ref, ti, nt, tt):
    nl = -lam_ref[...]
    sp = jnp.maximum(nl, 0.0) + jnp.log1p(jnp.exp(-jnp.abs(nl)))
    cur = xc_ref[...]
    prev = xp_ref[...] * (ti > 0).astype(F32)
    nxt = xn_ref[...] * (ti < nt - 1).astype(F32)
    row = lax.broadcasted_iota(I32, (tt, 1), 0)
    xm1 = jnp.where(row == 0, prev[7:8, :], pltpu.roll(cur, 1, 0))
    xm2 = jnp.where(row == 0, prev[6:7, :], jnp.where(row == 1, prev[7:8, :], pltpu.roll(cur, 2, 0)))
    xp1 = jnp.where(row == tt - 1, nxt[0:1, :], pltpu.roll(cur, tt - 1, 0))
    cw = cw_ref[...]
    xc = xm2 * cw[0:1, :] + xm1 * cw[1:2, :] + cur * cw[2:3, :] + xp1 * cw[3:4, :] + cb_ref[...]
    xcb = xc.astype(BF16)
    for j in range(D_RNN // MXU_DIM):
        cs = slice(j * MXU_DIM, (j + 1) * MXU_DIM)
        z = _dot(xcb[:, cs], wg_ref[j].astype(BF16))
        r = _sigmoid(z[:, :MXU_DIM] + bg_ref[0:1, cs])
        i = _sigmoid(z[:, MXU_DIM:] + bg_ref[1:2, cs])
        a = jnp.exp(-LRU_C * r * sp[:, cs])
        a_ref[:, cs] = a
        u_ref[:, cs] = jnp.sqrt(1.0 - a * a) * i * xc[:, cs]


def _tile_scan(a_ref, u_ref, hs_ref, h, tt, reverse):
    row = lax.broadcasted_iota(I32, (8, 1), 0)
    ng = tt // 8
    for g in (range(ng - 1, -1, -1) if reverse else range(ng)):
        rows = slice(g * 8, (g + 1) * 8)
        a8, u8 = a_ref[rows, :], u_ref[rows, :]
        for s in (1, 2, 4):
            keep = (row < 8 - s) if reverse else (row >= s)
            shift = 8 - s if reverse else s
            u8 = a8 * jnp.where(keep, pltpu.roll(u8, shift, 0), 0.0) + u8
            a8 = a8 * jnp.where(keep, pltpu.roll(a8, shift, 0), 1.0)
        h8 = a8 * h + u8
        hs_ref[rows, :] = h8
        h = h8[0:1, :] if reverse else h8[7:8, :]
    return h


def _lru_fwd_kernel(xp_ref, xc_ref, xn_ref, cw_ref, cb_ref, wg_ref, bg_ref, lam_ref, h0_ref,
                    hf_ref, last_ref, carry_ref, a_ref, u_ref, *, tt):
    ti, nt = pl.program_id(1), pl.num_programs(1)

    @pl.when(ti == 0)
    def _():
        carry_ref[...] = h0_ref[...]

    _lru_conv_gates(xp_ref, xc_ref, xn_ref, cw_ref, cb_ref, wg_ref, bg_ref, lam_ref, a_ref, u_ref, ti, nt, tt)
    h = _tile_scan(a_ref, u_ref, hf_ref, carry_ref[...], tt, reverse=False)
    carry_ref[...] = h
    last_ref[...] = h


def _lru_bwd_kernel(xp_ref, xc_ref, xn_ref, cw_ref, cb_ref, wg_ref, bg_ref, lam_ref, h0_ref,
                    gate_ref, hf_ref, x_ref, mod_ref, g2_ref, w_ref,
                    xo_ref, h2_ref, last_ref, carry_ref, a_ref, u_ref, hb_ref, wbf_ref, *, tt):
    ti, nt = pl.program_id(1), pl.num_programs(1)

    @pl.when((pl.program_id(0) == 0) & (ti == 0))
    def _():
        wbf_ref[...] = w_ref[...].astype(BF16)

    @pl.when(ti == 0)
    def _():
        carry_ref[...] = h0_ref[...]

    tr = nt - 1 - ti
    _lru_conv_gates(xp_ref, xc_ref, xn_ref, cw_ref, cb_ref, wg_ref, bg_ref, lam_ref, a_ref, u_ref, tr, nt, tt)
    h = _tile_scan(a_ref, u_ref, hb_ref, carry_ref[...], tt, reverse=True)
    carry_ref[...] = h
    last_ref[...] = h
    g = gate_ref[...]
    gelu = 0.5 * g * (1.0 + jnp.tanh(math.sqrt(2.0 / math.pi) * (g + 0.044715 * (g * g * g))))
    y = (gelu * (hf_ref[...] + hb_ref[...])).astype(BF16)
    _proj_tail(_dot(y, wbf_ref[...]), x_ref, mod_ref, g2_ref, xo_ref, h2_ref)


def _lru_mixer(gx, x, state, conv_w, conv_b, w_rg, b_rg, w_ig, b_ig, lam, w_out, mod3, mod_base, per_batch_mod,
               g2, b, t, tt=256):
    nt = t // tt
    per_tile = MXU_DIM // BW_C
    eye = jnp.eye(per_tile, dtype=F32)

    def bd(w):
        w4 = w.reshape(N_BLK_C // per_tile, per_tile, BW_C, BW_C)
        return (w4[:, :, :, None, :] * eye[None, :, None, :, None]).reshape(-1, MXU_DIM, MXU_DIM)

    full = lambda shape: pl.BlockSpec(shape, lambda bi, ti: (0,) * len(shape))
    finals, xo, h2, hf = [], None, None, None
    for d in range(2):
        tile = (lambda ti: ti) if d == 0 else (lambda ti: nt - 1 - ti)
        x_spec = lambda f: pl.BlockSpec((tt, D_RNN), f)
        row = lambda bi, ti, tile=tile: (bi * nt + tile(ti), 0)
        halo = lambda f: pl.BlockSpec((8, D_RNN), f)
        in_specs = [halo(lambda bi, ti, tile=tile: (jnp.maximum((bi * nt + tile(ti)) * (tt // 8) - 1, 0), 1)),
                    x_spec(lambda bi, ti, tile=tile: (bi * nt + tile(ti), 1)),
                    halo(lambda bi, ti, tile=tile: (jnp.minimum((bi * nt + tile(ti) + 1) * (tt // 8), b * t // 8 - 1), 1)),
                    full((4, D_RNN)), full((1, D_RNN)), full((4, MXU_DIM, 2 * MXU_DIM)),
                    full((2, D_RNN)), full((1, D_RNN)),
                    pl.BlockSpec((None, 1, D_RNN), lambda bi, ti: (bi, 0, 0))]
        args = [gx, gx, gx, conv_w, conv_b.reshape(1, D_RNN),
                jnp.concatenate([bd(w_rg[d]), bd(w_ig[d])], axis=-1), jnp.stack([b_rg[d], b_ig[d]]),
                lam[d].reshape(1, D_RNN), state[:, d].reshape(b, 1, D_RNN)]
        scratch = [pltpu.VMEM((1, D_RNN), F32), pltpu.VMEM((tt, D_RNN), F32), pltpu.VMEM((tt, D_RNN), F32)]
        last_spec = pl.BlockSpec((None, 1, D_RNN), lambda bi, ti: (bi, 0, 0))
        last_shape = jax.ShapeDtypeStruct((b, 1, D_RNN), F32)
        if d == 0:
            hf, last_f = pl.pallas_call(
                functools.partial(_lru_fwd_kernel, tt=tt),
                grid=(b, nt), in_specs=in_specs,
                out_specs=[pl.BlockSpec((tt, D_RNN), row), last_spec],
                out_shape=[jax.ShapeDtypeStruct((b * t, D_RNN), F32), last_shape],
                scratch_shapes=scratch,
                compiler_params=_cparams(2, 48),
                name="lru_fwd",
            )(*args)
        else:
            in_specs += [pl.BlockSpec((tt, D_RNN), row), pl.BlockSpec((tt, D_RNN), row), pl.BlockSpec((tt, D), row),
                         pl.BlockSpec((None, 1, 6 * D), lambda bi, ti: (mod_base + bi * per_batch_mod, 0, 0)),
                         full((1, D)), full((D_RNN, D))]
            args += [gx, hf, x, mod3, g2.reshape(1, D), w_out]
            xo, h2, last_b = pl.pallas_call(
                functools.partial(_lru_bwd_kernel, tt=tt),
                grid=(b, nt), in_specs=in_specs,
                out_specs=[pl.BlockSpec((tt, D), row), pl.BlockSpec((tt, D), row), last_spec],
                out_shape=[jax.ShapeDtypeStruct((b * t, D), F32), jax.ShapeDtypeStruct((b * t, D), BF16), last_shape],
                scratch_shapes=scratch + [pltpu.VMEM((tt, D_RNN), F32), pltpu.VMEM((D_RNN, D), BF16)],
                compiler_params=_cparams(2, 48),
                name="lru_bwd_proj",
            )(*args)
    return xo, h2, jnp.concatenate([last_f, last_b], axis=1)


def _select_kernel(h_ref, wr_ref, slot_ref, aff_ref, tri_ref, *, bs, t, cap):
    @pl.when(pl.program_id(0) == 0)
    def _():
        r = lax.broadcasted_iota(I32, (t, t), 0)
        c = lax.broadcasted_iota(I32, (t, t), 1)
        tri_ref[...] = jnp.where(r < c, 1.0, 0.0).astype(BF16)

    wr = wr_ref[...].astype(BF16)
    affs = []
    for s in range(bs):
        logits = _dot_nt(wr, h_ref[s])
        ex = jnp.exp(logits - jnp.max(logits, axis=0, keepdims=True))
        affs.append(ex / jnp.sum(ex, axis=0, keepdims=True))
    aff = jnp.concatenate(affs, axis=0) if bs > 1 else affs[0]
    bits = pltpu.bitcast(aff, I32)
    count = lambda mask: jnp.sum(jnp.where(mask, 1.0, 0.0), axis=-1, keepdims=True)
    th = jnp.zeros((bs * N_EXPERTS, 1), I32)
    for bit in range(30, -1, -1):
        cand = th | (1 << bit)
        th = jnp.where(count(bits >= cand) >= cap, cand, th)
    gt, eq = bits > th, bits == th
    need = cap - count(gt)
    lane = lax.broadcasted_iota(I32, (1, t), 1)
    lim = jnp.zeros((bs * N_EXPERTS, 1), I32)
    for bit in range(t.bit_length() - 1, -1, -1):
        cand = lim | (1 << bit)
        ok = (cand <= t) & (count(eq & (lane < cand)) <= need)
        lim = jnp.where(ok, cand, lim)
    sel = gt | (eq & (lane < lim))
    pos = _dot(jnp.where(sel, 1.0, 0.0).astype(BF16), tri_ref[...])
    slot = jnp.where(sel, pos.astype(I32), -1)
    for s in range(bs):
        slot_ref[s] = slot[s * N_EXPERTS:(s + 1) * N_EXPERTS, :]
        aff_ref[s] = aff[s * N_EXPERTS:(s + 1) * N_EXPERTS, :]


def _moe_select(h2, w_router, b, t, cap, bs):
    return pl.pallas_call(
        functools.partial(_select_kernel, bs=bs, t=t, cap=cap),
        grid=(b // bs,),
        in_specs=[pl.BlockSpec((bs, t, D), lambda i: (i, 0, 0)),
                  pl.BlockSpec((N_EXPERTS, D), lambda i: (0, 0))],
        out_specs=[pl.BlockSpec((bs, N_EXPERTS, t), lambda i: (i, 0, 0))] * 2,
        out_shape=[jax.ShapeDtypeStruct((b, N_EXPERTS, t), I32), jax.ShapeDtypeStruct((b, N_EXPERTS, t), F32)],
        scratch_shapes=[pltpu.VMEM((t, t), BF16)],
        compiler_params=_cparams(1, 48),
        name="moe_select",
    )(h2, w_router.T)


def _gather_kernel(slot_ref, aff_ref, h_ref, xs_ref, gc_ref, *, bq, cap, eg):
    j = lax.broadcasted_iota(I32, (cap, 1), 0)
    for s in range(bq):
        h = h_ref[s]
        for e0 in range(0, N_EXPERTS, eg):
            hots = [slot_ref[s, e:e + 1, :] == j for e in range(e0, e0 + eg)]
            p = jnp.concatenate([jnp.where(o, 1.0, 0.0).astype(BF16) for o in hots], axis=0)
            xs = _dot(p, h).astype(BF16)
            for k, e in enumerate(range(e0, e0 + eg)):
                xs_ref[e, s * cap:(s + 1) * cap, :] = xs[k * cap:(k + 1) * cap, :]
                gc_ref[e, s * cap:(s + 1) * cap, :] = jnp.sum(jnp.where(hots[k], aff_ref[s, e:e + 1, :], 0.0),
                                                              axis=-1, keepdims=True)


def _moe_gather(slot, aff, h3, b, t, cap, bq, eg):
    sa_spec = pl.BlockSpec((bq, N_EXPERTS, t), lambda i: (i, 0, 0))
    return pl.pallas_call(
        functools.partial(_gather_kernel, bq=bq, cap=cap, eg=eg),
        grid=(b // bq,),
        in_specs=[sa_spec, sa_spec, pl.BlockSpec((bq, t, D), lambda i: (i, 0, 0))],
        out_specs=[pl.BlockSpec((N_EXPERTS, bq * cap, D), lambda i: (0, i, 0)),
                   pl.BlockSpec((N_EXPERTS, bq * cap, 1), lambda i: (0, i, 0))],
        out_shape=[jax.ShapeDtypeStruct((N_EXPERTS, b * cap, D), BF16),
                   jax.ShapeDtypeStruct((N_EXPERTS, b * cap, 1), F32)],
        compiler_params=_cparams(1, 56),
        name="moe_gather",
    )(slot, aff, h3)


def _ffn_kernel(xs_ref, gc_ref, wg_ref, wu_ref, wd_ref, ys_ref, wgb_ref, wub_ref, wdb_ref):
    @pl.when(pl.program_id(1) == 0)
    def _():
        wgb_ref[...] = wg_ref[...].astype(BF16)
        wub_ref[...] = wu_ref[...].astype(BF16)
        wdb_ref[...] = wd_ref[...].astype(BF16)

    xs = xs_ref[...]
    acc = None
    for c in range(0, wgb_ref.shape[1], MXU_DIM):
        cs = slice(c, c + MXU_DIM)
        zg = _dot(xs, wgb_ref[:, cs])
        hid = (zg * _sigmoid(zg) * _dot(xs, wub_ref[:, cs])).astype(BF16)
        part = _dot(hid, wdb_ref[cs, :])
        acc = part if acc is None else acc + part
    ys_ref[...] = (acc * gc_ref[...]).astype(BF16)


def _moe_ffn(xs, gc, w_gate, w_up, w_down, layer, tmf=1024):
    r, f = xs.shape[1], w_gate.shape[-1]
    w_spec = lambda d0, d1: pl.BlockSpec((None, None, d0, d1), lambda e, i: (layer, e, 0, 0))
    return pl.pallas_call(
        _ffn_kernel,
        grid=(N_EXPERTS, r // tmf),
        in_specs=[pl.BlockSpec((None, tmf, D), lambda e, i: (e, i, 0)),
                  pl.BlockSpec((None, tmf, 1), lambda e, i: (e, i, 0)),
                  w_spec(D, f), w_spec(D, f), w_spec(f, D)],
        out_specs=pl.BlockSpec((None, tmf, D), lambda e, i: (e, i, 0)),
        out_shape=jax.ShapeDtypeStruct((N_EXPERTS, r, D), BF16),
        scratch_shapes=[pltpu.VMEM((D, f), BF16), pltpu.VMEM((D, f), BF16), pltpu.VMEM((f, D), BF16)],
        compiler_params=_cparams(2, 56),
        name="moe_ffn",
    )(xs, gc, w_gate, w_up, w_down)


def _scatter_rows(slot, width, ys):
    n = N_EXPERTS * width
    e_of = lax.broadcasted_iota(I32, (N_EXPERTS, n), 1) >> (width.bit_length() - 1)
    rep = jnp.where(e_of == lax.broadcasted_iota(I32, (N_EXPERTS, n), 0), 1.0, 0.0).astype(BF16)
    slot_rep = _dot(slot.astype(F32).astype(BF16), rep)
    jn = (lax.broadcasted_iota(I32, (1, n), 1) & (width - 1)).astype(F32)
    return _dot(jnp.where(slot_rep == jn, 1.0, 0.0).astype(BF16), ys)


def _combine_kernel(slot_ref, ys_ref, x_ref, mod_ref, o_ref, *, cap):
    y = _scatter_rows(slot_ref[...], cap, ys_ref[...].reshape(N_EXPERTS * cap, D))
    o_ref[...] = x_ref[...] + mod_ref[:, 5 * D:6 * D] * y


def _combine_win_kernel(st_s, ok_s, slot_ref, stv_ref, ys_ref, x_ref, mod_ref, o_ref, ysw_ref, *, cap, win, nt):
    tile = pl.program_id(0) * nt + pl.program_id(1)
    ga2 = mod_ref[:, 5 * D:6 * D]

    @pl.when(ok_s[tile] != 0)
    def _():
        for e in range(N_EXPERTS):
            st = pl.multiple_of(st_s[tile * N_EXPERTS + e], 16)
            ysw_ref[e * win:(e + 1) * win, :] = ys_ref[e, pl.ds(st, win), :]
        slot = slot_ref[...]
        rel = jnp.where(slot >= 0, slot - stv_ref[...], -1)
        o_ref[...] = x_ref[...] + ga2 * _scatter_rows(rel, win, ysw_ref[...])

    @pl.when(ok_s[tile] == 0)
    def _():
        y = _scatter_rows(slot_ref[...], cap, ys_ref[...].reshape(N_EXPERTS * cap, D))
        o_ref[...] = x_ref[...] + ga2 * y


def _moe_combine_windowed(slot, ys, x, mod3, mod_base, per_batch_mod, b, t, cap, tt=256, win=64):
    nt = t // tt
    cnt = (slot >= 0).reshape(b, N_EXPERTS, nt, tt).sum(-1).astype(I32)
    start = jnp.cumsum(cnt, axis=-1) - cnt
    st = jnp.clip((start // 16) * 16, 0, cap - win)
    ok = jnp.all(start + cnt <= st + win, axis=1).astype(I32)
    st_t = jnp.swapaxes(st, 1, 2)
    grid_spec = pltpu.PrefetchScalarGridSpec(
        num_scalar_prefetch=2,
        grid=(b, nt),
        in_specs=[pl.BlockSpec((None, tt, N_EXPERTS), lambda bi, ti, *_: (bi, ti, 0)),
                  pl.BlockSpec((None, None, 1, N_EXPERTS), lambda bi, ti, *_: (bi, ti, 0, 0)),
                  pl.BlockSpec((N_EXPERTS, None, cap, D), lambda bi, ti, *_: (0, bi, 0, 0)),
                  pl.BlockSpec((tt, D), lambda bi, ti, *_: (bi * nt + ti, 0)),
                  pl.BlockSpec((None, 1, 6 * D), lambda bi, ti, *_: (mod_base + bi * per_batch_mod, 0, 0))],
        out_specs=pl.BlockSpec((tt, D), lambda bi, ti, *_: (bi * nt + ti, 0)),
        scratch_shapes=[pltpu.VMEM((N_EXPERTS * win, D), BF16)])
    return pl.pallas_call(
        functools.partial(_combine_win_kernel, cap=cap, win=win, nt=nt),
        grid_spec=grid_spec,
        out_shape=jax.ShapeDtypeStruct((b * t, D), F32),
        compiler_params=_cparams(2, 56),
        name="moe_combine_win",
    )(st_t.reshape(-1), ok.reshape(-1), jnp.swapaxes(slot, 1, 2), st_t.reshape(b, nt, 1, N_EXPERTS), ys, x, mod3)


def _moe_combine(slot_t, ys, x, mod3, mod_base, per_batch_mod, b, t, cap, tt=512):
    tt = min(tt, t)
    nt = t // tt
    return pl.pallas_call(
        functools.partial(_combine_kernel, cap=cap),
        grid=(b, nt),
        in_specs=[pl.BlockSpec((None, tt, N_EXPERTS), lambda bi, ti: (bi, ti, 0)),
                  pl.BlockSpec((N_EXPERTS, None, cap, D), lambda bi, ti: (0, bi, 0, 0)),
                  pl.BlockSpec((tt, D), lambda bi, ti: (bi * nt + ti, 0)),
                  pl.BlockSpec((None, 1, 6 * D), lambda bi, ti: (mod_base + bi * per_batch_mod, 0, 0))],
        out_specs=pl.BlockSpec((tt, D), lambda bi, ti: (bi * nt + ti, 0)),
        out_shape=jax.ShapeDtypeStruct((b * t, D), F32),
        compiler_params=_cparams(2, 48),
        name="moe_combine",
    )(slot_t, ys, x, mod3)


def _ec_moe(x, h2, mod3, mod_base, per_batch_mod, moe_w, layer, b, t, bs, bg):
    w_router, w_gate, w_up, w_down = moe_w
    cap = EC_FACTOR * t // N_EXPERTS
    h3 = h2.reshape(b, t, D)
    slot, aff = _moe_select(h3, w_router[layer], b, t, cap, bs)
    xs, gc = _moe_gather(slot, aff, h3, b, t, cap, bq=bg, eg=4 if cap >= MXU_DIM else N_EXPERTS)
    ys = _moe_ffn(xs, gc, w_gate, w_up, w_down, layer).reshape(N_EXPERTS, b, cap, D)
    if cap >= MXU_DIM:
        return _moe_combine_windowed(slot, ys, x, mod3, mod_base, per_batch_mod, b, t, cap)
    return _moe_combine(jnp.swapaxes(slot, 1, 2), ys, x, mod3, mod_base, per_batch_mod, b, t, cap)


def kernel(x_prompt, x_sample, cache_a_k, cache_a_v, cache_b_k, cache_b_v, state_c_h, c, c_ctx, ada_w, ada_b, norm_mix_g, norm_ffn_g, a_w_in, a_q_norm_g, a_k_norm_g, a_lam_q, a_lam_k, a_subln_g, a_w_out, b_w_in, b_q_norm_g, b_k_norm_g, b_sink, b_w_out, c_w_in, c_conv_w, c_conv_b, c_w_rg, c_b_rg, c_w_ig, c_b_ig, c_lam, c_w_out, moe_w_router, moe_w_gate, moe_w_up, moe_w_down):
    bp, tp, _ = x_prompt.shape
    bs_, ts, _ = x_sample.shape
    past = cache_a_k.shape[2]
    cvec = jnp.concatenate([c, c_ctx[None, :], jnp.zeros((16 - bs_ - 1, D), F32)], axis=0)
    mods = _modulation(cvec, ada_w, ada_b)
    groups = {"p": (bp, tp, bs_, 0), "s": (bs_, ts, 0, 1)}
    xs = {"p": x_prompt.reshape(bp * tp, D), "s": x_sample.reshape(bs_ * ts, D)}
    ck_a = cache_a_k.reshape(bs_, -1, past, H_A * 2 * HD_A)
    cv_a = cache_a_v.reshape(bs_, -1, past, H_A * 2 * HD_A)
    ck_b = cache_b_k.reshape(bs_, -1, past, KV_B * HD_B)
    cv_b = cache_b_v.reshape(bs_, -1, past, KV_B * HD_B)
    moe_w = (moe_w_router, moe_w_gate, moe_w_up, moe_w_down)
    new_a_k, new_a_v, new_b_k, new_b_v, new_c_h = [], [], [], [], []
    ia = ib = ic = 0
    for l in range(DEPTH):
        mod3 = mods[l].reshape(16, 1, 6 * D)
        kind = l % 3
        for key in ("p", "s"):
            b, t, mbase, per_b = groups[key]
            rows_per_mod = t if per_b else b * t
            x = xs[key]
            sample = key == "s"
            if kind == 0:
                lam_init = 0.8 - 0.6 * math.exp(-0.3 * l)
                outs = _qkv_project(x, mod3, mbase, rows_per_mod, norm_mix_g[l], a_w_in[ia], a_q_norm_g[ia],
                                    a_k_norm_g[ia], D, D, D, t, rope=sample, kv_f32=not sample)
                q, k, v = outs[:3]
                if not sample:
                    new_a_k.append(outs[3].reshape(b, t, H_A, 2 * HD_A))
                    new_a_v.append(outs[4].reshape(b, t, H_A, 2 * HD_A))
                o = _diff_attention(q, k, v, (ck_a, cv_a, ia) if sample else None, a_lam_q[ia], a_lam_k[ia],
                                    a_subln_g[ia], lam_init, b, t, heads=1 if sample else H_A,
                                    tq=2048 if sample else t, nsub=16 if sample else 1)
                x, h2 = _proj_residual(o, x, mod3, mbase, rows_per_mod, norm_ffn_g[l], a_w_out[ia])
            elif kind == 1:
                nq, nk = H_B * HD_B, KV_B * HD_B
                outs = _qkv_project(x, mod3, mbase, rows_per_mod, norm_mix_g[l], b_w_in[ib], b_q_norm_g[ib],
                                    b_k_norm_g[ib], nq, nk, nk, t, rope=sample, kv_f32=not sample)
                q, k, v = outs[:3]
                if not sample:
                    new_b_k.append(outs[3].reshape(b, t, KV_B, HD_B))
                    new_b_v.append(outs[4].reshape(b, t, KV_B, HD_B))
                o = _gqa_attention(q, k, v, (ck_b, cv_b, ib) if sample else None, b_sink[ib], b, t,
                                   windowed=sample, tq=256)
                x, h2 = _proj_residual(o, x, mod3, mbase, rows_per_mod, norm_ffn_g[l], b_w_out[ib])
            else:
                gx = _norm_mod_matmul(x, mod3, mbase, rows_per_mod, norm_mix_g[l], c_w_in[ic])
                state = state_c_h[:, ic] if sample else jnp.zeros((b, 2, D_RNN), F32)
                x, h2, finals = _lru_mixer(gx, x, state, c_conv_w[ic], c_conv_b[ic], c_w_rg[ic], c_b_rg[ic],
                                           c_w_ig[ic], c_b_ig[ic], c_lam[ic], c_w_out[ic], mod3, mbase, per_b,
                                           norm_ffn_g[l], b, t)
                if not sample:
                    new_c_h.append(finals)
            xs[key] = _ec_moe(x, h2, mod3, mbase, per_b, moe_w, l, b, t,
                              bs=4 if sample else 16, bg=1 if sample else 8)
        ia, ib, ic = ia + (kind == 0), ib + (kind == 1), ic + (kind == 2)
    return (xs["p"].reshape(bp, tp, D), xs["s"].reshape(bs_, ts, D),
            jnp.stack(new_a_k, axis=1), jnp.stack(new_a_v, axis=1),
            jnp.stack(new_b_k, axis=1), jnp.stack(new_b_v, axis=1), jnp.stack(new_c_h, axis=1))
```

```python
import functools
import math

import jax
import jax.numpy as jnp
from jax import lax
from jax.experimental import pallas as pl
from jax.experimental.pallas import tpu as pltpu

F32, BF16, I32 = jnp.float32, jnp.bfloat16, jnp.int32

D = 1024
DEPTH = 4
GRID_W = 64
H_A, HD_A = 8, 64
H_B, KV_B, G_B, HD_B = 16, 4, 4, 64
WINDOW = 128
D_RNN = 1024
N_BLK_C, BW_C = 16, 64
LRU_C = 8.0
N_EXPERTS = 16
EC_FACTOR = 2
ROPE_THETA = 10000.0
EPS = 1e-6
LOG2E = math.log2(math.e)
LANES = 128
MXU_DIM = 256
MIB = 1024 * 1024


def _cparams(n_axes, vmem_mib=48):
    return pltpu.CompilerParams(dimension_semantics=("arbitrary",) * n_axes,
                                vmem_limit_bytes=vmem_mib * MIB)


def _sigmoid(x):
    return 0.5 * jnp.tanh(0.5 * x) + 0.5


def _rms(x):
    return x * lax.rsqrt(jnp.mean(x * x, axis=-1, keepdims=True) + EPS)


def _dot(a, b):
    return jnp.dot(a, b, preferred_element_type=F32)


def _dot_nt(a, b):
    return lax.dot_general(a, b, (((1,), (1,)), ((), ())), preferred_element_type=F32)


def _mod_kernel(c_ref, w_ref, b_ref, o_ref):
    c = c_ref[...]
    s = (c * _sigmoid(c)).astype(BF16)
    o_ref[...] = _dot(s, w_ref[...].astype(BF16)) + b_ref[...]


def _modulation(cvec, ada_w, ada_b):
    nt = 1536
    return pl.pallas_call(
        _mod_kernel,
        grid=(DEPTH, 6 * D // nt),
        in_specs=[pl.BlockSpec((16, D), lambda l, j: (0, 0)),
                  pl.BlockSpec((None, D, nt), lambda l, j: (l, 0, j)),
                  pl.BlockSpec((None, 1, nt), lambda l, j: (l, 0, j))],
        out_specs=pl.BlockSpec((None, 16, nt), lambda l, j: (l, 0, j)),
        out_shape=jax.ShapeDtypeStruct((DEPTH, 16, 6 * D), F32),
        compiler_params=_cparams(2, 32),
        name="adaln_mod",
    )(cvec, ada_w, ada_b.reshape(DEPTH, 1, 6 * D))


def _nmm_kernel(x_ref, mod_ref, g_ref, w_ref, o_ref, wbf_ref):
    @pl.when(pl.program_id(0) == 0)
    def _():
        wbf_ref[...] = w_ref[...].astype(BF16)

    m = mod_ref[...]
    h = (_rms(x_ref[...]) * g_ref[...] * (1.0 + m[:, D:2 * D]) + m[:, 0:D]).astype(BF16)
    o_ref[...] = _dot(h, wbf_ref[...])


def _norm_mod_matmul(x, mod3, mod_base, rows_per_mod, g, w, tm=1024):
    r, n = x.shape[0], w.shape[1]
    return pl.pallas_call(
        _nmm_kernel,
        grid=(r // tm,),
        in_specs=[pl.BlockSpec((tm, D), lambda i: (i, 0)),
                  pl.BlockSpec((None, 1, 6 * D), lambda i: (mod_base + (i * tm) // rows_per_mod, 0, 0)),
                  pl.BlockSpec((1, D), lambda i: (0, 0)),
                  pl.BlockSpec((D, n), lambda i: (0, 0), pipeline_mode=pl.Buffered(1))],
        out_specs=pl.BlockSpec((tm, n), lambda i: (i, 0)),
        out_shape=jax.ShapeDtypeStruct((r, n), F32),
        scratch_shapes=[pltpu.VMEM((D, n), BF16)],
        compiler_params=_cparams(1, 48),
        name="norm_mod_matmul",
    )(x, mod3, g.reshape(1, D), w)


def _proj_tail(y, x_ref, mod_ref, g2_ref, xo_ref, h2_ref):
    m = mod_ref[...]
    xn = x_ref[...] + m[:, 2 * D:3 * D] * y
    xo_ref[...] = xn
    h2_ref[...] = (_rms(xn) * g2_ref[...] * (1.0 + m[:, 4 * D:5 * D]) + m[:, 3 * D:4 * D]).astype(BF16)


def _proj_kernel(o_ref, x_ref, mod_ref, g2_ref, w_ref, xo_ref, h2_ref, wbf_ref):
    @pl.when(pl.program_id(0) == 0)
    def _():
        wbf_ref[...] = w_ref[...].astype(BF16)

    _proj_tail(_dot(o_ref[...], wbf_ref[...]), x_ref, mod_ref, g2_ref, xo_ref, h2_ref)


def _proj_residual(o, x, mod3, mod_base, rows_per_mod, g2, w, tm=1024):
    r = x.shape[0]
    return pl.pallas_call(
        _proj_kernel,
        grid=(r // tm,),
        in_specs=[pl.BlockSpec((tm, D), lambda i: (i, 0)),
                  pl.BlockSpec((tm, D), lambda i: (i, 0)),
                  pl.BlockSpec((None, 1, 6 * D), lambda i: (mod_base + (i * tm) // rows_per_mod, 0, 0)),
                  pl.BlockSpec((1, D), lambda i: (0, 0)),
                  pl.BlockSpec((D, D), lambda i: (0, 0))],
        out_specs=[pl.BlockSpec((tm, D), lambda i: (i, 0)), pl.BlockSpec((tm, D), lambda i: (i, 0))],
        out_shape=[jax.ShapeDtypeStruct((r, D), F32), jax.ShapeDtypeStruct((r, D), BF16)],
        scratch_shapes=[pltpu.VMEM((D, D), BF16)],
        compiler_params=_cparams(1, 40),
        name="proj_residual",
    )(o, x, mod3, g2.reshape(1, D), w)


def _group_inv_rms(x, on_mxu):
    ss = x * x
    if not on_mxu:
        low = lax.broadcasted_iota(I32, (1, LANES), 1) < HD_A
        s_lo = jnp.sum(jnp.where(low, ss, 0.0), axis=-1, keepdims=True)
        s_hi = jnp.sum(jnp.where(low, 0.0, ss), axis=-1, keepdims=True)
        return jnp.where(low, lax.rsqrt(s_lo * (1.0 / HD_A) + EPS), lax.rsqrt(s_hi * (1.0 / HD_A) + EPS))
    hi = ss.astype(BF16)
    lo = (ss - hi.astype(F32)).astype(BF16)
    r = lax.broadcasted_iota(I32, (LANES, LANES), 0) >> 6
    c = lax.broadcasted_iota(I32, (LANES, LANES), 1) >> 6
    ones_bd = jnp.where(r == c, 1.0, 0.0).astype(BF16)
    return lax.rsqrt((_dot(hi, ones_bd) + _dot(lo, ones_bd)) * (1.0 / HD_A) + EPS)


def _rope128(y, cos, sa, sb):
    return y * cos + pltpu.roll(y, LANES - 16, 1) * sa + pltpu.roll(y, 16, 1) * sb


def _qkv_kernel(*refs, nq, nk, rope, kv_f32, nsub):
    it = iter(refs)
    x_ref, mod_ref, g_ref, w_ref, qg_ref, kg_ref = (next(it) for _ in range(6))
    if rope:
        cos_ref, sa_ref, sb_ref = next(it), next(it), next(it)
    q_ref, k_ref, v_ref = next(it), next(it), next(it)
    kf_ref, vf_ref = (next(it), next(it)) if kv_f32 else (None, None)
    wbf_ref, qkv_ref = next(it), next(it)

    @pl.when(pl.program_id(0) == 0)
    def _():
        wbf_ref[...] = w_ref[...].astype(BF16)

    m = mod_ref[...]
    tm = x_ref.shape[0]
    parts = [slice(i * (tm // nsub), (i + 1) * (tm // nsub)) for i in range(nsub)]
    for rows in parts:
        h = (_rms(x_ref[rows, :]) * g_ref[...] * (1.0 + m[:, D:2 * D]) + m[:, 0:D]).astype(BF16)
        qkv_ref[rows, :] = _dot(h, wbf_ref[...])
    q_gain = qg_ref[...] * (HD_A ** -0.5 * LOG2E)
    for rows in parts:
        if rope:
            cos, sa, sb = cos_ref[rows, :], sa_ref[rows, :], sb_ref[rows, :]
        for j in range((nq + nk) // LANES):
            x = qkv_ref[rows, j * LANES:(j + 1) * LANES]
            is_q = j < nq // LANES
            y = x * _group_inv_rms(x, on_mxu=rope) * (q_gain if is_q else kg_ref[...])
            if rope:
                y = _rope128(y, cos, sa, sb)
            if is_q:
                q_ref[rows, j * LANES:(j + 1) * LANES] = y.astype(BF16)
            else:
                jj = j - nq // LANES
                k_ref[rows, jj * LANES:(jj + 1) * LANES] = y.astype(BF16)
                if kv_f32:
                    kf_ref[rows, jj * LANES:(jj + 1) * LANES] = y
        v = qkv_ref[rows, nq + nk:]
        v_ref[rows, :] = v.astype(BF16)
        if kv_f32:
            vf_ref[rows, :] = v


def _rope_tables(t):
    n_freq = HD_A // 4
    inv_freq = ROPE_THETA ** (-jnp.arange(n_freq, dtype=F32) / n_freq)
    pos_row = jnp.repeat(jnp.arange(t // GRID_W), GRID_W).astype(F32)
    pos_col = jnp.tile(jnp.arange(GRID_W), t // GRID_W).astype(F32)
    ang_r = pos_row[:, None] * inv_freq[None, :]
    ang_c = pos_col[:, None] * inv_freq[None, :]
    z = jnp.zeros_like(ang_r)
    cos64 = jnp.concatenate([jnp.cos(ang_r), jnp.cos(ang_r), jnp.cos(ang_c), jnp.cos(ang_c)], axis=-1)
    sa64 = jnp.concatenate([-jnp.sin(ang_r), z, -jnp.sin(ang_c), z], axis=-1)
    sb64 = jnp.concatenate([z, jnp.sin(ang_r), z, jnp.sin(ang_c)], axis=-1)
    return tuple(jnp.tile(a, (1, 2)) for a in (cos64, sa64, sb64))


def _qkv_project(x, mod3, mod_base, rows_per_mod, g, w, qg, kg, nq, nk, nv, t, rope, kv_f32, tm=512):
    r = x.shape[0]
    n = nq + nk + nv
    tile2 = lambda a: jnp.tile(a.reshape(1, HD_A), (1, 2))
    row = lambda width: pl.BlockSpec((tm, width), lambda i: (i, 0))
    args = [x, mod3, g.reshape(1, D), w, tile2(qg), tile2(kg)]
    in_specs = [row(D),
                pl.BlockSpec((None, 1, 6 * D), lambda i: (mod_base + (i * tm) // rows_per_mod, 0, 0)),
                pl.BlockSpec((1, D), lambda i: (0, 0)),
                pl.BlockSpec((D, n), lambda i: (0, 0), pipeline_mode=pl.Buffered(1)),
                pl.BlockSpec((1, LANES), lambda i: (0, 0)),
                pl.BlockSpec((1, LANES), lambda i: (0, 0))]
    if rope:
        args += list(_rope_tables(t))
        in_specs += [pl.BlockSpec((tm, LANES), lambda i: (i % (t // tm), 0))] * 3
    out_shape = [jax.ShapeDtypeStruct((r, nq), BF16), jax.ShapeDtypeStruct((r, nk), BF16),
                 jax.ShapeDtypeStruct((r, nv), BF16)]
    out_specs = [row(nq), row(nk), row(nv)]
    if kv_f32:
        out_shape += [jax.ShapeDtypeStruct((r, nk), F32), jax.ShapeDtypeStruct((r, nv), F32)]
        out_specs += [row(nk), row(nv)]
    return pl.pallas_call(
        functools.partial(_qkv_kernel, nq=nq, nk=nk, rope=rope, kv_f32=kv_f32, nsub=2),
        grid=(r // tm,),
        in_specs=in_specs, out_specs=out_specs, out_shape=out_shape,
        scratch_shapes=[pltpu.VMEM((D, n), BF16), pltpu.VMEM((tm, n), F32)],
        compiler_params=_cparams(1, 48),
        name="qkv_project",
    )(*args)


def _diff_attn_kernel(*refs, heads, ctx, lam_init, nsub):
    it = iter(refs)
    q_ref, k_ref, v_ref = next(it), next(it), next(it)
    if ctx:
        ck_ref, cv_ref = next(it), next(it)
    lq_ref, lk_ref, sg_ref, o_ref = next(it), next(it), next(it), next(it)
    e = jnp.exp(jnp.sum(lq_ref[...] * lk_ref[...], axis=-1, keepdims=True))
    lam = e[0:1, :] - e[1:2, :] + lam_init
    lane = lax.broadcasted_iota(I32, (1, LANES), 1)
    tqs = q_ref.shape[0] // nsub
    units = [(slice(h * LANES, (h + 1) * LANES), slice(j * tqs, (j + 1) * tqs))
             for h in range(heads) for j in range(nsub)]

    def scores(u):
        sl, rows = units[u]
        q = q_ref[rows, sl]
        out = []
        for c in range(2):
            qc = jnp.where((lane < HD_A) if c == 0 else (lane >= HD_A), q, jnp.zeros_like(q))
            out.append((_dot_nt(qc, k_ref[:, sl]), _dot_nt(qc, ck_ref[:, sl].astype(BF16)) if ctx else None))
        return out

    def weights(sc2):
        ps, ls = [], []
        for s, sc in sc2:
            m = jnp.max(s, axis=-1, keepdims=True)
            if ctx:
                m = jnp.maximum(m, jnp.max(sc, axis=-1, keepdims=True))
            p = jnp.exp2(s - m)
            l = jnp.sum(p, axis=-1, keepdims=True)
            pc = None
            if ctx:
                pc = jnp.exp2(sc - m)
                l = l + jnp.sum(pc, axis=-1, keepdims=True)
            ps.append((p, pc))
            ls.append(l)
        ratio = lam * ls[0] / ls[1]
        w_lat = (ps[0][0] - ratio * ps[1][0]).astype(BF16)
        w_ctx = (ps[0][1] - ratio * ps[1][1]).astype(BF16) if ctx else None
        return w_lat, w_ctx, 1.0 / ls[0]

    def values(u, w):
        sl, rows = units[u]
        w_lat, w_ctx, inv_l0 = w
        o = _dot(w_lat, v_ref[:, sl])
        if ctx:
            o = o + _dot(w_ctx, cv_ref[:, sl].astype(BF16))
        o_ref[rows, sl] = (_rms(o * inv_l0) * sg_ref[...] * (1.0 - lam_init)).astype(BF16)

    if not ctx:
        s_ref = next(it)
        tk = k_ref.shape[0]
        for u in range(len(units)):
            for c, (s, _) in enumerate(scores(u)):
                s_ref[(2 * u + c) * tqs:(2 * u + c + 1) * tqs, :] = s
        s = s_ref[...]
        p = jnp.exp2(s - jnp.max(s, axis=-1, keepdims=True))
        l = jnp.sum(p, axis=-1, keepdims=True)
        for u in range(len(units)):
            r0, r1 = slice(2 * u * tqs, (2 * u + 1) * tqs), slice((2 * u + 1) * tqs, (2 * u + 2) * tqs)
            w = (p[r0, :] - (lam * l[r0, :] / l[r1, :]) * p[r1, :]).astype(BF16)
            values(u, (w, None, 1.0 / l[r0, :]))
        return
    n = len(units)
    sc = {u: scores(u) for u in range(min(2, n))}
    for u in range(n):
        w = weights(sc.pop(u))
        if u + 2 < n:
            sc[u + 2] = scores(u + 2)
        values(u, w)


def _diff_attention(q, k, v, cache, lam_q, lam_k, subln_g, lam_init, b, t, heads, tq, nsub):
    nq = t // tq
    hb = H_A // heads
    args = [q, k, v]
    in_specs = [pl.BlockSpec((tq, heads * LANES), lambda bi, hi, qi: (bi * nq + qi, hi)),
                pl.BlockSpec((t, heads * LANES), lambda bi, hi, qi: (bi, hi)),
                pl.BlockSpec((t, heads * LANES), lambda bi, hi, qi: (bi, hi))]
    if cache is not None:
        ck, cv, la = cache
        p = ck.shape[2]
        args += [ck, cv]
        in_specs += [pl.BlockSpec((None, None, p, heads * LANES), lambda bi, hi, qi: (bi, la, 0, hi))] * 2
    args += [lam_q, lam_k, subln_g.reshape(1, LANES)]
    in_specs += [pl.BlockSpec((2, HD_A), lambda bi, hi, qi: (0, 0)),
                 pl.BlockSpec((2, HD_A), lambda bi, hi, qi: (0, 0)),
                 pl.BlockSpec((1, LANES), lambda bi, hi, qi: (0, 0))]
    return pl.pallas_call(
        functools.partial(_diff_attn_kernel, heads=heads, ctx=cache is not None, lam_init=lam_init, nsub=nsub),
        grid=(b, hb, nq),
        in_specs=in_specs,
        out_specs=pl.BlockSpec((tq, heads * LANES), lambda bi, hi, qi: (bi * nq + qi, hi)),
        out_shape=jax.ShapeDtypeStruct((b * t, H_A * LANES), BF16),
        scratch_shapes=[] if cache is not None else [pltpu.VMEM((2 * heads * tq, t), F32)],
        compiler_params=_cparams(3, 48),
        name="diff_attention",
    )(*args)


def _both_halves(x, half):
    lane_half = lax.broadcasted_iota(I32, (1, LANES), 1) >> 6
    xm = jnp.where(lane_half == half, x, 0.0)
    return xm + pltpu.roll(xm, HD_B, 1)


def _value_with_ones(x, half):
    lane = lax.broadcasted_iota(I32, (1, LANES), 1)
    return jnp.where(lane < HD_B, _both_halves(x, half), jnp.where(lane == HD_B, 1.0, 0.0))


def _gqa_dense_body(q_ref, k_ref, v_ref, sink_ref, o_ref, s_ref, t):
    lane_half = lax.broadcasted_iota(I32, (1, LANES), 1) >> 6
    vds, sinks = [], []
    for kv in range(KV_B):
        kcol = slice((kv // 2) * LANES, (kv // 2 + 1) * LANES)
        kd = _both_halves(k_ref[:, kcol].astype(F32), kv % 2).astype(BF16)
        vds.append(_value_with_ones(v_ref[:, kcol].astype(F32), kv % 2).astype(BF16))
        for g in range(G_B):
            h = kv * G_B + g
            q = q_ref[:, (h // 2) * LANES:(h // 2 + 1) * LANES]
            qm = jnp.where(lane_half == h % 2, q, jnp.zeros_like(q))
            s_ref[h * t:(h + 1) * t, :] = _dot_nt(qm, kd)
            sinks.append(jnp.broadcast_to(sink_ref[kv:kv + 1, g:g + 1] * LOG2E, (t, 1)))
    s = s_ref[...]
    sink2 = jnp.concatenate(sinks, axis=0)
    m = jnp.maximum(jnp.max(s, axis=-1, keepdims=True), sink2)
    p = jnp.exp2(s - m).astype(BF16)
    tail = jnp.exp2(sink2 - m)
    for hp in range(H_B // 2):
        halves = []
        for h in (2 * hp, 2 * hp + 1):
            rows = slice(h * t, (h + 1) * t)
            ov = _dot(p[rows, :], vds[h // G_B])
            halves.append(ov * (1.0 / (ov[:, HD_B:HD_B + 1] + tail[rows, :])))
        o_ref[:, hp * LANES:(hp + 1) * LANES] = jnp.where(
            lane_half == 0, halves[0], pltpu.roll(halves[1], HD_B, 1)).astype(BF16)


def _gqa_kernel(*refs, kvs, windowed, ctx, t, tq):
    it = iter(refs)
    q_ref, k_ref, v_ref = next(it), next(it), next(it)
    if ctx:
        ck_ref, cv_ref = next(it), next(it)
    sink_ref, o_ref = next(it), next(it)
    lane_half = lax.broadcasted_iota(I32, (1, LANES), 1) >> 6
    if not windowed:
        _gqa_dense_body(q_ref, k_ref, v_ref, sink_ref, o_ref, next(it), t)
        return
    kd_ref, vd_ref, ckd_ref, cvd_ref = next(it), next(it), next(it), next(it)

    @pl.when(pl.program_id(2) == 0)
    def _():
        for half in range(2):
            kd_ref[half] = _both_halves(k_ref[...].astype(F32), half).astype(BF16)
            vd_ref[half] = _value_with_ones(v_ref[...].astype(F32), half).astype(BF16)
            ckd_ref[half] = _both_halves(ck_ref[...], half).astype(BF16)
            cvd_ref[half] = _value_with_ones(cv_ref[...], half).astype(BF16)

    span = tq + 2 * WINDOW
    q0 = pl.program_id(2) * tq
    start = pl.multiple_of(jnp.clip(q0 - WINDOW, 0, t - span), LANES)
    qpos = q0 + lax.broadcasted_iota(I32, (tq, span), 0)
    kpos = start + lax.broadcasted_iota(I32, (tq, span), 1)
    valid = jnp.abs(qpos - kpos) <= WINDOW
    s_ref = next(it)
    sinks = []
    for half in range(2):
        kd, ckd = kd_ref[half, pl.ds(start, span), :], ckd_ref[half]
        for g in range(G_B):
            hl = half * G_B + g
            q = q_ref[:, hl // 2 * LANES:(hl // 2 + 1) * LANES]
            qm = jnp.where(lane_half == hl % 2, q, jnp.zeros_like(q))
            s_ref[hl * tq:(hl + 1) * tq, 0:span] = jnp.where(valid, _dot_nt(qm, kd), -jnp.inf)
            s_ref[hl * tq:(hl + 1) * tq, span:] = _dot_nt(qm, ckd)
            sinks.append(jnp.broadcast_to(sink_ref[half:half + 1, g:g + 1] * LOG2E, (tq, 1)))
    s = s_ref[...]
    sink2 = jnp.concatenate(sinks, axis=0)
    m = jnp.maximum(jnp.max(s, axis=-1, keepdims=True), sink2)
    p = jnp.exp2(s - m).astype(BF16)
    tail = jnp.exp2(sink2 - m)
    for hp in range(G_B):
        halves = []
        for hl in (2 * hp, 2 * hp + 1):
            rows, half = slice(hl * tq, (hl + 1) * tq), hl // G_B
            ov = (_dot(p[rows, 0:span], vd_ref[half, pl.ds(start, span), :]) + _dot(p[rows, span:], cvd_ref[half]))
            halves.append(ov * (1.0 / (ov[:, HD_B:HD_B + 1] + tail[rows, :])))
        o_ref[:, hp * LANES:(hp + 1) * LANES] = jnp.where(
            lane_half == 0, halves[0], pltpu.roll(halves[1], HD_B, 1)).astype(BF16)


def _gqa_attention(q, k, v, cache, sink, b, t, windowed, tq):
    if windowed:
        kvs, nq = 2, t // tq
        grid = (b, KV_B // 2, nq)
        qw = 2 * G_B * HD_B
        q_spec = pl.BlockSpec((tq, qw), lambda bi, kv, qi: (bi * nq + qi, kv))
        kv_spec = pl.BlockSpec((t, LANES), lambda bi, kv, qi: (bi, kv))
        sink_arr = sink.reshape(KV_B // 2, 2, G_B)
        sink_spec = pl.BlockSpec((None, 2, G_B), lambda bi, kv, qi: (kv, 0, 0))
        o_spec = pl.BlockSpec((tq, qw), lambda bi, kv, qi: (bi * nq + qi, kv))
    else:
        kvs, tq = KV_B, t
        grid = (b, 1, 1)
        q_spec = pl.BlockSpec((t, H_B * HD_B), lambda bi, kv, qi: (bi, 0))
        kv_spec = pl.BlockSpec((t, KV_B * HD_B), lambda bi, kv, qi: (bi, 0))
        sink_arr = sink.reshape(KV_B, G_B)
        sink_spec = pl.BlockSpec((KV_B, G_B), lambda bi, kv, qi: (0, 0))
        o_spec = pl.BlockSpec((t, H_B * HD_B), lambda bi, kv, qi: (bi, 0))
    args, in_specs = [q, k, v], [q_spec, kv_spec, kv_spec]
    if cache is not None:
        ck, cv, lb = cache
        p = ck.shape[2]
        args += [ck, cv]
        in_specs += [pl.BlockSpec((None, None, p, LANES), lambda bi, kv, qi: (bi, lb, 0, kv))] * 2
    args.append(sink_arr)
    in_specs.append(sink_spec)
    scratch = [pltpu.VMEM((H_B * t, t), F32)]
    if windowed:
        scratch = ([pltpu.VMEM((2, t, LANES), BF16)] * 2 + [pltpu.VMEM((2, p, LANES), BF16)] * 2
                   + [pltpu.VMEM((2 * G_B * tq, tq + 2 * WINDOW + p), F32)])
    return pl.pallas_call(
        functools.partial(_gqa_kernel, kvs=kvs, windowed=windowed, ctx=cache is not None, t=t, tq=tq),
        grid=grid, in_specs=in_specs, out_specs=o_spec,
        out_shape=jax.ShapeDtypeStruct((b * t, H_B * HD_B), BF16),
        scratch_shapes=scratch,
        compiler_params=_cparams(3, 48),
        name="gqa_attention",
    )(*args)


def _lru_conv_gates(xp_ref, xc_ref, xn_ref, cw_ref, cb_ref, wg_ref, bg_ref, lam_ref, a_ref, u_ref, ti, nt, tt):
    nl = -lam_ref[...]
    sp = jnp.maximum(nl, 0.0) + jnp.log1p(jnp.exp(-jnp.abs(nl)))
    cur = xc_ref[...]
    prev = xp_ref[...] * (ti > 0).astype(F32)
    nxt = xn_ref[...] * (ti < nt - 1).astype(F32)
    row = lax.broadcasted_iota(I32, (tt, 1), 0)
    xm1 = jnp.where(row == 0, prev[7:8, :], pltpu.roll(cur, 1, 0))
    xm2 = jnp.where(row == 0, prev[6:7, :], jnp.where(row == 1, prev[7:8, :], pltpu.roll(cur, 2, 0)))
    xp1 = jnp.where(row == tt - 1, nxt[0:1, :], pltpu.roll(cur, tt - 1, 0))
    cw = cw_ref[...]
    xc = xm2 * cw[0:1, :] + xm1 * cw[1:2, :] + cur * cw[2:3, :] + xp1 * cw[3:4, :] + cb_ref[...]
    xcb = xc.astype(BF16)
    for j in range(D_RNN // MXU_DIM):
        cs = slice(j * MXU_DIM, (j + 1) * MXU_DIM)
        z = _dot(xcb[:, cs], wg_ref[j].astype(BF16))
        r = _sigmoid(z[:, :MXU_DIM] + bg_ref[0:1, cs])
        i = _sigmoid(z[:, MXU_DIM:] + bg_ref[1:2, cs])
        a = jnp.exp(-LRU_C * r * sp[:, cs])
        a_ref[:, cs] = a
        u_ref[:, cs] = jnp.sqrt(1.0 - a * a) * i * xc[:, cs]


def _tile_scan(a_ref, u_ref, hs_ref, h, tt, reverse):
    row = lax.broadcasted_iota(I32, (8, 1), 0)
    ng = tt // 8
    for g in (range(ng - 1, -1, -1) if reverse else range(ng)):
        rows = slice(g * 8, (g + 1) * 8)
        a8, u8 = a_ref[rows, :], u_ref[rows, :]
        for s in (1, 2, 4):
            keep = (row < 8 - s) if reverse else (row >= s)
            shift = 8 - s if reverse else s
            u8 = a8 * jnp.where(keep, pltpu.roll(u8, shift, 0), 0.0) + u8
            a8 = a8 * jnp.where(keep, pltpu.roll(a8, shift, 0), 1.0)
        h8 = a8 * h + u8
        hs_ref[rows, :] = h8
        h = h8[0:1, :] if reverse else h8[7:8, :]
    return h


def _lru_fwd_kernel(xp_ref, xc_ref, xn_ref, cw_ref, cb_ref, wg_ref, bg_ref, lam_ref, h0_ref,
                    hf_ref, last_ref, carry_ref, a_ref, u_ref, *, tt):
    ti, nt = pl.program_id(1), pl.num_programs(1)

    @pl.when(ti == 0)
    def _():
        carry_ref[...] = h0_ref[...]

    _lru_conv_gates(xp_ref, xc_ref, xn_ref, cw_ref, cb_ref, wg_ref, bg_ref, lam_ref, a_ref, u_ref, ti, nt, tt)
    h = _tile_scan(a_ref, u_ref, hf_ref, carry_ref[...], tt, reverse=False)
    carry_ref[...] = h
    last_ref[...] = h


def _lru_bwd_kernel(xp_ref, xc_ref, xn_ref, cw_ref, cb_ref, wg_ref, bg_ref, lam_ref, h0_ref,
                    gate_ref, hf_ref, x_ref, mod_ref, g2_ref, w_ref,
                    xo_ref, h2_ref, last_ref, carry_ref, a_ref, u_ref, hb_ref, wbf_ref, *, tt):
    ti, nt = pl.program_id(1), pl.num_programs(1)

    @pl.when((pl.program_id(0) == 0) & (ti == 0))
    def _():
        wbf_ref[...] = w_ref[...].astype(BF16)

    @pl.when(ti == 0)
    def _():
        carry_ref[...] = h0_ref[...]

    tr = nt - 1 - ti
    _lru_conv_gates(xp_ref, xc_ref, xn_ref, cw_ref, cb_ref, wg_ref, bg_ref, lam_ref, a_ref, u_ref, tr, nt, tt)
    h = _tile_scan(a_ref, u_ref, hb_ref, carry_ref[...], tt, reverse=True)
    carry_ref[...] = h
    last_ref[...] = h
    g = gate_ref[...]
    gelu = 0.5 * g * (1.0 + jnp.tanh(math.sqrt(2.0 / math.pi) * (g + 0.044715 * (g * g * g))))
    y = (gelu * (hf_ref[...] + hb_ref[...])).astype(BF16)
    _proj_tail(_dot(y, wbf_ref[...]), x_ref, mod_ref, g2_ref, xo_ref, h2_ref)


def _lru_mixer(gx, x, state, conv_w, conv_b, w_rg, b_rg, w_ig, b_ig, lam, w_out, mod3, mod_base, per_batch_mod,
               g2, b, t, tt=512):
    tt = min(tt, t)
    nt = t // tt
    per_tile = MXU_DIM // BW_C
    eye = jnp.eye(per_tile, dtype=F32)

    def bd(w):
        w4 = w.reshape(N_BLK_C // per_tile, per_tile, BW_C, BW_C)
        return (w4[:, :, :, None, :] * eye[None, :, None, :, None]).reshape(-1, MXU_DIM, MXU_DIM)

    full = lambda shape: pl.BlockSpec(shape, lambda bi, ti: (0,) * len(shape))
    finals, xo, h2, hf = [], None, None, None
    for d in range(2):
        tile = (lambda ti: ti) if d == 0 else (lambda ti: nt - 1 - ti)
        x_spec = lambda f: pl.BlockSpec((tt, D_RNN), f)
        row = lambda bi, ti, tile=tile: (bi * nt + tile(ti), 0)
        halo = lambda f: pl.BlockSpec((8, D_RNN), f)
        in_specs = [halo(lambda bi, ti, tile=tile: (jnp.maximum((bi * nt + tile(ti)) * (tt // 8) - 1, 0), 1)),
                    x_spec(lambda bi, ti, tile=tile: (bi * nt + tile(ti), 1)),
                    halo(lambda bi, ti, tile=tile: (jnp.minimum((bi * nt + tile(ti) + 1) * (tt // 8), b * t // 8 - 1), 1)),
                    full((4, D_RNN)), full((1, D_RNN)), full((4, MXU_DIM, 2 * MXU_DIM)),
                    full((2, D_RNN)), full((1, D_RNN)),
                    pl.BlockSpec((None, 1, D_RNN), lambda bi, ti: (bi, 0, 0))]
        args = [gx, gx, gx, conv_w, conv_b.reshape(1, D_RNN),
                jnp.concatenate([bd(w_rg[d]), bd(w_ig[d])], axis=-1), jnp.stack([b_rg[d], b_ig[d]]),
                lam[d].reshape(1, D_RNN), state[:, d].reshape(b, 1, D_RNN)]
        scratch = [pltpu.VMEM((1, D_RNN), F32), pltpu.VMEM((tt, D_RNN), F32), pltpu.VMEM((tt, D_RNN), F32)]
        last_spec = pl.BlockSpec((None, 1, D_RNN), lambda bi, ti: (bi, 0, 0))
        last_shape = jax.ShapeDtypeStruct((b, 1, D_RNN), F32)
        if d == 0:
            hf, last_f = pl.pallas_call(
                functools.partial(_lru_fwd_kernel, tt=tt),
                grid=(b, nt), in_specs=in_specs,
                out_specs=[pl.BlockSpec((tt, D_RNN), row), last_spec],
                out_shape=[jax.ShapeDtypeStruct((b * t, D_RNN), F32), last_shape],
                scratch_shapes=scratch,
                compiler_params=_cparams(2, 48),
                name="lru_fwd",
            )(*args)
        else:
            in_specs += [pl.BlockSpec((tt, D_RNN), row), pl.BlockSpec((tt, D_RNN), row), pl.BlockSpec((tt, D), row),
                         pl.BlockSpec((None, 1, 6 * D), lambda bi, ti: (mod_base + bi * per_batch_mod, 0, 0)),
                         full((1, D)), full((D_RNN, D))]
            args += [gx, hf, x, mod3, g2.reshape(1, D), w_out]
            xo, h2, last_b = pl.pallas_call(
                functools.partial(_lru_bwd_kernel, tt=tt),
                grid=(b, nt), in_specs=in_specs,
                out_specs=[pl.BlockSpec((tt, D), row), pl.BlockSpec((tt, D), row), last_spec],
                out_shape=[jax.ShapeDtypeStruct((b * t, D), F32), jax.ShapeDtypeStruct((b * t, D), BF16), last_shape],
                scratch_shapes=scratch + [pltpu.VMEM((tt, D_RNN), F32), pltpu.VMEM((D_RNN, D), BF16)],
                compiler_params=_cparams(2, 48),
                name="lru_bwd_proj",
            )(*args)
    return xo, h2, jnp.concatenate([last_f, last_b], axis=1)


def _select_kernel(h_ref, wr_ref, slot_ref, aff_ref, tri_ref, *, bs, t, cap):
    @pl.when(pl.program_id(0) == 0)
    def _():
        r = lax.broadcasted_iota(I32, (t, t), 0)
        c = lax.broadcasted_iota(I32, (t, t), 1)
        tri_ref[...] = jnp.where(r < c, 1.0, 0.0).astype(BF16)

    wr = wr_ref[...].astype(BF16)
    affs = []
    for s in range(bs):
        logits = _dot_nt(wr, h_ref[s])
        ex = jnp.exp(logits - jnp.max(logits, axis=0, keepdims=True))
        affs.append(ex / jnp.sum(ex, axis=0, keepdims=True))
    aff = jnp.concatenate(affs, axis=0) if bs > 1 else affs[0]
    bits = pltpu.bitcast(aff, I32)
    count = lambda mask: jnp.sum(jnp.where(mask, 1.0, 0.0), axis=-1, keepdims=True)
    th = jnp.zeros((bs * N_EXPERTS, 1), I32)
    for bit in range(30, -1, -1):
        cand = th | (1 << bit)
        th = jnp.where(count(bits >= cand) >= cap, cand, th)
    gt, eq = bits > th, bits == th
    need = cap - count(gt)
    lane = lax.broadcasted_iota(I32, (1, t), 1)
    lim = jnp.zeros((bs * N_EXPERTS, 1), I32)
    for bit in range(t.bit_length() - 1, -1, -1):
        cand = lim | (1 << bit)
        ok = (cand <= t) & (count(eq & (lane < cand)) <= need)
        lim = jnp.where(ok, cand, lim)
    sel = gt | (eq & (lane < lim))
    pos = _dot(jnp.where(sel, 1.0, 0.0).astype(BF16), tri_ref[...])
    slot = jnp.where(sel, pos.astype(I32), -1)
    for s in range(bs):
        slot_ref[s] = slot[s * N_EXPERTS:(s + 1) * N_EXPERTS, :]
        aff_ref[s] = aff[s * N_EXPERTS:(s + 1) * N_EXPERTS, :]


def _moe_select(h2, w_router, b, t, cap, bs):
    return pl.pallas_call(
        functools.partial(_select_kernel, bs=bs, t=t, cap=cap),
        grid=(b // bs,),
        in_specs=[pl.BlockSpec((bs, t, D), lambda i: (i, 0, 0)),
                  pl.BlockSpec((N_EXPERTS, D), lambda i: (0, 0))],
        out_specs=[pl.BlockSpec((bs, N_EXPERTS, t), lambda i: (i, 0, 0))] * 2,
        out_shape=[jax.ShapeDtypeStruct((b, N_EXPERTS, t), I32), jax.ShapeDtypeStruct((b, N_EXPERTS, t), F32)],
        scratch_shapes=[pltpu.VMEM((t, t), BF16)],
        compiler_params=_cparams(1, 48),
        name="moe_select",
    )(h2, w_router.T)


def _gather_kernel(slot_ref, aff_ref, h_ref, xs_ref, gc_ref, *, bq, cap, eg):
    j = lax.broadcasted_iota(I32, (cap, 1), 0)
    for s in range(bq):
        h = h_ref[s]
        for e0 in range(0, N_EXPERTS, eg):
            hots = [slot_ref[s, e:e + 1, :] == j for e in range(e0, e0 + eg)]
            p = jnp.concatenate([jnp.where(o, 1.0, 0.0).astype(BF16) for o in hots], axis=0)
            xs = _dot(p, h).astype(BF16)
            for k, e in enumerate(range(e0, e0 + eg)):
                xs_ref[e, s * cap:(s + 1) * cap, :] = xs[k * cap:(k + 1) * cap, :]
                gc_ref[e, s * cap:(s + 1) * cap, :] = jnp.sum(jnp.where(hots[k], aff_ref[s, e:e + 1, :], 0.0),
                                                              axis=-1, keepdims=True)


def _moe_gather(slot, aff, h3, b, t, cap, bq, eg):
    sa_spec = pl.BlockSpec((bq, N_EXPERTS, t), lambda i: (i, 0, 0))
    return pl.pallas_call(
        functools.partial(_gather_kernel, bq=bq, cap=cap, eg=eg),
        grid=(b // bq,),
        in_specs=[sa_spec, sa_spec, pl.BlockSpec((bq, t, D), lambda i: (i, 0, 0))],
        out_specs=[pl.BlockSpec((N_EXPERTS, bq * cap, D), lambda i: (0, i, 0)),
                   pl.BlockSpec((N_EXPERTS, bq * cap, 1), lambda i: (0, i, 0))],
        out_shape=[jax.ShapeDtypeStruct((N_EXPERTS, b * cap, D), BF16),
                   jax.ShapeDtypeStruct((N_EXPERTS, b * cap, 1), F32)],
        compiler_params=_cparams(1, 56),
        name="moe_gather",
    )(slot, aff, h3)


def _ffn_kernel(xs_ref, gc_ref, wg_ref, wu_ref, wd_ref, ys_ref, wgb_ref, wub_ref, wdb_ref):
    @pl.when(pl.program_id(1) == 0)
    def _():
        wgb_ref[...] = wg_ref[...].astype(BF16)
        wub_ref[...] = wu_ref[...].astype(BF16)
        wdb_ref[...] = wd_ref[...].astype(BF16)

    xs = xs_ref[...]
    acc = None
    for c in range(0, wgb_ref.shape[1], MXU_DIM):
        cs = slice(c, c + MXU_DIM)
        zg = _dot(xs, wgb_ref[:, cs])
        hid = (zg * _sigmoid(zg) * _dot(xs, wub_ref[:, cs])).astype(BF16)
        part = _dot(hid, wdb_ref[cs, :])
        acc = part if acc is None else acc + part
    ys_ref[...] = (acc * gc_ref[...]).astype(BF16)


def _moe_ffn(xs, gc, w_gate, w_up, w_down, layer, tmf=1024):
    r, f = xs.shape[1], w_gate.shape[-1]
    w_spec = lambda d0, d1: pl.BlockSpec((None, None, d0, d1), lambda e, i: (layer, e, 0, 0))
    return pl.pallas_call(
        _ffn_kernel,
        grid=(N_EXPERTS, r // tmf),
        in_specs=[pl.BlockSpec((None, tmf, D), lambda e, i: (e, i, 0)),
                  pl.BlockSpec((None, tmf, 1), lambda e, i: (e, i, 0)),
                  w_spec(D, f), w_spec(D, f), w_spec(f, D)],
        out_specs=pl.BlockSpec((None, tmf, D), lambda e, i: (e, i, 0)),
        out_shape=jax.ShapeDtypeStruct((N_EXPERTS, r, D), BF16),
        scratch_shapes=[pltpu.VMEM((D, f), BF16), pltpu.VMEM((D, f), BF16), pltpu.VMEM((f, D), BF16)],
        compiler_params=_cparams(2, 56),
        name="moe_ffn",
    )(xs, gc, w_gate, w_up, w_down)


def _scatter_rows(slot, width, ys):
    n = N_EXPERTS * width
    e_of = lax.broadcasted_iota(I32, (N_EXPERTS, n), 1) >> (width.bit_length() - 1)
    rep = jnp.where(e_of == lax.broadcasted_iota(I32, (N_EXPERTS, n), 0), 1.0, 0.0).astype(BF16)
    slot_rep = _dot(slot.astype(F32).astype(BF16), rep)
    jn = (lax.broadcasted_iota(I32, (1, n), 1) & (width - 1)).astype(F32)
    return _dot(jnp.where(slot_rep == jn, 1.0, 0.0).astype(BF16), ys)


def _combine_kernel(slot_ref, ys_ref, x_ref, mod_ref, o_ref, *, cap):
    y = _scatter_rows(slot_ref[...], cap, ys_ref[...].reshape(N_EXPERTS * cap, D))
    o_ref[...] = x_ref[...] + mod_ref[:, 5 * D:6 * D] * y


def _combine_win_kernel(st_s, ok_s, slot_ref, stv_ref, ys_ref, x_ref, mod_ref, o_ref, *ysw_refs, cap, win, nt, tt):
    ga2 = mod_ref[:, 5 * D:6 * D]
    for sub, ysw_ref in enumerate(ysw_refs):
        tile = (pl.program_id(0) * nt + pl.program_id(1)) * len(ysw_refs) + sub
        rows = slice(sub * tt, (sub + 1) * tt)

        @pl.when(ok_s[tile] != 0)
        def _(tile=tile, rows=rows, ysw_ref=ysw_ref, sub=sub):
            for e in range(N_EXPERTS):
                st = pl.multiple_of(st_s[tile * N_EXPERTS + e], 16)
                ysw_ref[e * win:(e + 1) * win, :] = ys_ref[e, pl.ds(st, win), :]
            slot = slot_ref[rows, :]
            rel = jnp.where(slot >= 0, slot - stv_ref[sub], -1)
            o_ref[rows, :] = x_ref[rows, :] + ga2 * _scatter_rows(rel, win, ysw_ref[...])

        @pl.when(ok_s[tile] == 0)
        def _(rows=rows):
            y = _scatter_rows(slot_ref[rows, :], cap, ys_ref[...].reshape(N_EXPERTS * cap, D))
            o_ref[rows, :] = x_ref[rows, :] + ga2 * y


def _moe_combine_windowed(slot, ys, x, mod3, mod_base, per_batch_mod, b, t, cap, tt=256, win=64, nsub=2):
    nt = t // tt
    cnt = (slot >= 0).reshape(b, N_EXPERTS, nt, tt).sum(-1).astype(I32)
    start = jnp.cumsum(cnt, axis=-1) - cnt
    st = jnp.clip((start // 16) * 16, 0, cap - win)
    ok = jnp.all(start + cnt <= st + win, axis=1).astype(I32)
    st_t = jnp.swapaxes(st, 1, 2)
    ng = nt // nsub
    grid_spec = pltpu.PrefetchScalarGridSpec(
        num_scalar_prefetch=2,
        grid=(b, ng),
        in_specs=[pl.BlockSpec((None, nsub * tt, N_EXPERTS), lambda bi, ti, *_: (bi, ti, 0)),
                  pl.BlockSpec((None, nsub, 1, N_EXPERTS), lambda bi, ti, *_: (bi, ti, 0, 0)),
                  pl.BlockSpec((N_EXPERTS, None, cap, D), lambda bi, ti, *_: (0, bi, 0, 0)),
                  pl.BlockSpec((nsub * tt, D), lambda bi, ti, *_: (bi * ng + ti, 0)),
                  pl.BlockSpec((None, 1, 6 * D), lambda bi, ti, *_: (mod_base + bi * per_batch_mod, 0, 0))],
        out_specs=pl.BlockSpec((nsub * tt, D), lambda bi, ti, *_: (bi * ng + ti, 0)),
        scratch_shapes=[pltpu.VMEM((N_EXPERTS * win, D), BF16)] * nsub)
    return pl.pallas_call(
        functools.partial(_combine_win_kernel, cap=cap, win=win, nt=ng, tt=tt),
        grid_spec=grid_spec,
        out_shape=jax.ShapeDtypeStruct((b * t, D), F32),
        compiler_params=_cparams(2, 56),
        name="moe_combine_win",
    )(st_t.reshape(-1), ok.reshape(-1), jnp.swapaxes(slot, 1, 2), st_t.reshape(b, nt, 1, N_EXPERTS), ys, x, mod3)


def _moe_combine(slot_t, ys, x, mod3, mod_base, per_batch_mod, b, t, cap, tt=512):
    tt = min(tt, t)
    nt = t // tt
    return pl.pallas_call(
        functools.partial(_combine_kernel, cap=cap),
        grid=(b, nt),
        in_specs=[pl.BlockSpec((None, tt, N_EXPERTS), lambda bi, ti: (bi, ti, 0)),
                  pl.BlockSpec((N_EXPERTS, None, cap, D), lambda bi, ti: (0, bi, 0, 0)),
                  pl.BlockSpec((tt, D), lambda bi, ti: (bi * nt + ti, 0)),
                  pl.BlockSpec((None, 1, 6 * D), lambda bi, ti: (mod_base + bi * per_batch_mod, 0, 0))],
        out_specs=pl.BlockSpec((tt, D), lambda bi, ti: (bi * nt + ti, 0)),
        out_shape=jax.ShapeDtypeStruct((b * t, D), F32),
        compiler_params=_cparams(2, 48),
        name="moe_combine",
    )(slot_t, ys, x, mod3)


def _ec_moe(x, h2, mod3, mod_base, per_batch_mod, moe_w, layer, b, t, bs, bg):
    w_router, w_gate, w_up, w_down = moe_w
    cap = EC_FACTOR * t // N_EXPERTS
    h3 = h2.reshape(b, t, D)
    slot, aff = _moe_select(h3, w_router[layer], b, t, cap, bs)
    xs, gc = _moe_gather(slot, aff, h3, b, t, cap, bq=bg, eg=4 if cap >= MXU_DIM else N_EXPERTS)
    ys = _moe_ffn(xs, gc, w_gate, w_up, w_down, layer).reshape(N_EXPERTS, b, cap, D)
    if cap >= MXU_DIM:
        return _moe_combine_windowed(slot, ys, x, mod3, mod_base, per_batch_mod, b, t, cap)
    return _moe_combine(jnp.swapaxes(slot, 1, 2), ys, x, mod3, mod_base, per_batch_mod, b, t, cap)


def kernel(x_prompt, x_sample, cache_a_k, cache_a_v, cache_b_k, cache_b_v, state_c_h, c, c_ctx, ada_w, ada_b, norm_mix_g, norm_ffn_g, a_w_in, a_q_norm_g, a_k_norm_g, a_lam_q, a_lam_k, a_subln_g, a_w_out, b_w_in, b_q_norm_g, b_k_norm_g, b_sink, b_w_out, c_w_in, c_conv_w, c_conv_b, c_w_rg, c_b_rg, c_w_ig, c_b_ig, c_lam, c_w_out, moe_w_router, moe_w_gate, moe_w_up, moe_w_down):
    bp, tp, _ = x_prompt.shape
    bs_, ts, _ = x_sample.shape
    past = cache_a_k.shape[2]
    cvec = jnp.concatenate([c, c_ctx[None, :], jnp.zeros((16 - bs_ - 1, D), F32)], axis=0)
    mods = _modulation(cvec, ada_w, ada_b)
    groups = {"p": (bp, tp, bs_, 0), "s": (bs_, ts, 0, 1)}
    xs = {"p": x_prompt.reshape(bp * tp, D), "s": x_sample.reshape(bs_ * ts, D)}
    ck_a = cache_a_k.reshape(bs_, -1, past, H_A * 2 * HD_A)
    cv_a = cache_a_v.reshape(bs_, -1, past, H_A * 2 * HD_A)
    ck_b = cache_b_k.reshape(bs_, -1, past, KV_B * HD_B)
    cv_b = cache_b_v.reshape(bs_, -1, past, KV_B * HD_B)
    moe_w = (moe_w_router, moe_w_gate, moe_w_up, moe_w_down)
    new_a_k, new_a_v, new_b_k, new_b_v, new_c_h = [], [], [], [], []
    ia = ib = ic = 0
    for l in range(DEPTH):
        mod3 = mods[l].reshape(16, 1, 6 * D)
        kind = l % 3
        for key in ("p", "s"):
            b, t, mbase, per_b = groups[key]
            rows_per_mod = t if per_b else b * t
            x = xs[key]
            sample = key == "s"
            if kind == 0:
                lam_init = 0.8 - 0.6 * math.exp(-0.3 * l)
                outs = _qkv_project(x, mod3, mbase, rows_per_mod, norm_mix_g[l], a_w_in[ia], a_q_norm_g[ia],
                                    a_k_norm_g[ia], D, D, D, t, rope=sample, kv_f32=not sample)
                q, k, v = outs[:3]
                if not sample:
                    new_a_k.append(outs[3].reshape(b, t, H_A, 2 * HD_A))
                    new_a_v.append(outs[4].reshape(b, t, H_A, 2 * HD_A))
                o = _diff_attention(q, k, v, (ck_a, cv_a, ia) if sample else None, a_lam_q[ia], a_lam_k[ia],
                                    a_subln_g[ia], lam_init, b, t, heads=1 if sample else H_A,
                                    tq=2048 if sample else t, nsub=16 if sample else 1)
                x, h2 = _proj_residual(o, x, mod3, mbase, rows_per_mod, norm_ffn_g[l], a_w_out[ia])
            elif kind == 1:
                nq, nk = H_B * HD_B, KV_B * HD_B
                outs = _qkv_project(x, mod3, mbase, rows_per_mod, norm_mix_g[l], b_w_in[ib], b_q_norm_g[ib],
                                    b_k_norm_g[ib], nq, nk, nk, t, rope=sample, kv_f32=not sample)
                q, k, v = outs[:3]
                if not sample:
                    new_b_k.append(outs[3].reshape(b, t, KV_B, HD_B))
                    new_b_v.append(outs[4].reshape(b, t, KV_B, HD_B))
                o = _gqa_attention(q, k, v, (ck_b, cv_b, ib) if sample else None, b_sink[ib], b, t,
                                   windowed=sample, tq=256)
                x, h2 = _proj_residual(o, x, mod3, mbase, rows_per_mod, norm_ffn_g[l], b_w_out[ib])
            else:
                gx = _norm_mod_matmul(x, mod3, mbase, rows_per_mod, norm_mix_g[l], c_w_in[ic])
                state = state_c_h[:, ic] if sample else jnp.zeros((b, 2, D_RNN), F32)
                x, h2, finals = _lru_mixer(gx, x, state, c_conv_w[ic], c_conv_b[ic], c_w_rg[ic], c_b_rg[ic],
                                           c_w_ig[ic], c_b_ig[ic], c_lam[ic], c_w_out[ic], mod3, mbase, per_b,
                                           norm_ffn_g[l], b, t)
                if not sample:
                    new_c_h.append(finals)
            xs[key] = _ec_moe(x, h2, mod3, mbase, per_b, moe_w, l, b, t,
                              bs=4 if sample else 16, bg=1 if sample else 8)
        ia, ib, ic = ia + (kind == 0), ib + (kind == 1), ic + (kind == 2)
    return (xs["p"].reshape(bp, tp, D), xs["s"].reshape(bs_, ts, D),
            jnp.stack(new_a_k, axis=1), jnp.stack(new_a_v, axis=1),
            jnp.stack(new_b_k, axis=1), jnp.stack(new_b_v, axis=1), jnp.stack(new_c_h, axis=1))
```

```python
import functools
import math

import jax
import jax.numpy as jnp
from jax import lax
from jax.experimental import pallas as pl
from jax.experimental.pallas import tpu as pltpu

F32, BF16, I32 = jnp.float32, jnp.bfloat16, jnp.int32

D = 1024
DEPTH = 4
GRID_W = 64
H_A, HD_A = 8, 64
H_B, KV_B, G_B, HD_B = 16, 4, 4, 64
WINDOW = 128
D_RNN = 1024
N_BLK_C, BW_C = 16, 64
LRU_C = 8.0
N_EXPERTS = 16
EC_FACTOR = 2
ROPE_THETA = 10000.0
EPS = 1e-6
LOG2E = math.log2(math.e)
LANES = 128
MXU_DIM = 256
MIB = 1024 * 1024


def _cparams(n_axes, vmem_mib=48):
    return pltpu.CompilerParams(dimension_semantics=("arbitrary",) * n_axes,
                                vmem_limit_bytes=vmem_mib * MIB)


def _sigmoid(x):
    return 0.5 * jnp.tanh(0.5 * x) + 0.5


def _rms(x):
    return x * lax.rsqrt(jnp.mean(x * x, axis=-1, keepdims=True) + EPS)


def _dot(a, b):
    return jnp.dot(a, b, preferred_element_type=F32)


def _dot_nt(a, b):
    return lax.dot_general(a, b, (((1,), (1,)), ((), ())), preferred_element_type=F32)


def _mod_kernel(c_ref, w_ref, b_ref, o_ref):
    c = c_ref[...]
    s = (c * _sigmoid(c)).astype(BF16)
    o_ref[...] = _dot(s, w_ref[...].astype(BF16)) + b_ref[...]


def _modulation(cvec, ada_w, ada_b):
    nt = 1536
    return pl.pallas_call(
        _mod_kernel,
        grid=(DEPTH, 6 * D // nt),
        in_specs=[pl.BlockSpec((16, D), lambda l, j: (0, 0)),
                  pl.BlockSpec((None, D, nt), lambda l, j: (l, 0, j)),
                  pl.BlockSpec((None, 1, nt), lambda l, j: (l, 0, j))],
        out_specs=pl.BlockSpec((None, 16, nt), lambda l, j: (l, 0, j)),
        out_shape=jax.ShapeDtypeStruct((DEPTH, 16, 6 * D), F32),
        compiler_params=_cparams(2, 32),
        name="adaln_mod",
    )(cvec, ada_w, ada_b.reshape(DEPTH, 1, 6 * D))


def _nmm_kernel(x_ref, mod_ref, g_ref, w_ref, o_ref, wbf_ref):
    @pl.when(pl.program_id(0) == 0)
    def _():
        wbf_ref[...] = w_ref[...].astype(BF16)

    m = mod_ref[...]
    h = (_rms(x_ref[...]) * g_ref[...] * (1.0 + m[:, D:2 * D]) + m[:, 0:D]).astype(BF16)
    o_ref[...] = _dot(h, wbf_ref[...])


def _norm_mod_matmul(x, mod3, mod_base, rows_per_mod, g, w, tm=1024):
    r, n = x.shape[0], w.shape[1]
    return pl.pallas_call(
        _nmm_kernel,
        grid=(r // tm,),
        in_specs=[pl.BlockSpec((tm, D), lambda i: (i, 0)),
                  pl.BlockSpec((None, 1, 6 * D), lambda i: (mod_base + (i * tm) // rows_per_mod, 0, 0)),
                  pl.BlockSpec((1, D), lambda i: (0, 0)),
                  pl.BlockSpec((D, n), lambda i: (0, 0), pipeline_mode=pl.Buffered(1))],
        out_specs=pl.BlockSpec((tm, n), lambda i: (i, 0)),
        out_shape=jax.ShapeDtypeStruct((r, n), F32),
        scratch_shapes=[pltpu.VMEM((D, n), BF16)],
        compiler_params=_cparams(1, 48),
        name="norm_mod_matmul",
    )(x, mod3, g.reshape(1, D), w)


def _proj_tail(y, x_ref, mod_ref, g2_ref, xo_ref, h2_ref):
    m = mod_ref[...]
    xn = x_ref[...] + m[:, 2 * D:3 * D] * y
    xo_ref[...] = xn
    h2_ref[...] = (_rms(xn) * g2_ref[...] * (1.0 + m[:, 4 * D:5 * D]) + m[:, 3 * D:4 * D]).astype(BF16)


def _proj_kernel(o_ref, x_ref, mod_ref, g2_ref, w_ref, xo_ref, h2_ref, wbf_ref):
    @pl.when(pl.program_id(0) == 0)
    def _():
        wbf_ref[...] = w_ref[...].astype(BF16)

    _proj_tail(_dot(o_ref[...], wbf_ref[...]), x_ref, mod_ref, g2_ref, xo_ref, h2_ref)


def _proj_residual(o, x, mod3, mod_base, rows_per_mod, g2, w, tm=1024):
    r = x.shape[0]
    return pl.pallas_call(
        _proj_kernel,
        grid=(r // tm,),
        in_specs=[pl.BlockSpec((tm, D), lambda i: (i, 0)),
                  pl.BlockSpec((tm, D), lambda i: (i, 0)),
                  pl.BlockSpec((None, 1, 6 * D), lambda i: (mod_base + (i * tm) // rows_per_mod, 0, 0)),
                  pl.BlockSpec((1, D), lambda i: (0, 0)),
                  pl.BlockSpec((D, D), lambda i: (0, 0))],
        out_specs=[pl.BlockSpec((tm, D), lambda i: (i, 0)), pl.BlockSpec((tm, D), lambda i: (i, 0))],
        out_shape=[jax.ShapeDtypeStruct((r, D), F32), jax.ShapeDtypeStruct((r, D), BF16)],
        scratch_shapes=[pltpu.VMEM((D, D), BF16)],
        compiler_params=_cparams(1, 40),
        name="proj_residual",
    )(o, x, mod3, g2.reshape(1, D), w)


def _group_inv_rms(x, on_mxu):
    ss = x * x
    if not on_mxu:
        low = lax.broadcasted_iota(I32, (1, LANES), 1) < HD_A
        s_lo = jnp.sum(jnp.where(low, ss, 0.0), axis=-1, keepdims=True)
        s_hi = jnp.sum(jnp.where(low, 0.0, ss), axis=-1, keepdims=True)
        return jnp.where(low, lax.rsqrt(s_lo * (1.0 / HD_A) + EPS), lax.rsqrt(s_hi * (1.0 / HD_A) + EPS))
    hi = ss.astype(BF16)
    lo = (ss - hi.astype(F32)).astype(BF16)
    r = lax.broadcasted_iota(I32, (LANES, LANES), 0) >> 6
    c = lax.broadcasted_iota(I32, (LANES, LANES), 1) >> 6
    ones_bd = jnp.where(r == c, 1.0, 0.0).astype(BF16)
    return lax.rsqrt((_dot(hi, ones_bd) + _dot(lo, ones_bd)) * (1.0 / HD_A) + EPS)


def _rope128(y, cos, sa, sb):
    return y * cos + pltpu.roll(y, LANES - 16, 1) * sa + pltpu.roll(y, 16, 1) * sb


def _qkv_kernel(*refs, nq, nk, rope, kv_f32, nsub):
    it = iter(refs)
    x_ref, mod_ref, g_ref, w_ref, qg_ref, kg_ref = (next(it) for _ in range(6))
    if rope:
        cos_ref, sa_ref, sb_ref = next(it), next(it), next(it)
    q_ref, k_ref, v_ref = next(it), next(it), next(it)
    kf_ref, vf_ref = (next(it), next(it)) if kv_f32 else (None, None)
    wbf_ref, qkv_ref = next(it), next(it)

    @pl.when(pl.program_id(0) == 0)
    def _():
        wbf_ref[...] = w_ref[...].astype(BF16)

    m = mod_ref[...]
    tm = x_ref.shape[0]
    parts = [slice(i * (tm // nsub), (i + 1) * (tm // nsub)) for i in range(nsub)]
    for rows in parts:
        h = (_rms(x_ref[rows, :]) * g_ref[...] * (1.0 + m[:, D:2 * D]) + m[:, 0:D]).astype(BF16)
        qkv_ref[rows, :] = _dot(h, wbf_ref[...])
    q_gain = qg_ref[...] * (HD_A ** -0.5 * LOG2E)
    for rows in parts:
        if rope:
            cos, sa, sb = cos_ref[rows, :], sa_ref[rows, :], sb_ref[rows, :]
        for j in range((nq + nk) // LANES):
            x = qkv_ref[rows, j * LANES:(j + 1) * LANES]
            is_q = j < nq // LANES
            y = x * _group_inv_rms(x, on_mxu=rope) * (q_gain if is_q else kg_ref[...])
            if rope:
                y = _rope128(y, cos, sa, sb)
            if is_q:
                q_ref[rows, j * LANES:(j + 1) * LANES] = y.astype(BF16)
            else:
                jj = j - nq // LANES
                k_ref[rows, jj * LANES:(jj + 1) * LANES] = y.astype(BF16)
                if kv_f32:
                    kf_ref[rows, jj * LANES:(jj + 1) * LANES] = y
        v = qkv_ref[rows, nq + nk:]
        v_ref[rows, :] = v.astype(BF16)
        if kv_f32:
            vf_ref[rows, :] = v


def _rope_tables(t):
    n_freq = HD_A // 4
    inv_freq = ROPE_THETA ** (-jnp.arange(n_freq, dtype=F32) / n_freq)
    pos_row = jnp.repeat(jnp.arange(t // GRID_W), GRID_W).astype(F32)
    pos_col = jnp.tile(jnp.arange(GRID_W), t // GRID_W).astype(F32)
    ang_r = pos_row[:, None] * inv_freq[None, :]
    ang_c = pos_col[:, None] * inv_freq[None, :]
    z = jnp.zeros_like(ang_r)
    cos64 = jnp.concatenate([jnp.cos(ang_r), jnp.cos(ang_r), jnp.cos(ang_c), jnp.cos(ang_c)], axis=-1)
    sa64 = jnp.concatenate([-jnp.sin(ang_r), z, -jnp.sin(ang_c), z], axis=-1)
    sb64 = jnp.concatenate([z, jnp.sin(ang_r), z, jnp.sin(ang_c)], axis=-1)
    return tuple(jnp.tile(a, (1, 2)) for a in (cos64, sa64, sb64))


def _qkv_project(x, mod3, mod_base, rows_per_mod, g, w, qg, kg, nq, nk, nv, t, rope, kv_f32, tm=512):
    r = x.shape[0]
    n = nq + nk + nv
    tile2 = lambda a: jnp.tile(a.reshape(1, HD_A), (1, 2))
    row = lambda width: pl.BlockSpec((tm, width), lambda i: (i, 0))
    args = [x, mod3, g.reshape(1, D), w, tile2(qg), tile2(kg)]
    in_specs = [row(D),
                pl.BlockSpec((None, 1, 6 * D), lambda i: (mod_base + (i * tm) // rows_per_mod, 0, 0)),
                pl.BlockSpec((1, D), lambda i: (0, 0)),
                pl.BlockSpec((D, n), lambda i: (0, 0), pipeline_mode=pl.Buffered(1)),
                pl.BlockSpec((1, LANES), lambda i: (0, 0)),
                pl.BlockSpec((1, LANES), lambda i: (0, 0))]
    if rope:
        args += list(_rope_tables(t))
        in_specs += [pl.BlockSpec((tm, LANES), lambda i: (i % (t // tm), 0))] * 3
    out_shape = [jax.ShapeDtypeStruct((r, nq), BF16), jax.ShapeDtypeStruct((r, nk), BF16),
                 jax.ShapeDtypeStruct((r, nv), BF16)]
    out_specs = [row(nq), row(nk), row(nv)]
    if kv_f32:
        out_shape += [jax.ShapeDtypeStruct((r, nk), F32), jax.ShapeDtypeStruct((r, nv), F32)]
        out_specs += [row(nk), row(nv)]
    return pl.pallas_call(
        functools.partial(_qkv_kernel, nq=nq, nk=nk, rope=rope, kv_f32=kv_f32, nsub=2),
        grid=(r // tm,),
        in_specs=in_specs, out_specs=out_specs, out_shape=out_shape,
        scratch_shapes=[pltpu.VMEM((D, n), BF16), pltpu.VMEM((tm, n), F32)],
        compiler_params=_cparams(1, 48),
        name="qkv_project",
    )(*args)


def _diff_attn_kernel(*refs, heads, ctx, lam_init, nsub):
    it = iter(refs)
    q_ref, k_ref, v_ref = next(it), next(it), next(it)
    if ctx:
        ck_ref, cv_ref = next(it), next(it)
    lq_ref, lk_ref, sg_ref, o_ref = next(it), next(it), next(it), next(it)
    e = jnp.exp(jnp.sum(lq_ref[...] * lk_ref[...], axis=-1, keepdims=True))
    lam = e[0:1, :] - e[1:2, :] + lam_init
    lane = lax.broadcasted_iota(I32, (1, LANES), 1)
    tqs = q_ref.shape[0] // nsub
    units = [(slice(h * LANES, (h + 1) * LANES), slice(j * tqs, (j + 1) * tqs))
             for h in range(heads) for j in range(nsub)]

    def scores(u):
        sl, rows = units[u]
        q = q_ref[rows, sl]
        out = []
        for c in range(2):
            qc = jnp.where((lane < HD_A) if c == 0 else (lane >= HD_A), q, jnp.zeros_like(q))
            out.append((_dot_nt(qc, k_ref[:, sl]), _dot_nt(qc, ck_ref[:, sl].astype(BF16)) if ctx else None))
        return out

    def weights(sc2):
        ps, ls = [], []
        for s, sc in sc2:
            m = jnp.max(s, axis=-1, keepdims=True)
            if ctx:
                m = jnp.maximum(m, jnp.max(sc, axis=-1, keepdims=True))
            p = jnp.exp2(s - m)
            l = jnp.sum(p, axis=-1, keepdims=True)
            pc = None
            if ctx:
                pc = jnp.exp2(sc - m)
                l = l + jnp.sum(pc, axis=-1, keepdims=True)
            ps.append((p, pc))
            ls.append(l)
        ratio = lam * ls[0] / ls[1]
        w_lat = (ps[0][0] - ratio * ps[1][0]).astype(BF16)
        w_ctx = (ps[0][1] - ratio * ps[1][1]).astype(BF16) if ctx else None
        return w_lat, w_ctx, 1.0 / ls[0]

    def values(u, w):
        sl, rows = units[u]
        w_lat, w_ctx, inv_l0 = w
        o = _dot(w_lat, v_ref[:, sl])
        if ctx:
            o = o + _dot(w_ctx, cv_ref[:, sl].astype(BF16))
        o_ref[rows, sl] = (_rms(o * inv_l0) * sg_ref[...] * (1.0 - lam_init)).astype(BF16)

    if not ctx:
        s_ref = next(it)
        tk = k_ref.shape[0]
        for u in range(len(units)):
            for c, (s, _) in enumerate(scores(u)):
                s_ref[(2 * u + c) * tqs:(2 * u + c + 1) * tqs, :] = s
        s = s_ref[...]
        p = jnp.exp2(s - jnp.max(s, axis=-1, keepdims=True))
        l = jnp.sum(p, axis=-1, keepdims=True)
        for u in range(len(units)):
            r0, r1 = slice(2 * u * tqs, (2 * u + 1) * tqs), slice((2 * u + 1) * tqs, (2 * u + 2) * tqs)
            w = (p[r0, :] - (lam * l[r0, :] / l[r1, :]) * p[r1, :]).astype(BF16)
            values(u, (w, None, 1.0 / l[r0, :]))
        return
    n = len(units)
    sc = {u: scores(u) for u in range(min(2, n))}
    for u in range(n):
        w = weights(sc.pop(u))
        if u + 2 < n:
            sc[u + 2] = scores(u + 2)
        values(u, w)


def _diff_attention(q, k, v, cache, lam_q, lam_k, subln_g, lam_init, b, t, heads, tq, nsub):
    nq = t // tq
    hb = H_A // heads
    args = [q, k, v]
    in_specs = [pl.BlockSpec((tq, heads * LANES), lambda bi, hi, qi: (bi * nq + qi, hi)),
                pl.BlockSpec((t, heads * LANES), lambda bi, hi, qi: (bi, hi)),
                pl.BlockSpec((t, heads * LANES), lambda bi, hi, qi: (bi, hi))]
    if cache is not None:
        ck, cv, la = cache
        p = ck.shape[2]
        args += [ck, cv]
        in_specs += [pl.BlockSpec((None, None, p, heads * LANES), lambda bi, hi, qi: (bi, la, 0, hi))] * 2
    args += [lam_q, lam_k, subln_g.reshape(1, LANES)]
    in_specs += [pl.BlockSpec((2, HD_A), lambda bi, hi, qi: (0, 0)),
                 pl.BlockSpec((2, HD_A), lambda bi, hi, qi: (0, 0)),
                 pl.BlockSpec((1, LANES), lambda bi, hi, qi: (0, 0))]
    return pl.pallas_call(
        functools.partial(_diff_attn_kernel, heads=heads, ctx=cache is not None, lam_init=lam_init, nsub=nsub),
        grid=(b, hb, nq),
        in_specs=in_specs,
        out_specs=pl.BlockSpec((tq, heads * LANES), lambda bi, hi, qi: (bi * nq + qi, hi)),
        out_shape=jax.ShapeDtypeStruct((b * t, H_A * LANES), BF16),
        scratch_shapes=[] if cache is not None else [pltpu.VMEM((2 * heads * tq, t), F32)],
        compiler_params=_cparams(3, 48),
        name="diff_attention",
    )(*args)


def _both_halves(x, half):
    lane_half = lax.broadcasted_iota(I32, (1, LANES), 1) >> 6
    xm = jnp.where(lane_half == half, x, 0.0)
    return xm + pltpu.roll(xm, HD_B, 1)


def _value_with_ones(x, half):
    lane = lax.broadcasted_iota(I32, (1, LANES), 1)
    return jnp.where(lane < HD_B, _both_halves(x, half), jnp.where(lane == HD_B, 1.0, 0.0))


def _gqa_dense_body(q_ref, k_ref, v_ref, sink_ref, o_ref, s_ref, t):
    lane_half = lax.broadcasted_iota(I32, (1, LANES), 1) >> 6
    ns = q_ref.shape[0] // t
    vds, sinks = [], []
    for sq in range(ns):
        seq = slice(sq * t, (sq + 1) * t)
        for kv in range(KV_B):
            kcol = slice((kv // 2) * LANES, (kv // 2 + 1) * LANES)
            kd = _both_halves(k_ref[seq, kcol].astype(F32), kv % 2).astype(BF16)
            vds.append(_value_with_ones(v_ref[seq, kcol].astype(F32), kv % 2).astype(BF16))
            for g in range(G_B):
                h = kv * G_B + g
                q = q_ref[seq, (h // 2) * LANES:(h // 2 + 1) * LANES]
                qm = jnp.where(lane_half == h % 2, q, jnp.zeros_like(q))
                s_ref[(sq * H_B + h) * t:(sq * H_B + h + 1) * t, :] = _dot_nt(qm, kd)
                sinks.append(jnp.broadcast_to(sink_ref[kv:kv + 1, g:g + 1] * LOG2E, (t, 1)))
    s = s_ref[...]
    sink2 = jnp.concatenate(sinks, axis=0)
    m = jnp.maximum(jnp.max(s, axis=-1, keepdims=True), sink2)
    p = jnp.exp2(s - m).astype(BF16)
    tail = jnp.exp2(sink2 - m)
    for sq in range(ns):
        for hp in range(H_B // 2):
            halves = []
            for h in (2 * hp, 2 * hp + 1):
                rows = slice((sq * H_B + h) * t, (sq * H_B + h + 1) * t)
                ov = _dot(p[rows, :], vds[sq * KV_B + h // G_B])
                halves.append(ov * (1.0 / (ov[:, HD_B:HD_B + 1] + tail[rows, :])))
            o_ref[sq * t:(sq + 1) * t, hp * LANES:(hp + 1) * LANES] = jnp.where(
                lane_half == 0, halves[0], pltpu.roll(halves[1], HD_B, 1)).astype(BF16)


def _gqa_kernel(*refs, kvs, windowed, ctx, t, tq):
    it = iter(refs)
    q_ref, k_ref, v_ref = next(it), next(it), next(it)
    if ctx:
        ck_ref, cv_ref = next(it), next(it)
    sink_ref, o_ref = next(it), next(it)
    lane_half = lax.broadcasted_iota(I32, (1, LANES), 1) >> 6
    if not windowed:
        _gqa_dense_body(q_ref, k_ref, v_ref, sink_ref, o_ref, next(it), t)
        return
    kd_ref, vd_ref, ckd_ref, cvd_ref = next(it), next(it), next(it), next(it)

    @pl.when(pl.program_id(2) == 0)
    def _():
        for half in range(2):
            kd_ref[half] = _both_halves(k_ref[...].astype(F32), half).astype(BF16)
            vd_ref[half] = _value_with_ones(v_ref[...].astype(F32), half).astype(BF16)
            ckd_ref[half] = _both_halves(ck_ref[...], half).astype(BF16)
            cvd_ref[half] = _value_with_ones(cv_ref[...], half).astype(BF16)

    span = tq + 2 * WINDOW
    q0 = pl.program_id(2) * tq
    start = pl.multiple_of(jnp.clip(q0 - WINDOW, 0, t - span), LANES)
    qpos = q0 + lax.broadcasted_iota(I32, (tq, span), 0)
    kpos = start + lax.broadcasted_iota(I32, (tq, span), 1)
    valid = jnp.abs(qpos - kpos) <= WINDOW
    s_ref = next(it)
    sinks = []
    for half in range(2):
        kd, ckd = kd_ref[half, pl.ds(start, span), :], ckd_ref[half]
        for g in range(G_B):
            hl = half * G_B + g
            q = q_ref[:, hl // 2 * LANES:(hl // 2 + 1) * LANES]
            qm = jnp.where(lane_half == hl % 2, q, jnp.zeros_like(q))
            s_ref[hl * tq:(hl + 1) * tq, 0:span] = jnp.where(valid, _dot_nt(qm, kd), -jnp.inf)
            s_ref[hl * tq:(hl + 1) * tq, span:] = _dot_nt(qm, ckd)
            sinks.append(jnp.broadcast_to(sink_ref[half:half + 1, g:g + 1] * LOG2E, (tq, 1)))
    s = s_ref[...]
    sink2 = jnp.concatenate(sinks, axis=0)
    m = jnp.maximum(jnp.max(s, axis=-1, keepdims=True), sink2)
    p = jnp.exp2(s - m).astype(BF16)
    tail = jnp.exp2(sink2 - m)
    for hp in range(G_B):
        halves = []
        for hl in (2 * hp, 2 * hp + 1):
            rows, half = slice(hl * tq, (hl + 1) * tq), hl // G_B
            ov = (_dot(p[rows, 0:span], vd_ref[half, pl.ds(start, span), :]) + _dot(p[rows, span:], cvd_ref[half]))
            halves.append(ov * (1.0 / (ov[:, HD_B:HD_B + 1] + tail[rows, :])))
        o_ref[:, hp * LANES:(hp + 1) * LANES] = jnp.where(
            lane_half == 0, halves[0], pltpu.roll(halves[1], HD_B, 1)).astype(BF16)


def _gqa_attention(q, k, v, cache, sink, b, t, windowed, tq):
    if windowed:
        kvs, nq = 2, t // tq
        grid = (b, KV_B // 2, nq)
        qw = 2 * G_B * HD_B
        q_spec = pl.BlockSpec((tq, qw), lambda bi, kv, qi: (bi * nq + qi, kv))
        kv_spec = pl.BlockSpec((t, LANES), lambda bi, kv, qi: (bi, kv))
        sink_arr = sink.reshape(KV_B // 2, 2, G_B)
        sink_spec = pl.BlockSpec((None, 2, G_B), lambda bi, kv, qi: (kv, 0, 0))
        o_spec = pl.BlockSpec((tq, qw), lambda bi, kv, qi: (bi * nq + qi, kv))
    else:
        kvs, tq, ns = KV_B, t, 2
        grid = (b // ns, 1, 1)
        q_spec = pl.BlockSpec((ns * t, H_B * HD_B), lambda bi, kv, qi: (bi, 0))
        kv_spec = pl.BlockSpec((ns * t, KV_B * HD_B), lambda bi, kv, qi: (bi, 0))
        sink_arr = sink.reshape(KV_B, G_B)
        sink_spec = pl.BlockSpec((KV_B, G_B), lambda bi, kv, qi: (0, 0))
        o_spec = pl.BlockSpec((ns * t, H_B * HD_B), lambda bi, kv, qi: (bi, 0))
    args, in_specs = [q, k, v], [q_spec, kv_spec, kv_spec]
    if cache is not None:
        ck, cv, lb = cache
        p = ck.shape[2]
        args += [ck, cv]
        in_specs += [pl.BlockSpec((None, None, p, LANES), lambda bi, kv, qi: (bi, lb, 0, kv))] * 2
    args.append(sink_arr)
    in_specs.append(sink_spec)
    scratch = [pltpu.VMEM((2 * H_B * t, t), F32)]
    if windowed:
        scratch = ([pltpu.VMEM((2, t, LANES), BF16)] * 2 + [pltpu.VMEM((2, p, LANES), BF16)] * 2
                   + [pltpu.VMEM((2 * G_B * tq, tq + 2 * WINDOW + p), F32)])
    return pl.pallas_call(
        functools.partial(_gqa_kernel, kvs=kvs, windowed=windowed, ctx=cache is not None, t=t, tq=tq),
        grid=grid, in_specs=in_specs, out_specs=o_spec,
        out_shape=jax.ShapeDtypeStruct((b * t, H_B * HD_B), BF16),
        scratch_shapes=scratch,
        compiler_params=_cparams(3, 48),
        name="gqa_attention",
    )(*args)


def _lru_conv_gates(xp_ref, xc_ref, xn_ref, cw_ref, cb_ref, wg_ref, bg_ref, lam_ref, a_ref, u_ref, ti, nt, tt):
    nl = -lam_ref[...]
    sp = jnp.maximum(nl, 0.0) + jnp.log1p(jnp.exp(-jnp.abs(nl)))
    cur = xc_ref[...]
    prev = xp_ref[...] * (ti > 0).astype(F32)
    nxt = xn_ref[...] * (ti < nt - 1).astype(F32)
    row = lax.broadcasted_iota(I32, (tt, 1), 0)
    xm1 = jnp.where(row == 0, prev[7:8, :], pltpu.roll(cur, 1, 0))
    xm2 = jnp.where(row == 0, prev[6:7, :], jnp.where(row == 1, prev[7:8, :], pltpu.roll(cur, 2, 0)))
    xp1 = jnp.where(row == tt - 1, nxt[0:1, :], pltpu.roll(cur, tt - 1, 0))
    cw = cw_ref[...]
    xc = xm2 * cw[0:1, :] + xm1 * cw[1:2, :] + cur * cw[2:3, :] + xp1 * cw[3:4, :] + cb_ref[...]
    xcb = xc.astype(BF16)
    for j in range(D_RNN // MXU_DIM):
        cs = slice(j * MXU_DIM, (j + 1) * MXU_DIM)
        z = _dot(xcb[:, cs], wg_ref[j].astype(BF16))
        r = _sigmoid(z[:, :MXU_DIM] + bg_ref[0:1, cs])
        i = _sigmoid(z[:, MXU_DIM:] + bg_ref[1:2, cs])
        a = jnp.exp(-LRU_C * r * sp[:, cs])
        a_ref[:, cs] = a
        u_ref[:, cs] = jnp.sqrt(1.0 - a * a) * i * xc[:, cs]


def _tile_scan(a_ref, u_ref, hs_ref, h, tt, reverse):
    row = lax.broadcasted_iota(I32, (8, 1), 0)
    ng = tt // 8
    for g in (range(ng - 1, -1, -1) if reverse else range(ng)):
        rows = slice(g * 8, (g + 1) * 8)
        a8, u8 = a_ref[rows, :], u_ref[rows, :]
        for s in (1, 2, 4):
            keep = (row < 8 - s) if reverse else (row >= s)
            shift = 8 - s if reverse else s
            u8 = a8 * jnp.where(keep, pltpu.roll(u8, shift, 0), 0.0) + u8
            a8 = a8 * jnp.where(keep, pltpu.roll(a8, shift, 0), 1.0)
        h8 = a8 * h + u8
        hs_ref[rows, :] = h8
        h = h8[0:1, :] if reverse else h8[7:8, :]
    return h


def _lru_fwd_kernel(xp_ref, xc_ref, xn_ref, cw_ref, cb_ref, wg_ref, bg_ref, lam_ref, h0_ref,
                    hf_ref, last_ref, carry_ref, a_ref, u_ref, *, tt):
    ti, nt = pl.program_id(1), pl.num_programs(1)

    @pl.when(ti == 0)
    def _():
        carry_ref[...] = h0_ref[...]

    _lru_conv_gates(xp_ref, xc_ref, xn_ref, cw_ref, cb_ref, wg_ref, bg_ref, lam_ref, a_ref, u_ref, ti, nt, tt)
    h = _tile_scan(a_ref, u_ref, hf_ref, carry_ref[...], tt, reverse=False)
    carry_ref[...] = h
    last_ref[...] = h


def _lru_bwd_kernel(xp_ref, xc_ref, xn_ref, cw_ref, cb_ref, wg_ref, bg_ref, lam_ref, h0_ref,
                    gate_ref, hf_ref, x_ref, mod_ref, g2_ref, w_ref,
                    xo_ref, h2_ref, last_ref, carry_ref, a_ref, u_ref, hb_ref, wbf_ref, *, tt):
    ti, nt = pl.program_id(1), pl.num_programs(1)

    @pl.when((pl.program_id(0) == 0) & (ti == 0))
    def _():
        wbf_ref[...] = w_ref[...].astype(BF16)

    @pl.when(ti == 0)
    def _():
        carry_ref[...] = h0_ref[...]

    tr = nt - 1 - ti
    _lru_conv_gates(xp_ref, xc_ref, xn_ref, cw_ref, cb_ref, wg_ref, bg_ref, lam_ref, a_ref, u_ref, tr, nt, tt)
    h = _tile_scan(a_ref, u_ref, hb_ref, carry_ref[...], tt, reverse=True)
    carry_ref[...] = h
    last_ref[...] = h
    g = gate_ref[...]
    gelu = 0.5 * g * (1.0 + jnp.tanh(math.sqrt(2.0 / math.pi) * (g + 0.044715 * (g * g * g))))
    y = (gelu * (hf_ref[...] + hb_ref[...])).astype(BF16)
    _proj_tail(_dot(y, wbf_ref[...]), x_ref, mod_ref, g2_ref, xo_ref, h2_ref)


def _lru_mixer(gx, x, state, conv_w, conv_b, w_rg, b_rg, w_ig, b_ig, lam, w_out, mod3, mod_base, per_batch_mod,
               g2, b, t, tt=512):
    tt = min(tt, t)
    nt = t // tt
    per_tile = MXU_DIM // BW_C
    eye = jnp.eye(per_tile, dtype=F32)

    def bd(w):
        w4 = w.reshape(N_BLK_C // per_tile, per_tile, BW_C, BW_C)
        return (w4[:, :, :, None, :] * eye[None, :, None, :, None]).reshape(-1, MXU_DIM, MXU_DIM)

    full = lambda shape: pl.BlockSpec(shape, lambda bi, ti: (0,) * len(shape))
    finals, xo, h2, hf = [], None, None, None
    for d in range(2):
        tile = (lambda ti: ti) if d == 0 else (lambda ti: nt - 1 - ti)
        x_spec = lambda f: pl.BlockSpec((tt, D_RNN), f)
        row = lambda bi, ti, tile=tile: (bi * nt + tile(ti), 0)
        halo = lambda f: pl.BlockSpec((8, D_RNN), f)
        in_specs = [halo(lambda bi, ti, tile=tile: (jnp.maximum((bi * nt + tile(ti)) * (tt // 8) - 1, 0), 1)),
                    x_spec(lambda bi, ti, tile=tile: (bi * nt + tile(ti), 1)),
                    halo(lambda bi, ti, tile=tile: (jnp.minimum((bi * nt + tile(ti) + 1) * (tt // 8), b * t // 8 - 1), 1)),
                    full((4, D_RNN)), full((1, D_RNN)), full((4, MXU_DIM, 2 * MXU_DIM)),
                    full((2, D_RNN)), full((1, D_RNN)),
                    pl.BlockSpec((None, 1, D_RNN), lambda bi, ti: (bi, 0, 0))]
        args = [gx, gx, gx, conv_w, conv_b.reshape(1, D_RNN),
                jnp.concatenate([bd(w_rg[d]), bd(w_ig[d])], axis=-1), jnp.stack([b_rg[d], b_ig[d]]),
                lam[d].reshape(1, D_RNN), state[:, d].reshape(b, 1, D_RNN)]
        scratch = [pltpu.VMEM((1, D_RNN), F32), pltpu.VMEM((tt, D_RNN), F32), pltpu.VMEM((tt, D_RNN), F32)]
        last_spec = pl.BlockSpec((None, 1, D_RNN), lambda bi, ti: (bi, 0, 0))
        last_shape = jax.ShapeDtypeStruct((b, 1, D_RNN), F32)
        if d == 0:
            hf, last_f = pl.pallas_call(
                functools.partial(_lru_fwd_kernel, tt=tt),
                grid=(b, nt), in_specs=in_specs,
                out_specs=[pl.BlockSpec((tt, D_RNN), row), last_spec],
                out_shape=[jax.ShapeDtypeStruct((b * t, D_RNN), F32), last_shape],
                scratch_shapes=scratch,
                compiler_params=_cparams(2, 48),
                name="lru_fwd",
            )(*args)
        else:
            in_specs += [pl.BlockSpec((tt, D_RNN), row), pl.BlockSpec((tt, D_RNN), row), pl.BlockSpec((tt, D), row),
                         pl.BlockSpec((None, 1, 6 * D), lambda bi, ti: (mod_base + bi * per_batch_mod, 0, 0)),
                         full((1, D)), full((D_RNN, D))]
            args += [gx, hf, x, mod3, g2.reshape(1, D), w_out]
            xo, h2, last_b = pl.pallas_call(
                functools.partial(_lru_bwd_kernel, tt=tt),
                grid=(b, nt), in_specs=in_specs,
                out_specs=[pl.BlockSpec((tt, D), row), pl.BlockSpec((tt, D), row), last_spec],
                out_shape=[jax.ShapeDtypeStruct((b * t, D), F32), jax.ShapeDtypeStruct((b * t, D), BF16), last_shape],
                scratch_shapes=scratch + [pltpu.VMEM((tt, D_RNN), F32), pltpu.VMEM((D_RNN, D), BF16)],
                compiler_params=_cparams(2, 48),
                name="lru_bwd_proj",
            )(*args)
    return xo, h2, jnp.concatenate([last_f, last_b], axis=1)


def _select_kernel(h_ref, wr_ref, slot_ref, aff_ref, tri_ref, *, bs, t, cap):
    @pl.when(pl.program_id(0) == 0)
    def _():
        r = lax.broadcasted_iota(I32, (t, t), 0)
        c = lax.broadcasted_iota(I32, (t, t), 1)
        tri_ref[...] = jnp.where(r < c, 1.0, 0.0).astype(BF16)

    wr = wr_ref[...].astype(BF16)
    affs = []
    for s in range(bs):
        logits = _dot_nt(wr, h_ref[s])
        ex = jnp.exp(logits - jnp.max(logits, axis=0, keepdims=True))
        affs.append(ex / jnp.sum(ex, axis=0, keepdims=True))
    aff = jnp.concatenate(affs, axis=0) if bs > 1 else affs[0]
    bits = pltpu.bitcast(aff, I32)
    count = lambda mask: jnp.sum(jnp.where(mask, 1.0, 0.0), axis=-1, keepdims=True)
    th = jnp.zeros((bs * N_EXPERTS, 1), I32)
    for bit in range(30, -1, -1):
        cand = th | (1 << bit)
        th = jnp.where(count(bits >= cand) >= cap, cand, th)
    gt, eq = bits > th, bits == th
    need = cap - count(gt)
    lane = lax.broadcasted_iota(I32, (1, t), 1)
    lim = jnp.zeros((bs * N_EXPERTS, 1), I32)
    for bit in range(t.bit_length() - 1, -1, -1):
        cand = lim | (1 << bit)
        ok = (cand <= t) & (count(eq & (lane < cand)) <= need)
        lim = jnp.where(ok, cand, lim)
    sel = gt | (eq & (lane < lim))
    pos = _dot(jnp.where(sel, 1.0, 0.0).astype(BF16), tri_ref[...])
    slot = jnp.where(sel, pos.astype(I32), -1)
    for s in range(bs):
        slot_ref[s] = slot[s * N_EXPERTS:(s + 1) * N_EXPERTS, :]
        aff_ref[s] = aff[s * N_EXPERTS:(s + 1) * N_EXPERTS, :]


def _moe_select(h2, w_router, b, t, cap, bs):
    return pl.pallas_call(
        functools.partial(_select_kernel, bs=bs, t=t, cap=cap),
        grid=(b // bs,),
        in_specs=[pl.BlockSpec((bs, t, D), lambda i: (i, 0, 0)),
                  pl.BlockSpec((N_EXPERTS, D), lambda i: (0, 0))],
        out_specs=[pl.BlockSpec((bs, N_EXPERTS, t), lambda i: (i, 0, 0))] * 2,
        out_shape=[jax.ShapeDtypeStruct((b, N_EXPERTS, t), I32), jax.ShapeDtypeStruct((b, N_EXPERTS, t), F32)],
        scratch_shapes=[pltpu.VMEM((t, t), BF16)],
        compiler_params=_cparams(1, 48),
        name="moe_select",
    )(h2, w_router.T)


def _gather_kernel(slot_ref, aff_ref, h_ref, xs_ref, gc_ref, *, bq, cap, eg):
    j = lax.broadcasted_iota(I32, (cap, 1), 0)
    for s in range(bq):
        h = h_ref[s]
        for e0 in range(0, N_EXPERTS, eg):
            hots = [slot_ref[s, e:e + 1, :] == j for e in range(e0, e0 + eg)]
            p = jnp.concatenate([jnp.where(o, 1.0, 0.0).astype(BF16) for o in hots], axis=0)
            xs = _dot(p, h).astype(BF16)
            for k, e in enumerate(range(e0, e0 + eg)):
                xs_ref[e, s * cap:(s + 1) * cap, :] = xs[k * cap:(k + 1) * cap, :]
                gc_ref[e, s * cap:(s + 1) * cap, :] = jnp.sum(jnp.where(hots[k], aff_ref[s, e:e + 1, :], 0.0),
                                                              axis=-1, keepdims=True)


def _moe_gather(slot, aff, h3, b, t, cap, bq, eg):
    sa_spec = pl.BlockSpec((bq, N_EXPERTS, t), lambda i: (i, 0, 0))
    return pl.pallas_call(
        functools.partial(_gather_kernel, bq=bq, cap=cap, eg=eg),
        grid=(b // bq,),
        in_specs=[sa_spec, sa_spec, pl.BlockSpec((bq, t, D), lambda i: (i, 0, 0))],
        out_specs=[pl.BlockSpec((N_EXPERTS, bq * cap, D), lambda i: (0, i, 0)),
                   pl.BlockSpec((N_EXPERTS, bq * cap, 1), lambda i: (0, i, 0))],
        out_shape=[jax.ShapeDtypeStruct((N_EXPERTS, b * cap, D), BF16),
                   jax.ShapeDtypeStruct((N_EXPERTS, b * cap, 1), F32)],
        compiler_params=_cparams(1, 56),
        name="moe_gather",
    )(slot, aff, h3)


def _ffn_kernel(xs_ref, gc_ref, wg_ref, wu_ref, wd_ref, ys_ref, wgb_ref, wub_ref, wdb_ref):
    @pl.when(pl.program_id(1) == 0)
    def _():
        wgb_ref[...] = wg_ref[...].astype(BF16)
        wub_ref[...] = wu_ref[...].astype(BF16)
        wdb_ref[...] = wd_ref[...].astype(BF16)

    xs = xs_ref[...]
    acc = None
    for c in range(0, wgb_ref.shape[1], MXU_DIM):
        cs = slice(c, c + MXU_DIM)
        zg = _dot(xs, wgb_ref[:, cs])
        hid = (zg * _sigmoid(zg) * _dot(xs, wub_ref[:, cs])).astype(BF16)
        part = _dot(hid, wdb_ref[cs, :])
        acc = part if acc is None else acc + part
    ys_ref[...] = (acc * gc_ref[...]).astype(BF16)


def _moe_ffn(xs, gc, w_gate, w_up, w_down, layer, tmf=1024):
    r, f = xs.shape[1], w_gate.shape[-1]
    w_spec = lambda d0, d1: pl.BlockSpec((None, None, d0, d1), lambda e, i: (layer, e, 0, 0))
    return pl.pallas_call(
        _ffn_kernel,
        grid=(N_EXPERTS, r // tmf),
        in_specs=[pl.BlockSpec((None, tmf, D), lambda e, i: (e, i, 0)),
                  pl.BlockSpec((None, tmf, 1), lambda e, i: (e, i, 0)),
                  w_spec(D, f), w_spec(D, f), w_spec(f, D)],
        out_specs=pl.BlockSpec((None, tmf, D), lambda e, i: (e, i, 0)),
        out_shape=jax.ShapeDtypeStruct((N_EXPERTS, r, D), BF16),
        scratch_shapes=[pltpu.VMEM((D, f), BF16), pltpu.VMEM((D, f), BF16), pltpu.VMEM((f, D), BF16)],
        compiler_params=_cparams(2, 56),
        name="moe_ffn",
    )(xs, gc, w_gate, w_up, w_down)


def _scatter_rows(slot, width, ys):
    n = N_EXPERTS * width
    e_of = lax.broadcasted_iota(I32, (N_EXPERTS, n), 1) >> (width.bit_length() - 1)
    rep = jnp.where(e_of == lax.broadcasted_iota(I32, (N_EXPERTS, n), 0), 1.0, 0.0).astype(BF16)
    slot_rep = _dot(slot.astype(F32).astype(BF16), rep)
    jn = (lax.broadcasted_iota(I32, (1, n), 1) & (width - 1)).astype(F32)
    return _dot(jnp.where(slot_rep == jn, 1.0, 0.0).astype(BF16), ys)


def _combine_kernel(slot_ref, ys_ref, x_ref, mod_ref, o_ref, *, cap):
    y = _scatter_rows(slot_ref[...], cap, ys_ref[...].reshape(N_EXPERTS * cap, D))
    o_ref[...] = x_ref[...] + mod_ref[:, 5 * D:6 * D] * y


def _combine_win_kernel(st_s, ok_s, slot_ref, stv_ref, ys_ref, x_ref, mod_ref, o_ref, *ysw_refs, cap, win, nt, tt):
    ga2 = mod_ref[:, 5 * D:6 * D]
    for sub, ysw_ref in enumerate(ysw_refs):
        tile = (pl.program_id(0) * nt + pl.program_id(1)) * len(ysw_refs) + sub
        rows = slice(sub * tt, (sub + 1) * tt)

        @pl.when(ok_s[tile] != 0)
        def _(tile=tile, rows=rows, ysw_ref=ysw_ref, sub=sub):
            for e in range(N_EXPERTS):
                st = pl.multiple_of(st_s[tile * N_EXPERTS + e], 16)
                ysw_ref[e * win:(e + 1) * win, :] = ys_ref[e, pl.ds(st, win), :]
            slot = slot_ref[rows, :]
            rel = jnp.where(slot >= 0, slot - stv_ref[sub], -1)
            o_ref[rows, :] = x_ref[rows, :] + ga2 * _scatter_rows(rel, win, ysw_ref[...])

        @pl.when(ok_s[tile] == 0)
        def _(rows=rows):
            y = _scatter_rows(slot_ref[rows, :], cap, ys_ref[...].reshape(N_EXPERTS * cap, D))
            o_ref[rows, :] = x_ref[rows, :] + ga2 * y


def _moe_combine_windowed(slot, ys, x, mod3, mod_base, per_batch_mod, b, t, cap, tt=256, win=64, nsub=2):
    nt = t // tt
    cnt = (slot >= 0).reshape(b, N_EXPERTS, nt, tt).sum(-1).astype(I32)
    start = jnp.cumsum(cnt, axis=-1) - cnt
    st = jnp.clip((start // 16) * 16, 0, cap - win)
    ok = jnp.all(start + cnt <= st + win, axis=1).astype(I32)
    st_t = jnp.swapaxes(st, 1, 2)
    ng = nt // nsub
    grid_spec = pltpu.PrefetchScalarGridSpec(
        num_scalar_prefetch=2,
        grid=(b, ng),
        in_specs=[pl.BlockSpec((None, nsub * tt, N_EXPERTS), lambda bi, ti, *_: (bi, ti, 0)),
                  pl.BlockSpec((None, nsub, 1, N_EXPERTS), lambda bi, ti, *_: (bi, ti, 0, 0)),
                  pl.BlockSpec((N_EXPERTS, None, cap, D), lambda bi, ti, *_: (0, bi, 0, 0)),
                  pl.BlockSpec((nsub * tt, D), lambda bi, ti, *_: (bi * ng + ti, 0)),
                  pl.BlockSpec((None, 1, 6 * D), lambda bi, ti, *_: (mod_base + bi * per_batch_mod, 0, 0))],
        out_specs=pl.BlockSpec((nsub * tt, D), lambda bi, ti, *_: (bi * ng + ti, 0)),
        scratch_shapes=[pltpu.VMEM((N_EXPERTS * win, D), BF16)] * nsub)
    return pl.pallas_call(
        functools.partial(_combine_win_kernel, cap=cap, win=win, nt=ng, tt=tt),
        grid_spec=grid_spec,
        out_shape=jax.ShapeDtypeStruct((b * t, D), F32),
        compiler_params=_cparams(2, 56),
        name="moe_combine_win",
    )(st_t.reshape(-1), ok.reshape(-1), jnp.swapaxes(slot, 1, 2), st_t.reshape(b, nt, 1, N_EXPERTS), ys, x, mod3)


def _moe_combine(slot_t, ys, x, mod3, mod_base, per_batch_mod, b, t, cap, tt=512):
    tt = min(tt, t)
    nt = t // tt
    return pl.pallas_call(
        functools.partial(_combine_kernel, cap=cap),
        grid=(b, nt),
        in_specs=[pl.BlockSpec((None, tt, N_EXPERTS), lambda bi, ti: (bi, ti, 0)),
                  pl.BlockSpec((N_EXPERTS, None, cap, D), lambda bi, ti: (0, bi, 0, 0)),
                  pl.BlockSpec((tt, D), lambda bi, ti: (bi * nt + ti, 0)),
                  pl.BlockSpec((None, 1, 6 * D), lambda bi, ti: (mod_base + bi * per_batch_mod, 0, 0))],
        out_specs=pl.BlockSpec((tt, D), lambda bi, ti: (bi * nt + ti, 0)),
        out_shape=jax.ShapeDtypeStruct((b * t, D), F32),
        compiler_params=_cparams(2, 48),
        name="moe_combine",
    )(slot_t, ys, x, mod3)


def _ec_moe(x, h2, mod3, mod_base, per_batch_mod, moe_w, layer, b, t, bs, bg):
    w_router, w_gate, w_up, w_down = moe_w
    cap = EC_FACTOR * t // N_EXPERTS
    h3 = h2.reshape(b, t, D)
    slot, aff = _moe_select(h3, w_router[layer], b, t, cap, bs)
    xs, gc = _moe_gather(slot, aff, h3, b, t, cap, bq=bg, eg=4 if cap >= MXU_DIM else N_EXPERTS)
    ys = _moe_ffn(xs, gc, w_gate, w_up, w_down, layer).reshape(N_EXPERTS, b, cap, D)
    if cap >= MXU_DIM:
        return _moe_combine_windowed(slot, ys, x, mod3, mod_base, per_batch_mod, b, t, cap)
    return _moe_combine(jnp.swapaxes(slot, 1, 2), ys, x, mod3, mod_base, per_batch_mod, b, t, cap)


def kernel(x_prompt, x_sample, cache_a_k, cache_a_v, cache_b_k, cache_b_v, state_c_h, c, c_ctx, ada_w, ada_b, norm_mix_g, norm_ffn_g, a_w_in, a_q_norm_g, a_k_norm_g, a_lam_q, a_lam_k, a_subln_g, a_w_out, b_w_in, b_q_norm_g, b_k_norm_g, b_sink, b_w_out, c_w_in, c_conv_w, c_conv_b, c_w_rg, c_b_rg, c_w_ig, c_b_ig, c_lam, c_w_out, moe_w_router, moe_w_gate, moe_w_up, moe_w_down):
    bp, tp, _ = x_prompt.shape
    bs_, ts, _ = x_sample.shape
    past = cache_a_k.shape[2]
    cvec = jnp.concatenate([c, c_ctx[None, :], jnp.zeros((16 - bs_ - 1, D), F32)], axis=0)
    mods = _modulation(cvec, ada_w, ada_b)
    groups = {"p": (bp, tp, bs_, 0), "s": (bs_, ts, 0, 1)}
    xs = {"p": x_prompt.reshape(bp * tp, D), "s": x_sample.reshape(bs_ * ts, D)}
    ck_a = cache_a_k.reshape(bs_, -1, past, H_A * 2 * HD_A)
    cv_a = cache_a_v.reshape(bs_, -1, past, H_A * 2 * HD_A)
    ck_b = cache_b_k.reshape(bs_, -1, past, KV_B * HD_B)
    cv_b = cache_b_v.reshape(bs_, -1, past, KV_B * HD_B)
    moe_w = (moe_w_router, moe_w_gate, moe_w_up, moe_w_down)
    new_a_k, new_a_v, new_b_k, new_b_v, new_c_h = [], [], [], [], []
    ia = ib = ic = 0
    for l in range(DEPTH):
        mod3 = mods[l].reshape(16, 1, 6 * D)
        kind = l % 3
        for key in ("p", "s"):
            b, t, mbase, per_b = groups[key]
            rows_per_mod = t if per_b else b * t
            x = xs[key]
            sample = key == "s"
            if kind == 0:
                lam_init = 0.8 - 0.6 * math.exp(-0.3 * l)
                outs = _qkv_project(x, mod3, mbase, rows_per_mod, norm_mix_g[l], a_w_in[ia], a_q_norm_g[ia],
                                    a_k_norm_g[ia], D, D, D, t, rope=sample, kv_f32=not sample)
                q, k, v = outs[:3]
                if not sample:
                    new_a_k.append(outs[3].reshape(b, t, H_A, 2 * HD_A))
                    new_a_v.append(outs[4].reshape(b, t, H_A, 2 * HD_A))
                o = _diff_attention(q, k, v, (ck_a, cv_a, ia) if sample else None, a_lam_q[ia], a_lam_k[ia],
                                    a_subln_g[ia], lam_init, b, t, heads=1 if sample else H_A,
                                    tq=2048 if sample else t, nsub=16 if sample else 1)
                x, h2 = _proj_residual(o, x, mod3, mbase, rows_per_mod, norm_ffn_g[l], a_w_out[ia])
            elif kind == 1:
                nq, nk = H_B * HD_B, KV_B * HD_B
                outs = _qkv_project(x, mod3, mbase, rows_per_mod, norm_mix_g[l], b_w_in[ib], b_q_norm_g[ib],
                                    b_k_norm_g[ib], nq, nk, nk, t, rope=sample, kv_f32=not sample)
                q, k, v = outs[:3]
                if not sample:
                    new_b_k.append(outs[3].reshape(b, t, KV_B, HD_B))
                    new_b_v.append(outs[4].reshape(b, t, KV_B, HD_B))
                o = _gqa_attention(q, k, v, (ck_b, cv_b, ib) if sample else None, b_sink[ib], b, t,
                                   windowed=sample, tq=256)
                x, h2 = _proj_residual(o, x, mod3, mbase, rows_per_mod, norm_ffn_g[l], b_w_out[ib])
            else:
                gx = _norm_mod_matmul(x, mod3, mbase, rows_per_mod, norm_mix_g[l], c_w_in[ic])
                state = state_c_h[:, ic] if sample else jnp.zeros((b, 2, D_RNN), F32)
                x, h2, finals = _lru_mixer(gx, x, state, c_conv_w[ic], c_conv_b[ic], c_w_rg[ic], c_b_rg[ic],
                                           c_w_ig[ic], c_b_ig[ic], c_lam[ic], c_w_out[ic], mod3, mbase, per_b,
                                           norm_ffn_g[l], b, t)
                if not sample:
                    new_c_h.append(finals)
            xs[key] = _ec_moe(x, h2, mod3, mbase, per_b, moe_w, l, b, t,
                              bs=4 if sample else 16, bg=1 if sample else 8)
        ia, ib, ic = ia + (kind == 0), ib + (kind == 1), ic + (kind == 2)
    return (xs["p"].reshape(bp, tp, D), xs["s"].reshape(bs_, ts, D),
            jnp.stack(new_a_k, axis=1), jnp.stack(new_a_v, axis=1),
            jnp.stack(new_b_k, axis=1), jnp.stack(new_b_v, axis=1), jnp.stack(new_c_h, axis=1))
```
